```python
import math
import jax
import jax.numpy as jnp
from jax import lax
import numpy as np

D_MODEL = 1024
BATCH = 8
SEQ = 2048
DEPTH = 2
DEC_BATCH = 32
DEC_SEQ = 1
PAST_LEN = 16384
PAGE_SIZE = 128

N_A_LAYERS = DEPTH // 2
N_B_LAYERS = DEPTH - N_A_LAYERS
ML_HEADS = 4
ML_DK = 128
ML_DV = 256
ML_CHUNK = 128
ML_IN_W = 2 * ML_HEADS * ML_DK + 2 * ML_HEADS * ML_DV + 2 * ML_HEADS
ML_FORGET_BIAS = 3.0
M_INIT = -1e30
MLA_HEADS = 8
NOPE_DIM = 128
ROPE_DIM = 64
V_DIM = 128
Q_LORA = 384
KV_LORA = 256
ROPE_THETA = 10000.0
Q_BLOCK = 128
N_EXPERTS = 64
TOP_K = 8
N_GROUPS = 8
TOPK_GROUPS = 4
EXPERT_FF = 256
SHARED_FF = 256
ROUTED_SCALE = 2.5
EXPERT_BLOCK = 128
NORM_EPS = 1e-6

kernel_name = 'yoco_mlstm_mla_moe_adaln_step'


def rmsnorm(x, g):
    xf = x.astype(jnp.float32)
    y = xf * lax.rsqrt(jnp.mean(xf * xf, axis=-1, keepdims=True) + NORM_EPS)
    return (y * g.astype(jnp.float32)).astype(x.dtype)


def modulate(h, shift, scale):
    return h * (1.0 + scale) + shift


def rope(x, pos):
    half = x.shape[-1] // 2
    inv = ROPE_THETA ** (-jnp.arange(half, dtype=jnp.float32) / half)
    ang = pos.astype(jnp.float32)[:, None] * inv[None, :]
    cos = jnp.cos(ang)[None, :, None, :]
    sin = jnp.sin(ang)[None, :, None, :]
    xf = x.astype(jnp.float32)
    x1, x2 = xf[..., :half], xf[..., half:]
    return jnp.concatenate([x1 * cos - x2 * sin, x2 * cos + x1 * sin], axis=-1).astype(x.dtype)


def mlstm_chunkwise(q, k, v, log_i, log_f, C0, n0, m0):
    B, T, H, _ = q.shape
    L = T if T <= ML_CHUNK else math.gcd(T, ML_CHUNK)
    nc = T // L
    causal = jnp.tril(jnp.ones((L, L), dtype=bool))

    def to_chunks(a):
        return a.reshape(B, nc, L, *a.shape[2:]).swapaxes(0, 1)

    def step(carry, inp):
        C, n, m = carry
        qc, kc, vc, ic, fc = inp
        b = jnp.cumsum(fc, axis=1)
        g = b + m[:, None, :]
        D = b[:, :, None, :] - b[:, None, :, :] + ic[:, None, :, :]
        D = jnp.where(causal[None, :, :, None], D, -jnp.inf)
        m_t = jnp.maximum(g, jnp.max(D, axis=2))
        w_intra = jnp.exp(D - m_t[:, :, None, :])
        w_inter = jnp.exp(g - m_t)
        s = jnp.einsum('bthd,bshd->btsh', qc, kc) * w_intra
        num = jnp.einsum('btsh,bshv->bthv', s, vc) + w_inter[..., None] * jnp.einsum('bthd,bhdv->bthv', qc, C)
        den = jnp.sum(s, axis=2) + w_inter * jnp.einsum('bthd,bhd->bth', qc, n)
        h = num / jnp.maximum(jnp.abs(den), jnp.exp(-m_t))[..., None]
        bL = b[:, -1, :]
        m_new = m_t[:, -1, :]
        a_inter = jnp.exp(bL + m - m_new)
        a_intra = jnp.exp(bL[:, None, :] - b + ic - m_new[:, None, :])
        C_new = a_inter[..., None, None] * C + jnp.einsum('bsh,bshd,bshv->bhdv', a_intra, kc, vc)
        n_new = a_inter[..., None] * n + jnp.einsum('bsh,bshd->bhd', a_intra, kc)
        return (C_new, n_new, m_new), h

    (C, n, m), hs = lax.scan(step, (C0, n0, m0),
                             (to_chunks(q), to_chunks(k), to_chunks(v), to_chunks(log_i), to_chunks(log_f)))
    return hs.swapaxes(0, 1).reshape(B, T, H, v.shape[-1]), C, n, m


def mlstm_mixer(h, w_in, b_gates, g_head, w_out, C0, n0, m0):
    B, T, _ = h.shape
    hk = ML_HEADS * ML_DK
    hv = ML_HEADS * ML_DV
    proj = h @ w_in
    q = proj[..., :hk].reshape(B, T, ML_HEADS, ML_DK).astype(jnp.float32)
    k = proj[..., hk:2 * hk].reshape(B, T, ML_HEADS, ML_DK).astype(jnp.float32) * (ML_DK ** -0.5)
    v = proj[..., 2 * hk:2 * hk + hv].reshape(B, T, ML_HEADS, ML_DV).astype(jnp.float32)
    o_gate = proj[..., 2 * hk + hv:2 * hk + 2 * hv]
    gates = (proj[..., 2 * hk + 2 * hv:] + b_gates).astype(jnp.float32)
    log_i = gates[..., :ML_HEADS]
    log_f = jax.nn.log_sigmoid(gates[..., ML_HEADS:])
    hh, C, n, m = mlstm_chunkwise(q, k, v, log_i, log_f, C0.astype(jnp.float32),
                                  n0.astype(jnp.float32), m0.astype(jnp.float32))
    hh = rmsnorm(hh, g_head.reshape(ML_HEADS, ML_DV)).reshape(B, T, hv)
    out = (hh.astype(h.dtype) * jax.nn.sigmoid(o_gate)) @ w_out
    return out, C, n, m


def shared_kv(x, cs, pos, w_ada_kv, b_ada_kv, g_kv_in, w_dkv, g_kv_lat):
    mod = (cs @ w_ada_kv + b_ada_kv)[:, None, :]
    sh, sc = jnp.split(mod, 2, axis=-1)
    h = modulate(rmsnorm(x, g_kv_in), sh, sc)
    ckr = h @ w_dkv
    lat = rmsnorm(ckr[..., :KV_LORA], g_kv_lat)
    kr = rope(ckr[..., None, KV_LORA:], pos)[:, :, 0, :]
    return lat, kr


def mla_attend(q_abs, q_rope, lat, kr, q_pos, k_pos):
    B, Tq, H, C = q_abs.shape
    qb = Q_BLOCK if Tq % Q_BLOCK == 0 else Tq
    nb = Tq // qb
    scale = (NOPE_DIM + ROPE_DIM) ** -0.5

    def split(a):
        return a.reshape(B, nb, qb, *a.shape[2:]).swapaxes(0, 1)

    def one_block(args):
        qa, qr, qp = args
        s = (jnp.einsum('bqhc,bkc->bhqk', qa, lat) + jnp.einsum('bqhr,bkr->bhqk', qr, kr)).astype(jnp.float32) * scale
        mask = k_pos[None, :] <= qp[:, None]
        p = jax.nn.softmax(jnp.where(mask[None, None], s, -jnp.inf), axis=-1).astype(lat.dtype)
        return jnp.einsum('bhqk,bkc->bqhc', p, lat)

    o = lax.map(one_block, (split(q_abs), split(q_rope), q_pos.reshape(nb, qb)))
    return o.swapaxes(0, 1).reshape(B, Tq, H, C)


def mla_mixer(h, pos, k_pos, lat, kr, w_dq, g_q_lat, w_uq, w_uk, w_uv, w_out):
    B, T, _ = h.shape
    q_lat = rmsnorm(h @ w_dq, g_q_lat)
    q = (q_lat @ w_uq).reshape(B, T, MLA_HEADS, NOPE_DIM + ROPE_DIM)
    q_rope = rope(q[..., NOPE_DIM:], pos)
    q_abs = jnp.einsum('bthn,chn->bthc', q[..., :NOPE_DIM], w_uk)
    o_lat = mla_attend(q_abs, q_rope, lat, kr, pos, k_pos)
    o = jnp.einsum('bthc,chv->bthv', o_lat, w_uv).reshape(B, T, MLA_HEADS * V_DIM)
    return o @ w_out


def swiglu(x, w_gate, w_up, w_down):
    return (jax.nn.silu(x @ w_gate) * (x @ w_up)) @ w_down


def routed_experts(xf, idx, wts, w_gate, w_up, w_down):
    n_tok, d = xf.shape
    n_assign = n_tok * TOP_K
    n_blocks = (n_assign + N_EXPERTS * (EXPERT_BLOCK - 1) + EXPERT_BLOCK - 1) // EXPERT_BLOCK
    n_rows = n_blocks * EXPERT_BLOCK
    flat_e = idx.reshape(-1)
    flat_tok = jnp.arange(n_assign, dtype=jnp.int32) // TOP_K
    flat_w = wts.reshape(-1)
    order = jnp.argsort(flat_e)
    e_s = flat_e[order]
    counts = jnp.bincount(flat_e, length=N_EXPERTS)
    padded = (counts + EXPERT_BLOCK - 1) // EXPERT_BLOCK * EXPERT_BLOCK
    pad_end = jnp.cumsum(padded)
    pad_start = pad_end - padded
    unpad_start = jnp.cumsum(counts) - counts
    dest = pad_start[e_s] + jnp.arange(n_assign, dtype=jnp.int32) - unpad_start[e_s]
    row_tok = jnp.full((n_rows,), n_tok, jnp.int32).at[dest].set(flat_tok[order])
    row_w = jnp.zeros((n_rows,), xf.dtype).at[dest].set(flat_w[order].astype(xf.dtype))
    block_start = jnp.arange(n_blocks, dtype=jnp.int32) * EXPERT_BLOCK
    block_e = jnp.minimum(jnp.searchsorted(pad_end, block_start, side='right'), N_EXPERTS - 1)
    x_pad = jnp.concatenate([xf, jnp.zeros((1, d), xf.dtype)], axis=0)

    def one_block(args):
        tok, w, e = args
        xb = x_pad[tok]
        hb = jax.nn.silu(xb @ w_gate[e]) * (xb @ w_up[e])
        return (hb @ w_down[e]) * w[:, None]

    out = lax.map(one_block, (row_tok.reshape(n_blocks, EXPERT_BLOCK), row_w.reshape(n_blocks, EXPERT_BLOCK), block_e))
    return jax.ops.segment_sum(out.reshape(n_rows, d), row_tok, num_segments=n_tok + 1)[:n_tok]


def moe(h, w_router, b_router, w_gate, w_up, w_down, ws_gate, ws_up, ws_down):
    shp = h.shape
    xf = h.reshape(-1, shp[-1])
    n_tok = xf.shape[0]
    scores = jax.nn.sigmoid((xf @ w_router).astype(jnp.float32))
    sel = scores + b_router.astype(jnp.float32)
    grp = sel.reshape(n_tok, N_GROUPS, N_EXPERTS // N_GROUPS)
    gscore = jnp.sum(lax.top_k(grp, 2)[0], axis=-1)
    _, gidx = lax.top_k(gscore, TOPK_GROUPS)
    gmask = jnp.sum(jax.nn.one_hot(gidx, N_GROUPS, dtype=jnp.float32), axis=1) > 0
    sel = jnp.where(jnp.repeat(gmask, N_EXPERTS // N_GROUPS, axis=1), sel, -jnp.inf)
    _, idx = lax.top_k(sel, TOP_K)
    wts = jnp.take_along_axis(scores, idx, axis=1)
    wts = wts / jnp.sum(wts, axis=-1, keepdims=True) * ROUTED_SCALE
    routed = routed_experts(xf, idx, wts, w_gate, w_up, w_down)
    shared = swiglu(xf, ws_gate, ws_up, ws_down)
    return (routed + shared).reshape(shp)


def run_group(x, c, pos0, C0s, n0s, m0s, past_lat, past_kr, p):
    B, T, _ = x.shape
    pos = pos0 + jnp.arange(T, dtype=jnp.int32)
    k_pos = jnp.arange(past_lat.shape[1] + T, dtype=jnp.int32)
    cs = jax.nn.silu(c)
    Cs, ns, ms = [], [], []
    lat_all = kr_all = new_lat = new_kr = None
    for l in range(DEPTH):
        mod = (cs @ p['w_ada'][l] + p['b_ada'][l])[:, None, :]
        sh_m, sc_m, g_m, sh_f, sc_f, g_f = jnp.split(mod, 6, axis=-1)
        h = modulate(rmsnorm(x, p['g_mix'][l]), sh_m, sc_m)
        if l < N_A_LAYERS:
            out, Cn, nn_, mn = mlstm_mixer(h, p['w_mlstm_in'][l], p['b_mlstm_gates'][l], p['g_mlstm_head'][l],
                                           p['w_mlstm_out'][l], C0s[l], n0s[l], m0s[l])
            Cs.append(Cn)
            ns.append(nn_)
            ms.append(mn)
        else:
            if l == N_A_LAYERS:
                new_lat, new_kr = shared_kv(x, cs, pos, p['w_ada_kv'], p['b_ada_kv'], p['g_kv_in'],
                                            p['w_dkv'], p['g_kv_lat'])
                lat_all = jnp.concatenate([past_lat.astype(new_lat.dtype), new_lat], axis=1)
                kr_all = jnp.concatenate([past_kr.astype(new_kr.dtype), new_kr], axis=1)
            j = l - N_A_LAYERS
            out = mla_mixer(h, pos, k_pos, lat_all, kr_all, p['w_dq'][j], p['g_q_lat'][j], p['w_uq'][j],
                            p['w_uk'], p['w_uv'], p['w_mla_out'][j])
        x = x + g_m * out
        h = modulate(rmsnorm(x, p['g_ffn'][l]), sh_f, sc_f)
        x = x + g_f * moe(h, p['w_router'][l], p['b_router'][l], p['w_exp_gate'][l], p['w_exp_up'][l],
                          p['w_exp_down'][l], p['w_sh_gate'][l], p['w_sh_up'][l], p['w_sh_down'][l])
    y = rmsnorm(x, p['g_final'])
    return y, jnp.stack(Cs), jnp.stack(ns), jnp.stack(ms), new_lat, new_kr


def setup_inputs(seed: int = 0) -> dict:
    key = jax.random.key(seed)
    ks = iter(jax.random.split(key, 64))

    def nrm(shape, scale):
        return jax.random.normal(next(ks), shape, jnp.float32) * scale

    def gain(shape):
        return 1.0 + nrm(shape, 0.05)

    n_pages = PAST_LEN // PAGE_SIZE
    n_phys = (5 * DEC_BATCH * n_pages) // 4
    page_table = jax.random.permutation(next(ks), n_phys)[:DEC_BATCH * n_pages].reshape(DEC_BATCH, n_pages).astype(jnp.int32)
    D = D_MODEL
    b_gates = jnp.concatenate([nrm((N_A_LAYERS, ML_HEADS), 0.1),
                               ML_FORGET_BIAS + nrm((N_A_LAYERS, ML_HEADS), 0.1)], axis=-1)
    return {
        'x_prompt': nrm((BATCH, SEQ, D), 1.0),
        'x_sample': nrm((DEC_BATCH, DEC_SEQ, D), 1.0),
        'state_mlstm_C': nrm((N_A_LAYERS, DEC_BATCH, ML_HEADS, ML_DK, ML_DV), 0.1),
        'state_mlstm_n': nrm((N_A_LAYERS, DEC_BATCH, ML_HEADS, ML_DK), 0.1),
        'state_mlstm_m': nrm((N_A_LAYERS, DEC_BATCH, ML_HEADS), 1.0),
        'cache_latent': nrm((n_phys, PAGE_SIZE, KV_LORA), 1.0),
        'cache_krope': nrm((n_phys, PAGE_SIZE, ROPE_DIM), 1.0),
        'page_table': page_table,
        'c_prompt': nrm((BATCH, D), 1.0),
        'c_sample': nrm((DEC_BATCH, D), 1.0),
        'w_ada': nrm((DEPTH, D, 6 * D), 0.5 * D ** -0.5),
        'b_ada': nrm((DEPTH, 6 * D), 0.02),
        'g_mix': gain((DEPTH, D)),
        'g_ffn': gain((DEPTH, D)),
        'w_mlstm_in': nrm((N_A_LAYERS, D, ML_IN_W), D ** -0.5),
        'b_mlstm_gates': b_gates,
        'g_mlstm_head': gain((N_A_LAYERS, ML_HEADS * ML_DV)),
        'w_mlstm_out': nrm((N_A_LAYERS, ML_HEADS * ML_DV, D), (ML_HEADS * ML_DV) ** -0.5),
        'w_ada_kv': nrm((D, 2 * D), 0.5 * D ** -0.5),
        'b_ada_kv': nrm((2 * D,), 0.02),
        'g_kv_in': gain((D,)),
        'w_dkv': nrm((D, KV_LORA + ROPE_DIM), D ** -0.5),
        'g_kv_lat': gain((KV_LORA,)),
        'w_uk': nrm((KV_LORA, MLA_HEADS, NOPE_DIM), KV_LORA ** -0.5),
        'w_uv': nrm((KV_LORA, MLA_HEADS, V_DIM), KV_LORA ** -0.5),
        'w_dq': nrm((N_B_LAYERS, D, Q_LORA), D ** -0.5),
        'g_q_lat': gain((N_B_LAYERS, Q_LORA)),
        'w_uq': nrm((N_B_LAYERS, Q_LORA, MLA_HEADS * (NOPE_DIM + ROPE_DIM)), Q_LORA ** -0.5),
        'w_mla_out': nrm((N_B_LAYERS, MLA_HEADS * V_DIM, D), (MLA_HEADS * V_DIM) ** -0.5),
        'w_router': nrm((DEPTH, D, N_EXPERTS), D ** -0.5),
        'b_router': nrm((DEPTH, N_EXPERTS), 0.01),
        'w_exp_gate': nrm((DEPTH, N_EXPERTS, D, EXPERT_FF), D ** -0.5),
        'w_exp_up': nrm((DEPTH, N_EXPERTS, D, EXPERT_FF), D ** -0.5),
        'w_exp_down': nrm((DEPTH, N_EXPERTS, EXPERT_FF, D), EXPERT_FF ** -0.5),
        'w_sh_gate': nrm((DEPTH, D, SHARED_FF), D ** -0.5),
        'w_sh_up': nrm((DEPTH, D, SHARED_FF), D ** -0.5),
        'w_sh_down': nrm((DEPTH, SHARED_FF, D), SHARED_FF ** -0.5),
        'g_final': gain((D,)),
    }


def reference(x_prompt, x_sample, state_mlstm_C, state_mlstm_n, state_mlstm_m, cache_latent, cache_krope,
              page_table, c_prompt, c_sample, w_ada, b_ada, g_mix, g_ffn, w_mlstm_in, b_mlstm_gates,
              g_mlstm_head, w_mlstm_out, w_ada_kv, b_ada_kv, g_kv_in, w_dkv, g_kv_lat, w_uk, w_uv, w_dq,
              g_q_lat, w_uq, w_mla_out, w_router, b_router, w_exp_gate, w_exp_up, w_exp_down, w_sh_gate,
              w_sh_up, w_sh_down, g_final):
    p = dict(w_ada=w_ada, b_ada=b_ada, g_mix=g_mix, g_ffn=g_ffn, w_mlstm_in=w_mlstm_in,
             b_mlstm_gates=b_mlstm_gates, g_mlstm_head=g_mlstm_head, w_mlstm_out=w_mlstm_out,
             w_ada_kv=w_ada_kv, b_ada_kv=b_ada_kv, g_kv_in=g_kv_in, w_dkv=w_dkv, g_kv_lat=g_kv_lat,
             w_uk=w_uk, w_uv=w_uv, w_dq=w_dq, g_q_lat=g_q_lat, w_uq=w_uq, w_mla_out=w_mla_out,
             w_router=w_router, b_router=b_router, w_exp_gate=w_exp_gate, w_exp_up=w_exp_up,
             w_exp_down=w_exp_down, w_sh_gate=w_sh_gate, w_sh_up=w_sh_up, w_sh_down=w_sh_down,
             g_final=g_final)
    C0 = jnp.zeros((N_A_LAYERS, BATCH, ML_HEADS, ML_DK, ML_DV), jnp.float32)
    n0 = jnp.zeros((N_A_LAYERS, BATCH, ML_HEADS, ML_DK), jnp.float32)
    m0 = jnp.full((N_A_LAYERS, BATCH, ML_HEADS), M_INIT, jnp.float32)
    empty_lat = jnp.zeros((BATCH, 0, KV_LORA), x_prompt.dtype)
    empty_kr = jnp.zeros((BATCH, 0, ROPE_DIM), x_prompt.dtype)
    y_prompt, p_C, p_n, p_m, p_lat, p_kr = run_group(x_prompt, c_prompt, 0, C0, n0, m0, empty_lat, empty_kr, p)
    n_pages = PAST_LEN // PAGE_SIZE
    past_lat = cache_latent[page_table].reshape(DEC_BATCH, n_pages * PAGE_SIZE, KV_LORA)
    past_kr = cache_krope[page_table].reshape(DEC_BATCH, n_pages * PAGE_SIZE, ROPE_DIM)
    y_sample, s_C, s_n, s_m, s_lat, s_kr = run_group(x_sample, c_sample, PAST_LEN, state_mlstm_C, state_mlstm_n,
                                                     state_mlstm_m, past_lat, past_kr, p)
    return (y_prompt, y_sample, p_C, p_n, p_m, p_lat, p_kr, s_C, s_n, s_m, s_lat, s_kr)
```

```python
import functools

import jax
import jax.numpy as jnp
from jax import lax
from jax.experimental import pallas as pl
from jax.experimental.pallas import tpu as pltpu

F32 = jnp.float32
BF16 = jnp.bfloat16
U32 = jnp.uint32
I32 = jnp.int32
HIGHEST = lax.Precision.HIGHEST
NEG_INF = float("-inf")

ML_HEADS = 4
ML_DK = 128
ML_DV = 256
ML_CHUNK = 128
M_INIT = -1e30
MLA_HEADS = 8
NOPE_DIM = 128
ROPE_DIM = 64
V_DIM = 128
KV_LORA = 256
ROPE_THETA = 10000.0
N_EXPERTS = 64
TOP_K = 8
N_GROUPS = 8
TOPK_GROUPS = 4
ROUTED_SCALE = 2.5
NORM_EPS = 1e-6

EXPERT_ROWS = 512
VMEM_LIMIT = 56 * 1024 * 1024


def _cparams(*sem):
    return pltpu.CompilerParams(dimension_semantics=sem, vmem_limit_bytes=VMEM_LIMIT)


def _dot(a, b):
    return jnp.dot(a.astype(BF16), b.astype(BF16), preferred_element_type=F32)


def _dot_nt(a, b):
    return lax.dot_general(a.astype(BF16), b.astype(BF16), (((1,), (1,)), ((), ())),
                           preferred_element_type=F32)


def _dot_f32(a, b):
    return jnp.dot(a, b, precision=HIGHEST, preferred_element_type=F32)


def _rms(x, g):
    return x * lax.rsqrt(jnp.mean(x * x, axis=-1, keepdims=True) + NORM_EPS) * g


def _rms_mod(x, g, shift, scale):
    return _rms(x, g) * (1.0 + scale) + shift


def _sigmoid(x):
    return 1.0 / (1.0 + jnp.exp(-x))


def _silu(x):
    return x * _sigmoid(x)


def _log_sigmoid(x):
    return jnp.minimum(x, 0.0) - jnp.log1p(jnp.exp(-jnp.abs(x)))


def _ada_kernel(c_ref, w_ref, b_ref, o_ref):
    o_ref[...] = _dot(_silu(c_ref[...]), w_ref[...]) + b_ref[...]


def _ada(c, w, b, layer=None, tn=1024):
    m, d = c.shape
    n_out = w.shape[-1]
    if layer is None:
        w_spec = pl.BlockSpec((d, tn), lambda j: (0, j))
    else:
        w_spec = pl.BlockSpec((None, d, tn), lambda j: (layer, 0, j))
    return pl.pallas_call(
        _ada_kernel,
        grid=(n_out // tn,),
        in_specs=[pl.BlockSpec((m, d), lambda j: (0, 0)), w_spec, pl.BlockSpec((1, tn), lambda j: (0, j))],
        out_specs=pl.BlockSpec((m, tn), lambda j: (0, j)),
        out_shape=jax.ShapeDtypeStruct((m, n_out), F32),
        compiler_params=_cparams("arbitrary"),
        name="ada",
    )(c, w, b.reshape(1, n_out))


class _Tok:
    def __init__(self, batch, seq, tile):
        if seq == 1:
            self.tm, self.n_tiles, self.per_seq = batch, 1, 1
            self.mod_block = (1, batch, None)
        else:
            self.tm = min(tile, seq)
            self.per_seq = seq // self.tm
            self.n_tiles = batch * self.per_seq
            self.mod_block = (1, 1, None)
        self.batch, self.seq = batch, seq

    def mod(self, v):
        d = v.shape[-1]
        if self.seq == 1:
            return v.reshape(1, self.batch, d), pl.BlockSpec((1, self.batch, d), lambda i, *_: (0, 0, 0))
        per = self.per_seq
        return v.reshape(self.batch, 1, d), pl.BlockSpec((1, 1, d), lambda i, *_: (i // per, 0, 0))

    def rows(self, width):
        return pl.BlockSpec((self.tm, width), lambda i, *_: (i, 0))

    def seq_rows(self, width):
        per = self.per_seq
        return pl.BlockSpec((self.tm, width), lambda i, *_: (i % per, 0))


def _full(shape):
    nd = len(shape)
    return pl.BlockSpec(shape, lambda i, *_: (0,) * nd)


def _in0_kernel(x_ref, sh_ref, sc_ref, g_ref, w_ref, wg_ref, bg_ref, o_ref, gt_ref):
    h = _rms_mod(x_ref[...], g_ref[...], sh_ref[0], sc_ref[0])
    o_ref[...] = _dot(h, w_ref[...]).astype(o_ref.dtype)
    gt_ref[...] = _dot_f32(h, wg_ref[...]) + bg_ref[...]


def _in0(tok, x, shift, scale, g, w, w_gates, b_gates, out_dtype):
    n, d = x.shape
    wo = w.shape[1]
    sh, sh_spec = tok.mod(shift)
    sc, sc_spec = tok.mod(scale)
    return pl.pallas_call(
        _in0_kernel,
        grid=(tok.n_tiles,),
        in_specs=[tok.rows(d), sh_spec, sc_spec, _full((1, d)), _full((d, wo)), _full((d, 128)), _full((1, 128))],
        out_specs=[tok.rows(wo), tok.rows(128)],
        out_shape=[jax.ShapeDtypeStruct((n, wo), out_dtype), jax.ShapeDtypeStruct((n, 128), F32)],
        compiler_params=_cparams("arbitrary"),
        name="in0",
    )(x, sh, sc, g.reshape(1, d), w, w_gates, b_gates)


def _mlstm_chunk_kernel(q_ref, k_ref, v_ref, og_ref, gt_ref, c0_ref, n0_ref, m0_ref, gh_ref,
                        hh_ref, c_ref, n_ref, m_ref):
    L = q_ref.shape[0]
    scale = ML_DK ** -0.5

    @pl.when(pl.program_id(1) == 0)
    def _():
        c_ref[...] = c0_ref[...]
        n_ref[...] = n0_ref[...]
        m_ref[...] = m0_ref[...]

    gates = gt_ref[...]
    logf = _log_sigmoid(gates)
    row = lax.broadcasted_iota(I32, (L, L), 0)
    col = lax.broadcasted_iota(I32, (L, L), 1)
    causal = col <= row
    tril = causal.astype(F32)
    b_cols = _dot_f32(tril, logf)
    gates_t = gates.T
    b_rows = _dot_f32(logf.T, (row <= col).astype(F32))

    for h in range(ML_HEADS):
        i_row = gates_t[h:h + 1, :]
        i_col = gates[:, h:h + 1]
        b_row = b_rows[ML_HEADS + h:ML_HEADS + h + 1, :]
        b_col = b_cols[:, ML_HEADS + h:ML_HEADS + h + 1]
        m_prev = m_ref[0, :, h:h + 1]
        c_prev = c_ref[0, h]
        n_prev = n_ref[0, h:h + 1, :]
        q = q_ref[:, h * ML_DK:(h + 1) * ML_DK]
        k = k_ref[:, h * ML_DK:(h + 1) * ML_DK]
        v = v_ref[:, h * ML_DV:(h + 1) * ML_DV]

        g_col = b_col + m_prev
        dmat = jnp.where(causal, b_col - b_row + i_row, NEG_INF)
        m_t = jnp.maximum(g_col, jnp.max(dmat, axis=1, keepdims=True))
        w_intra = jnp.exp(dmat - m_t)
        w_inter = jnp.exp(g_col - m_t)
        s = _dot_nt(q, k) * (w_intra * scale)
        num = _dot(s, v) + w_inter * _dot(q, c_prev)
        qn = jnp.sum(q.astype(F32) * n_prev, axis=1, keepdims=True)
        den = jnp.sum(s, axis=1, keepdims=True) + w_inter * qn
        cell = num / jnp.maximum(jnp.abs(den), jnp.exp(-m_t))

        b_last = b_col[L - 1:L, :]
        m_new = m_t[L - 1:L, :]
        a_inter = jnp.exp(b_last + m_prev - m_new)
        a_intra = jnp.exp(b_last - b_col + i_col - m_new) * scale
        kw = k.astype(F32) * a_intra
        c_ref[0, h] = a_inter * c_prev + _dot(kw.T, v)
        n_ref[0, h:h + 1, :] = a_inter * n_prev + jnp.sum(kw, axis=0, keepdims=True)
        m_ref[0, :, h:h + 1] = m_new

        y = _rms(cell, gh_ref[:, h * ML_DV:(h + 1) * ML_DV])
        og = og_ref[:, h * ML_DV:(h + 1) * ML_DV].astype(F32)
        hh_ref[:, h * ML_DV:(h + 1) * ML_DV] = (y * _sigmoid(og)).astype(hh_ref.dtype)


def _mlstm_chunks(proj, gates, c0, n0, m0, g_head, batch, seq):
    L = seq if seq <= ML_CHUNK else ML_CHUNK
    nc = seq // L
    hk, hv = ML_HEADS * ML_DK, ML_HEADS * ML_DV
    rows = lambda w, cb: pl.BlockSpec((L, w), lambda b, j: (b * nc + j, cb))
    state = lambda shape: pl.BlockSpec((1,) + shape, lambda b, j: (b,) + (0,) * len(shape))
    return pl.pallas_call(
        _mlstm_chunk_kernel,
        grid=(batch, nc),
        in_specs=[rows(hk, 0), rows(hk, 1), rows(hv, 2 * hk // hv), rows(hv, 2 * hk // hv + 1), rows(128, 0),
                  state((ML_HEADS, ML_DK, ML_DV)), state((ML_HEADS, ML_DK)), state((1, ML_HEADS)),
                  pl.BlockSpec((1, hv), lambda b, j: (0, 0))],
        out_specs=[rows(hv, 0), state((ML_HEADS, ML_DK, ML_DV)), state((ML_HEADS, ML_DK)), state((1, ML_HEADS))],
        out_shape=[jax.ShapeDtypeStruct((batch * seq, hv), BF16),
                   jax.ShapeDtypeStruct((batch, ML_HEADS, ML_DK, ML_DV), F32),
                   jax.ShapeDtypeStruct((batch, ML_HEADS, ML_DK), F32),
                   jax.ShapeDtypeStruct((batch, 1, ML_HEADS), F32)],
        compiler_params=_cparams("arbitrary", "arbitrary"),
        name="mlstm_chunks",
    )(proj, proj, proj, proj, gates, c0, n0, m0.reshape(batch, 1, ML_HEADS), g_head.reshape(1, hv))


def _row_to_col(r):
    n = r.shape[1]
    eye = lax.broadcasted_iota(I32, (n, n), 0) == lax.broadcasted_iota(I32, (n, n), 1)
    return jnp.sum(jnp.where(eye, jnp.broadcast_to(r, (n, n)), 0.0), axis=1, keepdims=True)


def _mlstm_step_kernel(p_ref, gt_ref, c0_ref, n0_ref, m0_ref, gh_ref, hh_ref, c_ref, n_ref, m_ref):
    scale = ML_DK ** -0.5
    hk, hv = ML_HEADS * ML_DK, ML_HEADS * ML_DV
    gates = gt_ref[0]
    logf = _log_sigmoid(gates)
    for h in range(ML_HEADS):
        q = p_ref[0, :, h * ML_DK:(h + 1) * ML_DK].astype(F32)
        k = p_ref[0, :, hk + h * ML_DK:hk + (h + 1) * ML_DK].astype(F32)
        v = p_ref[0, :, 2 * hk + h * ML_DV:2 * hk + (h + 1) * ML_DV].astype(F32)
        og = p_ref[0, :, 2 * hk + hv + h * ML_DV:2 * hk + hv + (h + 1) * ML_DV].astype(F32)
        log_i = gates[:, h:h + 1]
        m_prev = m0_ref[0, :, h:h + 1]
        c_prev = c0_ref[0, h]
        n_prev = n0_ref[0, h:h + 1, :]

        g = logf[:, ML_HEADS + h:ML_HEADS + h + 1] + m_prev
        m_t = jnp.maximum(g, log_i)
        w_intra = jnp.exp(log_i - m_t)
        w_inter = jnp.exp(g - m_t)
        s = jnp.sum(q * k, axis=1, keepdims=True) * (w_intra * scale)
        q_c = _row_to_col(q)
        num = s * v + w_inter * jnp.sum(q_c * c_prev, axis=0, keepdims=True)
        den = s + w_inter * jnp.sum(q * n_prev, axis=1, keepdims=True)
        cell = num / jnp.maximum(jnp.abs(den), jnp.exp(-m_t))

        kw = k * (w_intra * scale)
        c_ref[0, h] = w_inter * c_prev + _row_to_col(kw) * v
        n_ref[0, h:h + 1, :] = w_inter * n_prev + kw
        m_ref[0, :, h:h + 1] = m_t

        y = _rms(cell, gh_ref[:, h * ML_DV:(h + 1) * ML_DV])
        hh_ref[0, :, h * ML_DV:(h + 1) * ML_DV] = (y * _sigmoid(og)).astype(hh_ref.dtype)


def _mlstm_step(proj, gates, c0, n0, m0, g_head, batch):
    hv = ML_HEADS * ML_DV
    wp = proj.shape[1]
    one = lambda shape: pl.BlockSpec((1,) + shape, lambda b: (b,) + (0,) * len(shape))
    return pl.pallas_call(
        _mlstm_step_kernel,
        grid=(batch,),
        in_specs=[one((1, wp)), one((1, 128)), one((ML_HEADS, ML_DK, ML_DV)), one((ML_HEADS, ML_DK)),
                  one((1, ML_HEADS)), pl.BlockSpec((1, hv), lambda b: (0, 0))],
        out_specs=[one((1, hv)), one((ML_HEADS, ML_DK, ML_DV)), one((ML_HEADS, ML_DK)), one((1, ML_HEADS))],
        out_shape=[jax.ShapeDtypeStruct((batch, 1, hv), BF16),
                   jax.ShapeDtypeStruct((batch, ML_HEADS, ML_DK, ML_DV), F32),
                   jax.ShapeDtypeStruct((batch, ML_HEADS, ML_DK), F32),
                   jax.ShapeDtypeStruct((batch, 1, ML_HEADS), F32)],
        compiler_params=_cparams("arbitrary"),
        name="mlstm_step",
    )(proj.reshape(batch, 1, wp), gates.reshape(batch, 1, 128), c0, n0, m0.reshape(batch, 1, ML_HEADS),
      g_head.reshape(1, hv))


def _group_reduce(x, op):
    e, tm = x.shape
    per = e // N_GROUPS
    x3 = x.reshape(N_GROUPS, per, tm)
    r = op(x3, axis=1, keepdims=True)
    return jnp.broadcast_to(r, (N_GROUPS, per, tm)).reshape(e, tm)


def _route(logits_t, bias_t):
    e, tm = logits_t.shape
    per = e // N_GROUPS
    scores = _sigmoid(logits_t)
    sel = scores + bias_t
    eidx = lax.broadcasted_iota(I32, (e, tm), 0)
    jidx = eidx % per
    gidx = eidx // per
    m1 = _group_reduce(sel, jnp.max)
    first1 = _group_reduce(jnp.where(sel == m1, jidx, per), jnp.min)
    m2 = _group_reduce(jnp.where(jidx == first1, NEG_INF, sel), jnp.max)
    gs = m1 + m2
    chosen = jnp.zeros((e, tm), F32)
    for _ in range(TOPK_GROUPS):
        mx = jnp.max(gs, axis=0, keepdims=True)
        f = jnp.min(jnp.where(gs == mx, gidx, N_GROUPS), axis=0, keepdims=True)
        hit = gidx == f
        chosen = jnp.where(hit, 1.0, chosen)
        gs = jnp.where(hit, NEG_INF, gs)
    cand = jnp.where(chosen > 0.0, sel, NEG_INF)
    picked = jnp.zeros((e, tm), F32)
    idx, wts = [], []
    for _ in range(TOP_K):
        mx = jnp.max(cand, axis=0, keepdims=True)
        f = jnp.min(jnp.where(cand == mx, eidx, e), axis=0, keepdims=True)
        hit = eidx == f
        idx.append(f)
        wts.append(jnp.sum(jnp.where(hit, scores, 0.0), axis=0, keepdims=True))
        cand = jnp.where(hit, NEG_INF, cand)
        picked = jnp.where(hit, 1.0, picked)
    return idx, wts, picked


def _post_kernel(x_ref, a_ref, wo_ref, gm_ref, gn_ref, sh_ref, sc_ref, gf_ref, wr_ref, br_ref,
                 wsgu_ref, wsd_ref, cnt0_ref,
                 xs_ref, hp_ref, idx_ref, wt_ref, rank_ref, cnt_ref, carry):
    tm, d = x_ref.shape
    half = d // 2

    @pl.when(pl.program_id(0) == 0)
    def _():
        carry[...] = cnt0_ref[...]

    x1 = x_ref[...] + gm_ref[0] * _dot(a_ref[...], wo_ref[...])
    h2 = _rms_mod(x1, gn_ref[...], sh_ref[0], sc_ref[0])

    hi = lax.bitcast_convert_type(h2[:, :half].astype(BF16).astype(F32), U32)
    lo = lax.bitcast_convert_type(h2[:, half:].astype(BF16).astype(F32), U32)
    hp_ref[...] = hi | (lo >> 16)

    ff = wsd_ref.shape[0]
    gu = _dot(h2, wsgu_ref[...])
    shared = _dot(_silu(gu[:, :ff]) * gu[:, ff:], wsd_ref[...])
    xs_ref[...] = x1 + gf_ref[0] * shared

    logits_t = lax.dot_general(wr_ref[...], h2, (((1,), (1,)), ((), ())), precision=HIGHEST,
                               preferred_element_type=F32)
    idx, wts, picked = _route(logits_t, br_ref[:, 0:1])
    e = picked.shape[0]
    eidx = lax.broadcasted_iota(I32, (e, tm), 0)
    before = (lax.broadcasted_iota(I32, (tm, tm), 0) < lax.broadcasted_iota(I32, (tm, tm), 1)).astype(BF16)
    base = carry[:, 0:1]
    rank_t = jnp.dot(picked.astype(BF16), before, preferred_element_type=F32) + base
    total = wts[0]
    for k in range(1, TOP_K):
        total = total + wts[k]
    for k in range(TOP_K):
        idx_ref[k:k + 1, :] = idx[k]
        wt_ref[k:k + 1, :] = wts[k] / total * ROUTED_SCALE
        rank_ref[k:k + 1, :] = jnp.sum(jnp.where(eidx == idx[k], rank_t, 0.0), axis=0, keepdims=True).astype(I32)
    new = base + jnp.sum(picked, axis=1, keepdims=True)
    carry[...] = jnp.broadcast_to(new, carry.shape)
    cnt_ref[...] = jnp.broadcast_to(new, cnt_ref.shape)


def _post(tok, x, a, w_o, g_m, g_norm, shift, scale, g_f, w_router_t, b_router, ws_gu, ws_d, cnt0):
    n, d = x.shape
    e = w_router_t.shape[0]
    ff = ws_d.shape[0]
    gm, gm_spec = tok.mod(g_m)
    sh, sh_spec = tok.mod(shift)
    sc, sc_spec = tok.mod(scale)
    gf, gf_spec = tok.mod(g_f)
    cols = lambda rows: pl.BlockSpec((rows, tok.tm), lambda i: (0, i))
    return pl.pallas_call(
        _post_kernel,
        grid=(tok.n_tiles,),
        in_specs=[tok.rows(d), tok.rows(a.shape[1]), _full(w_o.shape), gm_spec, _full((1, d)), sh_spec, sc_spec,
                  gf_spec, _full((e, d)), _full((e, 128)), _full((d, 2 * ff)), _full((ff, d)), _full((e, 128))],
        out_specs=[tok.rows(d), tok.rows(d // 2), cols(TOP_K), cols(TOP_K), cols(TOP_K), _full((e, 128))],
        out_shape=[jax.ShapeDtypeStruct((n, d), F32), jax.ShapeDtypeStruct((n, d // 2), U32),
                   jax.ShapeDtypeStruct((TOP_K, n), I32), jax.ShapeDtypeStruct((TOP_K, n), F32),
                   jax.ShapeDtypeStruct((TOP_K, n), I32), jax.ShapeDtypeStruct((e, 128), F32)],
        scratch_shapes=[pltpu.VMEM((e, 128), F32)],
        compiler_params=_cparams("arbitrary"),
        name="post",
    )(x, a, w_o, gm, g_norm.reshape(1, d), sh, sc, gf, w_router_t,
      jnp.broadcast_to(b_router.reshape(e, 1), (e, 128)), ws_gu, ws_d, cnt0)


def _row_copy(hp_ref, xs_ref, dest_ref, sem, r, k):
    return pltpu.make_async_copy(hp_ref.at[pl.ds(r, 1)], xs_ref.at[pl.ds(dest_ref[r * TOP_K + k], 1)], sem)


def _scatter_kernel(dest_ref, hp_ref, xin_ref, xs_ref, sem):
    del xin_ref
    tm = hp_ref.shape[0]

    def start(r, c):
        for k in range(TOP_K):
            _row_copy(hp_ref, xs_ref, dest_ref, sem, r, k).start()
        return c

    def wait(r, c):
        for k in range(TOP_K):
            _row_copy(hp_ref, xs_ref, dest_ref, sem, r, k).wait()
        return c

    lax.fori_loop(0, tm, start, 0)
    lax.fori_loop(0, tm, wait, 0)


def _scatter(hp, dest_flat, xs, tm):
    n, w = hp.shape
    tm = min(tm, n)
    return pl.pallas_call(
        _scatter_kernel,
        grid=(n // tm,),
        in_specs=[pl.BlockSpec((tm * TOP_K,), lambda i: (i,), memory_space=pltpu.SMEM),
                  pl.BlockSpec((tm, w), lambda i: (i, 0)),
                  pl.BlockSpec(memory_space=pl.ANY)],
        out_specs=pl.BlockSpec(memory_space=pl.ANY),
        out_shape=jax.ShapeDtypeStruct(xs.shape, xs.dtype),
        scratch_shapes=[pltpu.SemaphoreType.DMA],
        input_output_aliases={2: 0},
        compiler_params=_cparams("arbitrary"),
        name="dispatch",
    )(dest_flat, hp, xs)


def _expert_kernel(be_ref, first_ref, nu_ref, x_ref, wg_ref, wu_ref, wd_ref, o_ref, wgu_s, wd_s):
    i = pl.program_id(0)
    ff = wd_s.shape[0]
    half = x_ref.shape[1]

    @pl.when(i < nu_ref[0])
    def _():
        @pl.when(first_ref[i] == 1)
        def _():
            wgu_s[:, :ff] = wg_ref[...].astype(BF16)
            wgu_s[:, ff:] = wu_ref[...].astype(BF16)
            wd_s[...] = wd_ref[...].astype(BF16)

        p = x_ref[...]
        hi = lax.bitcast_convert_type(p & jnp.uint32(0xFFFF0000), F32).astype(BF16)
        lo = lax.bitcast_convert_type(p << 16, F32).astype(BF16)
        gu = (jnp.dot(hi, wgu_s[:half, :], preferred_element_type=F32)
              + jnp.dot(lo, wgu_s[half:, :], preferred_element_type=F32))
        act = _silu(gu[:, :ff]) * gu[:, ff:]
        o_ref[...] = jnp.dot(act.astype(BF16), wd_s[...], preferred_element_type=F32)

    @pl.when(i >= nu_ref[0])
    def _():
        o_ref[...] = jnp.zeros_like(o_ref)


def _experts(xs, block_e, first, n_used, w_gate, w_up, w_down, layer):
    n_rows, half = xs.shape
    d = 2 * half
    ff = w_gate.shape[-1]
    bm = EXPERT_ROWS
    grid_spec = pltpu.PrefetchScalarGridSpec(
        num_scalar_prefetch=3,
        grid=(n_rows // bm,),
        in_specs=[pl.BlockSpec((bm, half), lambda i, be, fi, nu: (jnp.minimum(i, nu[0] - 1), 0)),
                  pl.BlockSpec((None, None, d, ff), lambda i, be, fi, nu: (layer, be[i], 0, 0)),
                  pl.BlockSpec((None, None, d, ff), lambda i, be, fi, nu: (layer, be[i], 0, 0)),
                  pl.BlockSpec((None, None, ff, d), lambda i, be, fi, nu: (layer, be[i], 0, 0))],
        out_specs=pl.BlockSpec((bm, d), lambda i, be, fi, nu: (i, 0)),
        scratch_shapes=[pltpu.VMEM((d, 2 * ff), BF16), pltpu.VMEM((ff, d), BF16)],
    )
    return pl.pallas_call(
        _expert_kernel,
        grid_spec=grid_spec,
        out_shape=jax.ShapeDtypeStruct((n_rows, d), F32),
        compiler_params=_cparams("arbitrary"),
        name="experts",
    )(block_e, first, n_used, xs, w_gate, w_up, w_down)


def _combine_kernel(dest_ref, xs_ref, w_ref, gf_ref, gfin_ref, os_ref, y_ref, buf, sem, *, final_norm):
    tm = xs_ref.shape[0]

    def copy(r, k):
        return pltpu.make_async_copy(os_ref.at[pl.ds(dest_ref[r * TOP_K + k], 1)], buf.at[k, pl.ds(r, 1)], sem)

    def start(r, c):
        for k in range(TOP_K):
            copy(r, k).start()
        return c

    def wait(r, c):
        for k in range(TOP_K):
            copy(r, k).wait()
        return c

    lax.fori_loop(0, tm, start, 0)
    lax.fori_loop(0, tm, wait, 0)
    w = w_ref[...]
    acc = w[:, 0:1] * buf[0]
    for k in range(1, TOP_K):
        acc = acc + w[:, k:k + 1] * buf[k]
    x2 = xs_ref[...] + gf_ref[0] * acc
    y_ref[...] = _rms(x2, gfin_ref[...]) if final_norm else x2


def _combine(tok_seq, xs, wts, g_f, g_final, out_sorted, dest_flat, final_norm, tm):
    n, d = xs.shape
    batch, seq = tok_seq
    tok = _Tok(batch, seq, tm)
    gf, gf_spec = tok.mod(g_f)
    tm = tok.tm
    return pl.pallas_call(
        functools.partial(_combine_kernel, final_norm=final_norm),
        grid=(tok.n_tiles,),
        in_specs=[pl.BlockSpec((tm * TOP_K,), lambda i: (i,), memory_space=pltpu.SMEM),
                  tok.rows(d), tok.rows(TOP_K), gf_spec, _full((1, d)),
                  pl.BlockSpec(memory_space=pl.ANY)],
        out_specs=tok.rows(d),
        out_shape=jax.ShapeDtypeStruct((n, d), F32),
        scratch_shapes=[pltpu.VMEM((TOP_K, tm, d), F32), pltpu.SemaphoreType.DMA],
        compiler_params=_cparams("arbitrary"),
        name="combine",
    )(dest_flat, xs, wts, gf, g_final.reshape(1, d), out_sorted)


def _rope(x, cos2, sin2):
    half = x.shape[1] // 2
    swapped = jnp.concatenate([x[:, half:], x[:, :half]], axis=1)
    return x * cos2 + swapped * sin2


def _in1_kernel(x_ref, shk_ref, sck_ref, gk_ref, wdkv_ref, glat_ref, cos_ref, sin_ref,
                shm_ref, scm_ref, gm_ref, wdq_ref, gq_ref, wqn_ref, wqr_ref, wuk_ref,
                lat_ref, kr_ref, latb_ref, krb_ref, qa_ref, qr_ref):
    x = x_ref[...]
    cos2, sin2 = cos_ref[...], sin_ref[...]
    att_scale = (NOPE_DIM + ROPE_DIM) ** -0.5

    hk = _rms_mod(x, gk_ref[...], shk_ref[0], sck_ref[0])
    ckr = _dot(hk, wdkv_ref[...])
    lat = _rms(ckr[:, :KV_LORA], glat_ref[...])
    kr = _rope(ckr[:, KV_LORA:], cos2, sin2)
    lat_ref[...] = lat
    kr_ref[...] = kr
    latb_ref[...] = lat.astype(BF16)
    krb_ref[...] = kr.astype(BF16)

    hm = _rms_mod(x, gm_ref[...], shm_ref[0], scm_ref[0])
    q_lat = _rms(_dot(hm, wdq_ref[...]), gq_ref[...])
    q_nope = _dot(q_lat, wqn_ref[...])
    q_rope = _dot(q_lat, wqr_ref[...])
    for h in range(MLA_HEADS):
        qa = _dot(q_nope[:, h * NOPE_DIM:(h + 1) * NOPE_DIM], wuk_ref[h])
        qa_ref[h] = (qa * att_scale).astype(BF16)
        qr = _rope(q_rope[:, h * ROPE_DIM:(h + 1) * ROPE_DIM], cos2, sin2)
        qr_ref[h] = (qr * att_scale).astype(BF16)


def _in1(tok, x, sh_kv, sc_kv, g_kv_in, w_dkv, g_kv_lat, cos2, sin2, sh_m, sc_m, g_mix, w_dq, g_q_lat,
         w_q_nope, w_q_rope, w_uk_t):
    n, d = x.shape
    q_lora = w_dq.shape[1]
    shk, shk_spec = tok.mod(sh_kv)
    sck, sck_spec = tok.mod(sc_kv)
    shm, shm_spec = tok.mod(sh_m)
    scm, scm_spec = tok.mod(sc_m)
    heads = lambda w: pl.BlockSpec((MLA_HEADS, tok.tm, w), lambda i: (0, i, 0))
    return pl.pallas_call(
        _in1_kernel,
        grid=(tok.n_tiles,),
        in_specs=[tok.rows(d), shk_spec, sck_spec, _full((1, d)), _full(w_dkv.shape), _full((1, KV_LORA)),
                  tok.seq_rows(ROPE_DIM), tok.seq_rows(ROPE_DIM),
                  shm_spec, scm_spec, _full((1, d)), _full(w_dq.shape), _full((1, q_lora)),
                  _full(w_q_nope.shape), _full(w_q_rope.shape), _full(w_uk_t.shape)],
        out_specs=[tok.rows(KV_LORA), tok.rows(ROPE_DIM), tok.rows(KV_LORA), tok.rows(ROPE_DIM),
                   heads(KV_LORA), heads(ROPE_DIM)],
        out_shape=[jax.ShapeDtypeStruct((n, KV_LORA), F32), jax.ShapeDtypeStruct((n, ROPE_DIM), F32),
                   jax.ShapeDtypeStruct((n, KV_LORA), BF16), jax.ShapeDtypeStruct((n, ROPE_DIM), BF16),
                   jax.ShapeDtypeStruct((MLA_HEADS, n, KV_LORA), BF16),
                   jax.ShapeDtypeStruct((MLA_HEADS, n, ROPE_DIM), BF16)],
        compiler_params=_cparams("arbitrary"),
        name="in1",
    )(x, shk, sck, g_kv_in.reshape(1, d), w_dkv, g_kv_lat.reshape(1, KV_LORA), cos2, sin2,
      shm, scm, g_mix.reshape(1, d), w_dq, g_q_lat.reshape(1, q_lora), w_q_nope, w_q_rope, w_uk_t)


def _attn_prompt_kernel(qa_ref, qr_ref, lat_ref, kr_ref, wuv_ref, o_ref, m_s, l_s, acc_s):
    heads, tq, c = qa_ref.shape
    rows = heads * tq
    qi = pl.program_id(1)
    qa = qa_ref[...].reshape(rows, c)
    qr = qr_ref[...].reshape(rows, qr_ref.shape[2])
    m_s[...] = jnp.full(m_s.shape, NEG_INF, F32)
    l_s[...] = jnp.zeros(l_s.shape, F32)
    acc_s[...] = jnp.zeros(acc_s.shape, F32)
    q_pos = qi * tq + lax.broadcasted_iota(I32, (rows, tq), 0) % tq
    k_off = lax.broadcasted_iota(I32, (rows, tq), 1)

    def step(j, carry):
        start = pl.multiple_of(j * tq, tq)
        kc = lat_ref[pl.ds(start, tq), :]
        s = _dot_nt(qa, kc) + _dot_nt(qr, kr_ref[pl.ds(start, tq), :])
        s = jnp.where(k_off + j * tq <= q_pos, s, NEG_INF)
        m_old = m_s[...]
        m_new = jnp.maximum(m_old, jnp.max(s, axis=1, keepdims=True))
        alpha = jnp.exp(m_old - m_new)
        p = jnp.exp(s - m_new)
        l_s[...] = alpha * l_s[...] + jnp.sum(p, axis=1, keepdims=True)
        acc_s[...] = alpha * acc_s[...] + _dot(p, kc)
        m_s[...] = m_new
        return carry

    lax.fori_loop(0, qi + 1, step, 0)
    o_lat = acc_s[...] / l_s[...]
    for h in range(heads):
        o_ref[:, h * V_DIM:(h + 1) * V_DIM] = _dot(o_lat[h * tq:(h + 1) * tq, :], wuv_ref[h]).astype(o_ref.dtype)


def _attn_prompt(qa, qr, latb, krb, w_uv_t, batch, seq, tq=128):
    tq = min(tq, seq)
    nq = seq // tq
    rows = MLA_HEADS * tq
    return pl.pallas_call(
        _attn_prompt_kernel,
        grid=(batch, nq),
        in_specs=[pl.BlockSpec((MLA_HEADS, tq, KV_LORA), lambda b, i: (0, b * nq + i, 0)),
                  pl.BlockSpec((MLA_HEADS, tq, ROPE_DIM), lambda b, i: (0, b * nq + i, 0)),
                  pl.BlockSpec((seq, KV_LORA), lambda b, i: (b, 0)),
                  pl.BlockSpec((seq, ROPE_DIM), lambda b, i: (b, 0)),
                  pl.BlockSpec(w_uv_t.shape, lambda b, i: (0, 0, 0))],
        out_specs=pl.BlockSpec((tq, MLA_HEADS * V_DIM), lambda b, i: (b * nq + i, 0)),
        out_shape=jax.ShapeDtypeStruct((batch * seq, MLA_HEADS * V_DIM), BF16),
        scratch_shapes=[pltpu.VMEM((rows, 1), F32), pltpu.VMEM((rows, 1), F32), pltpu.VMEM((rows, KV_LORA), F32)],
        compiler_params=_cparams("arbitrary", "arbitrary"),
        name="attn_prompt",
    )(qa, qr, latb, krb, w_uv_t)


def _attn_decode_kernel(pt_ref, qa_ref, qr_ref, lat1_ref, kr1_ref, wuv_ref, *rest, pages):
    lat_refs = rest[:pages]
    kr_refs = rest[pages:2 * pages]
    o_ref, m_s, l_s, acc_s = rest[2 * pages:]
    j = pl.program_id(1)
    qa = qa_ref[0]
    qr = qr_ref[0]

    @pl.when(j == 0)
    def _():
        lat1 = lat1_ref[0].astype(F32)
        m_s[...] = (jnp.sum(qa.astype(F32) * lat1, axis=1, keepdims=True)
                    + jnp.sum(qr.astype(F32) * kr1_ref[0].astype(F32), axis=1, keepdims=True))
        l_s[...] = jnp.ones(l_s.shape, F32)
        acc_s[...] = jnp.broadcast_to(lat1, acc_s.shape)

    lats = [r[0].astype(BF16) for r in lat_refs]
    s = jnp.concatenate([_dot_nt(qa, lats[p]) + _dot_nt(qr, kr_refs[p][0]) for p in range(pages)], axis=1)
    m_old = m_s[...]
    m_new = jnp.maximum(m_old, jnp.max(s, axis=1, keepdims=True))
    alpha = jnp.exp(m_old - m_new)
    p_all = jnp.exp(s - m_new)
    l_s[...] = alpha * l_s[...] + jnp.sum(p_all, axis=1, keepdims=True)
    page = lats[0].shape[0]
    pv = _dot(p_all[:, :page], lats[0])
    for p in range(1, pages):
        pv = pv + _dot(p_all[:, p * page:(p + 1) * page], lats[p])
    acc_s[...] = alpha * acc_s[...] + pv
    m_s[...] = m_new

    @pl.when(j == pl.num_programs(1) - 1)
    def _():
        o_all = _dot(acc_s[...] / l_s[...], wuv_ref[...])
        for h in range(o_all.shape[0]):
            o_ref[0, :, h * V_DIM:(h + 1) * V_DIM] = o_all[h:h + 1, h * V_DIM:(h + 1) * V_DIM].astype(o_ref.dtype)


def _attn_decode(qa, qr, lat1, kr1, w_uv_flat, cache_latent, cache_krope, page_table, pages=16):
    batch, n_pages = page_table.shape
    pages = min(pages, n_pages)
    page = cache_latent.shape[1]
    steps = n_pages // pages
    one = lambda shape: pl.BlockSpec((1,) + shape, lambda b, j, pt: (b,) + (0,) * len(shape))
    paged = lambda w, p: pl.BlockSpec((1, page, w), lambda b, j, pt: (pt[b, j * pages + p], 0, 0))
    grid_spec = pltpu.PrefetchScalarGridSpec(
        num_scalar_prefetch=1,
        grid=(batch, steps),
        in_specs=[one((MLA_HEADS, KV_LORA)), one((MLA_HEADS, ROPE_DIM)), one((1, KV_LORA)), one((1, ROPE_DIM)),
                  pl.BlockSpec(w_uv_flat.shape, lambda b, j, pt: (0, 0))]
                 + [paged(KV_LORA, p) for p in range(pages)] + [paged(ROPE_DIM, p) for p in range(pages)],
        out_specs=one((1, MLA_HEADS * V_DIM)),
        scratch_shapes=[pltpu.VMEM((MLA_HEADS, 1), F32), pltpu.VMEM((MLA_HEADS, 1), F32),
                        pltpu.VMEM((MLA_HEADS, KV_LORA), F32)],
    )
    return pl.pallas_call(
        functools.partial(_attn_decode_kernel, pages=pages),
        grid_spec=grid_spec,
        out_shape=jax.ShapeDtypeStruct((batch, 1, MLA_HEADS * V_DIM), BF16),
        compiler_params=_cparams("arbitrary", "arbitrary"),
        name="attn_decode",
    )(page_table, qa, qr, lat1, kr1, w_uv_flat, *([cache_latent] * pages), *([cache_krope] * pages))


def _rope_tables(pos):
    half = ROPE_DIM // 2
    inv = ROPE_THETA ** (-jnp.arange(half, dtype=F32) / half)
    ang = pos.astype(F32)[:, None] * inv[None, :]
    cos, sin = jnp.cos(ang), jnp.sin(ang)
    return jnp.concatenate([cos, cos], axis=1), jnp.concatenate([-sin, sin], axis=1)


def _routing_tables(idx_t, rank_t, counts, n_rows):
    bm = EXPERT_ROWS
    padded = (counts + bm - 1) // bm * bm
    pad_end = jnp.cumsum(padded)
    pad_start = pad_end - padded
    dest_t = pad_start[idx_t] + rank_t
    block_start = jnp.arange(n_rows // bm, dtype=I32) * bm
    block_e = jnp.minimum(jnp.searchsorted(pad_end, block_start, side="right"), N_EXPERTS - 1).astype(I32)
    first = jnp.concatenate([jnp.ones((1,), I32), (block_e[1:] != block_e[:-1]).astype(I32)])
    n_used = (pad_end[-1:] // bm).astype(I32)
    return dest_t.T.reshape(-1).astype(I32), block_e, first, n_used


def kernel(x_prompt, x_sample, state_mlstm_C, state_mlstm_n, state_mlstm_m, cache_latent, cache_krope, page_table, c_prompt, c_sample, w_ada, b_ada, g_mix, g_ffn, w_mlstm_in, b_mlstm_gates, g_mlstm_head, w_mlstm_out, w_ada_kv, b_ada_kv, g_kv_in, w_dkv, g_kv_lat, w_uk, w_uv, w_dq, g_q_lat, w_uq, w_mla_out, w_router, b_router, w_exp_gate, w_exp_up, w_exp_down, w_sh_gate, w_sh_up, w_sh_down, g_final):
    bp, tp, d = x_prompt.shape
    bs, ts, _ = x_sample.shape
    depth = w_ada.shape[0]
    n_a = w_mlstm_in.shape[0]
    past_len = page_table.shape[1] * cache_latent.shape[1]
    hk, hv = ML_HEADS * ML_DK, ML_HEADS * ML_DV
    n_p, n_s = bp * tp, bs * ts
    n_tot = n_p + n_s
    n_rows = (n_tot * TOP_K + N_EXPERTS * (EXPERT_ROWS - 1) + EXPERT_ROWS - 1) // EXPERT_ROWS * EXPERT_ROWS

    groups = [
        dict(batch=bp, seq=tp, tok=_Tok(bp, tp, 256), x=x_prompt.reshape(n_p, d), pos0=0,
             C0=jnp.zeros((n_a, bp, ML_HEADS, ML_DK, ML_DV), F32), n0=jnp.zeros((n_a, bp, ML_HEADS, ML_DK), F32),
             m0=jnp.full((n_a, bp, ML_HEADS), M_INIT, F32)),
        dict(batch=bs, seq=ts, tok=_Tok(bs, ts, 256), x=x_sample.reshape(n_s, d), pos0=past_len,
             C0=state_mlstm_C, n0=state_mlstm_n, m0=state_mlstm_m),
    ]
    c_all = jnp.concatenate([c_prompt, c_sample], axis=0)
    row0 = [0, bp]
    for g in groups:
        g["Cs"], g["ns"], g["ms"] = [], [], []

    for g in groups:
        g["cos2"], g["sin2"] = _rope_tables(g["pos0"] + jnp.arange(g["seq"], dtype=I32))
        if g["seq"] == 1:
            g["cos2"] = jnp.broadcast_to(g["cos2"], (g["batch"], ROPE_DIM))
            g["sin2"] = jnp.broadcast_to(g["sin2"], (g["batch"], ROPE_DIM))

    mod_kv = _ada(c_all, w_ada_kv, b_ada_kv)
    w_uk_t = jnp.transpose(w_uk, (1, 2, 0)).astype(BF16)
    w_uv_t = jnp.transpose(w_uv, (1, 0, 2)).astype(BF16)

    for l in range(depth):
        mod = _ada(c_all, w_ada, b_ada[l], layer=l)
        w_router_t = w_router[l].T
        ws_gu = jnp.concatenate([w_sh_gate[l], w_sh_up[l]], axis=1).astype(BF16)
        ws_d = w_sh_down[l].astype(BF16)
        cnt = jnp.zeros((N_EXPERTS, 128), F32)
        for gi, g in enumerate(groups):
            r0, nb = row0[gi], g["batch"]
            sh_m, sc_m, g_m, sh_f, sc_f, g_f = [mod[r0:r0 + nb, i * d:(i + 1) * d] for i in range(6)]
            g["g_f"] = g_f
            tok = g["tok"]
            if l < n_a:
                w_in = w_mlstm_in[l]
                w_main = w_in[:, :2 * hk + 2 * hv].astype(BF16)
                w_gates = jnp.pad(w_in[:, 2 * hk + 2 * hv:], ((0, 0), (0, 128 - 2 * ML_HEADS)))
                b_gates = jnp.pad(b_mlstm_gates[l], (0, 128 - 2 * ML_HEADS)).reshape(1, 128)
                if g["seq"] == 1:
                    proj, gates = _in0(tok, g["x"], sh_m, sc_m, g_mix[l], w_main, w_gates, b_gates, F32)
                    a, Cn, nn, mn = _mlstm_step(proj, gates, g["C0"][l], g["n0"][l], g["m0"][l],
                                                g_mlstm_head[l], nb)
                    a = a.reshape(nb, hv)
                else:
                    proj, gates = _in0(tok, g["x"], sh_m, sc_m, g_mix[l], w_main, w_gates, b_gates, BF16)
                    a, Cn, nn, mn = _mlstm_chunks(proj, gates, g["C0"][l], g["n0"][l], g["m0"][l],
                                                  g_mlstm_head[l], nb, g["seq"])
                g["Cs"].append(Cn)
                g["ns"].append(nn)
                g["ms"].append(mn.reshape(nb, ML_HEADS))
                w_o = w_mlstm_out[l].astype(BF16)
            else:
                j = l - n_a
                w_q = w_uq[j].reshape(-1, MLA_HEADS, NOPE_DIM + ROPE_DIM)
                w_q_nope = w_q[:, :, :NOPE_DIM].reshape(-1, MLA_HEADS * NOPE_DIM).astype(BF16)
                w_q_rope = w_q[:, :, NOPE_DIM:].reshape(-1, MLA_HEADS * ROPE_DIM).astype(BF16)
                sh_kv, sc_kv = mod_kv[r0:r0 + nb, :d], mod_kv[r0:r0 + nb, d:]
                lat, kr, latb, krb, qa, qr = _in1(tok, g["x"], sh_kv, sc_kv, g_kv_in, w_dkv.astype(BF16), g_kv_lat,
                                                  g["cos2"], g["sin2"], sh_m, sc_m, g_mix[l], w_dq[j].astype(BF16),
                                                  g_q_lat[j], w_q_nope, w_q_rope, w_uk_t)
                if l == n_a:
                    g["lat"], g["kr"], g["latb"], g["krb"] = lat, kr, latb, krb
                if g["seq"] == 1:
                    a = _attn_decode(jnp.transpose(qa, (1, 0, 2)), jnp.transpose(qr, (1, 0, 2)),
                                     g["latb"].reshape(nb, 1, KV_LORA), g["krb"].reshape(nb, 1, ROPE_DIM),
                                     w_uv.reshape(KV_LORA, MLA_HEADS * V_DIM).astype(BF16),
                                     cache_latent, cache_krope, page_table).reshape(nb, MLA_HEADS * V_DIM)
                else:
                    a = _attn_prompt(qa, qr, g["latb"], g["krb"], w_uv_t, nb, g["seq"])
                w_o = w_mla_out[j].astype(BF16)
            g["xs"], g["hp"], g["idx"], g["wt"], g["rank"], cnt = _post(
                tok, g["x"], a, w_o, g_m, g_ffn[l], sh_f, sc_f, g_f, w_router_t, b_router[l], ws_gu, ws_d, cnt)

        idx_t = jnp.concatenate([g["idx"] for g in groups], axis=1)
        rank_t = jnp.concatenate([g["rank"] for g in groups], axis=1)
        dest, block_e, first, n_used = _routing_tables(idx_t, rank_t, cnt[:, 0].astype(I32), n_rows)
        xs_sorted = jnp.zeros((n_rows, d // 2), U32)
        for gi, g in enumerate(groups):
            t0 = 0 if gi == 0 else n_p
            g["dest"] = dest[t0 * TOP_K:(t0 + g["batch"] * g["seq"]) * TOP_K]
            xs_sorted = _scatter(g["hp"], g["dest"], xs_sorted, 256)
        out_sorted = _experts(xs_sorted, block_e, first, n_used, w_exp_gate, w_exp_up, w_exp_down, l)
        for g in groups:
            g["x"] = _combine((g["batch"], g["seq"]), g["xs"], g["wt"].T, g["g_f"], g_final, out_sorted, g["dest"],
                              l == depth - 1, 128)

    outs = []
    for g in groups:
        nb, seq = g["batch"], g["seq"]
        outs.append((g["x"].reshape(nb, seq, d), jnp.stack(g["Cs"]), jnp.stack(g["ns"]), jnp.stack(g["ms"]),
                     g["lat"].reshape(nb, seq, KV_LORA), g["kr"].reshape(nb, seq, ROPE_DIM)))
    p, s = outs
    return (p[0], s[0], p[1], p[2], p[3], p[4], p[5], s[1], s[2], s[3], s[4], s[5])
```

```python
import functools

import jax
import jax.numpy as jnp
from jax import lax
from jax.experimental import pallas as pl
from jax.experimental.pallas import tpu as pltpu

F32 = jnp.float32
BF16 = jnp.bfloat16
U32 = jnp.uint32
I32 = jnp.int32
HIGHEST = lax.Precision.HIGHEST
NEG_INF = float("-inf")

ML_HEADS = 4
ML_DK = 128
ML_DV = 256
ML_CHUNK = 128
M_INIT = -1e30
MLA_HEADS = 8
NOPE_DIM = 128
ROPE_DIM = 64
V_DIM = 128
KV_LORA = 256
ROPE_THETA = 10000.0
N_EXPERTS = 64
TOP_K = 8
N_GROUPS = 8
TOPK_GROUPS = 4
ROUTED_SCALE = 2.5
NORM_EPS = 1e-6

LANES = 128
EXPERT_ROWS = 512
VMEM_LIMIT = 56 * 1024 * 1024


def _cparams(*sem):
    return pltpu.CompilerParams(dimension_semantics=sem, vmem_limit_bytes=VMEM_LIMIT)


def _dot(a, b):
    return jnp.dot(a.astype(BF16), b.astype(BF16), preferred_element_type=F32)


def _dot_nt(a, b):
    return lax.dot_general(a.astype(BF16), b.astype(BF16), (((1,), (1,)), ((), ())),
                           preferred_element_type=F32)


def _dot_f32(a, b):
    return jnp.dot(a, b, precision=HIGHEST, preferred_element_type=F32)


def _dot_w(a, w):
    return _dot_f32(a.astype(F32), w) if w.dtype == F32 else _dot(a, w)


def _dot_w_nt(w, a):
    if w.dtype == F32:
        return lax.dot_general(w, a.astype(F32), (((1,), (1,)), ((), ())), precision=HIGHEST,
                               preferred_element_type=F32)
    return _dot_nt(w, a)


def _rms(x, g):
    return x * lax.rsqrt(jnp.mean(x * x, axis=-1, keepdims=True) + NORM_EPS) * g


def _rms_mod(x, g, shift, scale):
    return _rms(x, g) * (1.0 + scale) + shift


def _sigmoid(x):
    return 1.0 / (1.0 + jnp.exp(-x))


def _silu(x):
    return x * _sigmoid(x)


def _log_sigmoid(x):
    return jnp.minimum(x, 0.0) - jnp.log1p(jnp.exp(-jnp.abs(x)))


def _ada_kernel(c_ref, w_ref, b_ref, o_ref):
    o_ref[...] = _dot_f32(_silu(c_ref[...]), w_ref[...]) + b_ref[...]


def _ada(c, w, b, layer=None, tn=1024):
    m, d = c.shape
    n_out = w.shape[-1]
    if layer is None:
        w_spec = pl.BlockSpec((d, tn), lambda j: (0, j))
    else:
        w_spec = pl.BlockSpec((None, d, tn), lambda j: (layer, 0, j))
    return pl.pallas_call(
        _ada_kernel,
        grid=(n_out // tn,),
        in_specs=[pl.BlockSpec((m, d), lambda j: (0, 0)), w_spec, pl.BlockSpec((1, tn), lambda j: (0, j))],
        out_specs=pl.BlockSpec((m, tn), lambda j: (0, j)),
        out_shape=jax.ShapeDtypeStruct((m, n_out), F32),
        compiler_params=_cparams("arbitrary"),
        name="ada",
    )(c, w, b.reshape(1, n_out))


class _Tok:
    def __init__(self, batch, seq, tile):
        if seq == 1:
            self.tm, self.n_tiles, self.per_seq = batch, 1, 1
            self.mod_block = (1, batch, None)
        else:
            self.tm = min(tile, seq)
            self.per_seq = seq // self.tm
            self.n_tiles = batch * self.per_seq
            self.mod_block = (1, 1, None)
        self.batch, self.seq = batch, seq

    def mod(self, v):
        d = v.shape[-1]
        if self.seq == 1:
            return v.reshape(1, self.batch, d), pl.BlockSpec((1, self.batch, d), lambda i, *_: (0, 0, 0))
        per = self.per_seq
        return v.reshape(self.batch, 1, d), pl.BlockSpec((1, 1, d), lambda i, *_: (i // per, 0, 0))

    def rows(self, width):
        return pl.BlockSpec((self.tm, width), lambda i, *_: (i, 0))

    def seq_rows(self, width):
        per = self.per_seq
        return pl.BlockSpec((self.tm, width), lambda i, *_: (i % per, 0))


def _full(shape):
    nd = len(shape)
    return pl.BlockSpec(shape, lambda i, *_: (0,) * nd)


def _in0_kernel(x_ref, sh_ref, sc_ref, g_ref, w_ref, wg_ref, bg_ref, o_ref, gt_ref):
    h = _rms_mod(x_ref[...], g_ref[...], sh_ref[0], sc_ref[0])
    o_ref[...] = _dot_w(h, w_ref[...]).astype(o_ref.dtype)
    gt_ref[...] = _dot_f32(h, wg_ref[...]) + bg_ref[...]


def _in0(tok, x, shift, scale, g, w, w_gates, b_gates, out_dtype):
    n, d = x.shape
    wo = w.shape[1]
    sh, sh_spec = tok.mod(shift)
    sc, sc_spec = tok.mod(scale)
    return pl.pallas_call(
        _in0_kernel,
        grid=(tok.n_tiles,),
        in_specs=[tok.rows(d), sh_spec, sc_spec, _full((1, d)), _full((d, wo)), _full((d, 128)), _full((1, 128))],
        out_specs=[tok.rows(wo), tok.rows(128)],
        out_shape=[jax.ShapeDtypeStruct((n, wo), out_dtype), jax.ShapeDtypeStruct((n, 128), F32)],
        compiler_params=_cparams("arbitrary"),
        name="in0",
    )(x, sh, sc, g.reshape(1, d), w, w_gates, b_gates)


def _mlstm_chunk_kernel(q_ref, k_ref, v_ref, og_ref, gt_ref, c0_ref, n0_ref, m0_ref, gh_ref,
                        hh_ref, c_ref, n_ref, m_ref):
    L = q_ref.shape[0]
    scale = ML_DK ** -0.5

    @pl.when(pl.program_id(1) == 0)
    def _():
        c_ref[...] = c0_ref[...]
        n_ref[...] = n0_ref[...]
        m_ref[...] = m0_ref[...]

    gates = gt_ref[...]
    logf = _log_sigmoid(gates)
    row = lax.broadcasted_iota(I32, (L, L), 0)
    col = lax.broadcasted_iota(I32, (L, L), 1)
    causal = col <= row
    tril = causal.astype(F32)
    b_cols = _dot_f32(tril, logf)
    gates_t = gates.T
    b_rows = _dot_f32(logf.T, (row <= col).astype(F32))

    for h in range(ML_HEADS):
        i_row = gates_t[h:h + 1, :]
        i_col = gates[:, h:h + 1]
        b_row = b_rows[ML_HEADS + h:ML_HEADS + h + 1, :]
        b_col = b_cols[:, ML_HEADS + h:ML_HEADS + h + 1]
        m_prev = m_ref[0, :, h:h + 1]
        c_prev = c_ref[0, h]
        n_prev = n_ref[0, h:h + 1, :]
        q = q_ref[:, h * ML_DK:(h + 1) * ML_DK]
        k = k_ref[:, h * ML_DK:(h + 1) * ML_DK]
        v = v_ref[:, h * ML_DV:(h + 1) * ML_DV]

        g_col = b_col + m_prev
        dmat = jnp.where(causal, b_col - b_row + i_row, NEG_INF)
        m_t = jnp.maximum(g_col, jnp.max(dmat, axis=1, keepdims=True))
        w_intra = jnp.exp(dmat - m_t)
        w_inter = jnp.exp(g_col - m_t)
        s = _dot_nt(q, k) * (w_intra * scale)
        num = _dot(s, v) + w_inter * _dot(q, c_prev)
        qn = jnp.sum(q.astype(F32) * n_prev, axis=1, keepdims=True)
        den = jnp.sum(s, axis=1, keepdims=True) + w_inter * qn
        cell = num / jnp.maximum(jnp.abs(den), jnp.exp(-m_t))

        b_last = b_col[L - 1:L, :]
        m_new = m_t[L - 1:L, :]
        a_inter = jnp.exp(b_last + m_prev - m_new)
        a_intra = jnp.exp(b_last - b_col + i_col - m_new) * scale
        kw = k.astype(F32) * a_intra
        c_ref[0, h] = a_inter * c_prev + _dot(kw.T, v)
        n_ref[0, h:h + 1, :] = a_inter * n_prev + jnp.sum(kw, axis=0, keepdims=True)
        m_ref[0, :, h:h + 1] = m_new

        y = _rms(cell, gh_ref[:, h * ML_DV:(h + 1) * ML_DV])
        og = og_ref[:, h * ML_DV:(h + 1) * ML_DV].astype(F32)
        hh_ref[:, h * ML_DV:(h + 1) * ML_DV] = (y * _sigmoid(og)).astype(hh_ref.dtype)


def _mlstm_chunks(proj, gates, c0, n0, m0, g_head, batch, seq):
    L = seq if seq <= ML_CHUNK else ML_CHUNK
    nc = seq // L
    hk, hv = ML_HEADS * ML_DK, ML_HEADS * ML_DV
    rows = lambda w, cb: pl.BlockSpec((L, w), lambda b, j: (b * nc + j, cb))
    state = lambda shape: pl.BlockSpec((1,) + shape, lambda b, j: (b,) + (0,) * len(shape))
    return pl.pallas_call(
        _mlstm_chunk_kernel,
        grid=(batch, nc),
        in_specs=[rows(hk, 0), rows(hk, 1), rows(hv, 2 * hk // hv), rows(hv, 2 * hk // hv + 1), rows(128, 0),
                  state((ML_HEADS, ML_DK, ML_DV)), state((ML_HEADS, ML_DK)), state((1, ML_HEADS)),
                  pl.BlockSpec((1, hv), lambda b, j: (0, 0))],
        out_specs=[rows(hv, 0), state((ML_HEADS, ML_DK, ML_DV)), state((ML_HEADS, ML_DK)), state((1, ML_HEADS))],
        out_shape=[jax.ShapeDtypeStruct((batch * seq, hv), BF16),
                   jax.ShapeDtypeStruct((batch, ML_HEADS, ML_DK, ML_DV), F32),
                   jax.ShapeDtypeStruct((batch, ML_HEADS, ML_DK), F32),
                   jax.ShapeDtypeStruct((batch, 1, ML_HEADS), F32)],
        compiler_params=_cparams("arbitrary", "arbitrary"),
        name="mlstm_chunks",
    )(proj, proj, proj, proj, gates, c0, n0, m0.reshape(batch, 1, ML_HEADS), g_head.reshape(1, hv))


def _row_to_col(r):
    n = r.shape[1]
    eye = lax.broadcasted_iota(I32, (n, n), 0) == lax.broadcasted_iota(I32, (n, n), 1)
    return jnp.sum(jnp.where(eye, jnp.broadcast_to(r, (n, n)), 0.0), axis=1, keepdims=True)


def _mlstm_step_kernel(p_ref, gt_ref, c0_ref, n0_ref, m0_ref, gh_ref, hh_ref, c_ref, n_ref, m_ref):
    scale = ML_DK ** -0.5
    hk, hv = ML_HEADS * ML_DK, ML_HEADS * ML_DV
    gates = gt_ref[0]
    logf = _log_sigmoid(gates)
    for h in range(ML_HEADS):
        q = p_ref[0, :, h * ML_DK:(h + 1) * ML_DK].astype(F32)
        k = p_ref[0, :, hk + h * ML_DK:hk + (h + 1) * ML_DK].astype(F32)
        v = p_ref[0, :, 2 * hk + h * ML_DV:2 * hk + (h + 1) * ML_DV].astype(F32)
        og = p_ref[0, :, 2 * hk + hv + h * ML_DV:2 * hk + hv + (h + 1) * ML_DV].astype(F32)
        log_i = gates[:, h:h + 1]
        m_prev = m0_ref[0, :, h:h + 1]
        c_prev = c0_ref[0, h]
        n_prev = n0_ref[0, h:h + 1, :]

        g = logf[:, ML_HEADS + h:ML_HEADS + h + 1] + m_prev
        m_t = jnp.maximum(g, log_i)
        w_intra = jnp.exp(log_i - m_t)
        w_inter = jnp.exp(g - m_t)
        s = jnp.sum(q * k, axis=1, keepdims=True) * (w_intra * scale)
        q_c = _row_to_col(q)
        num = s * v + w_inter * jnp.sum(q_c * c_prev, axis=0, keepdims=True)
        den = s + w_inter * jnp.sum(q * n_prev, axis=1, keepdims=True)
        cell = num / jnp.maximum(jnp.abs(den), jnp.exp(-m_t))

        kw = k * (w_intra * scale)
        c_ref[0, h] = w_inter * c_prev + _row_to_col(kw) * v
        n_ref[0, h:h + 1, :] = w_inter * n_prev + kw
        m_ref[0, :, h:h + 1] = m_t

        y = _rms(cell, gh_ref[:, h * ML_DV:(h + 1) * ML_DV])
        hh_ref[0, :, h * ML_DV:(h + 1) * ML_DV] = (y * _sigmoid(og)).astype(hh_ref.dtype)


def _mlstm_step(proj, gates, c0, n0, m0, g_head, batch):
    hv = ML_HEADS * ML_DV
    wp = proj.shape[1]
    one = lambda shape: pl.BlockSpec((1,) + shape, lambda b: (b,) + (0,) * len(shape))
    return pl.pallas_call(
        _mlstm_step_kernel,
        grid=(batch,),
        in_specs=[one((1, wp)), one((1, 128)), one((ML_HEADS, ML_DK, ML_DV)), one((ML_HEADS, ML_DK)),
                  one((1, ML_HEADS)), pl.BlockSpec((1, hv), lambda b: (0, 0))],
        out_specs=[one((1, hv)), one((ML_HEADS, ML_DK, ML_DV)), one((ML_HEADS, ML_DK)), one((1, ML_HEADS))],
        out_shape=[jax.ShapeDtypeStruct((batch, 1, hv), F32),
                   jax.ShapeDtypeStruct((batch, ML_HEADS, ML_DK, ML_DV), F32),
                   jax.ShapeDtypeStruct((batch, ML_HEADS, ML_DK), F32),
                   jax.ShapeDtypeStruct((batch, 1, ML_HEADS), F32)],
        compiler_params=_cparams("arbitrary"),
        name="mlstm_step",
    )(proj.reshape(batch, 1, wp), gates.reshape(batch, 1, 128), c0, n0, m0.reshape(batch, 1, ML_HEADS),
      g_head.reshape(1, hv))


def _group_reduce(x, op):
    e, tm = x.shape
    per = e // N_GROUPS
    x3 = x.reshape(N_GROUPS, per, tm)
    r = op(x3, axis=1, keepdims=True)
    return jnp.broadcast_to(r, (N_GROUPS, per, tm)).reshape(e, tm)


def _route(logits_t, bias_t):
    e, tm = logits_t.shape
    per = e // N_GROUPS
    scores = _sigmoid(logits_t)
    sel = scores + bias_t
    eidx = lax.broadcasted_iota(I32, (e, tm), 0)
    jidx = eidx % per
    gidx = eidx // per
    m1 = _group_reduce(sel, jnp.max)
    first1 = _group_reduce(jnp.where(sel == m1, jidx, per), jnp.min)
    m2 = _group_reduce(jnp.where(jidx == first1, NEG_INF, sel), jnp.max)
    gs = m1 + m2
    chosen = jnp.zeros((e, tm), F32)
    for _ in range(TOPK_GROUPS):
        mx = jnp.max(gs, axis=0, keepdims=True)
        f = jnp.min(jnp.where(gs == mx, gidx, N_GROUPS), axis=0, keepdims=True)
        hit = gidx == f
        chosen = jnp.where(hit, 1.0, chosen)
        gs = jnp.where(hit, NEG_INF, gs)
    cand = jnp.where(chosen > 0.0, sel, NEG_INF)
    picked = jnp.zeros((e, tm), F32)
    idx, wts = [], []
    for _ in range(TOP_K):
        mx = jnp.max(cand, axis=0, keepdims=True)
        f = jnp.min(jnp.where(cand == mx, eidx, e), axis=0, keepdims=True)
        hit = eidx == f
        idx.append(f)
        wts.append(jnp.sum(jnp.where(hit, scores, 0.0), axis=0, keepdims=True))
        cand = jnp.where(hit, NEG_INF, cand)
        picked = jnp.where(hit, 1.0, picked)
    return idx, wts, picked


def _post_kernel(x_ref, a_ref, wo_ref, gm_ref, gn_ref, sh_ref, sc_ref, gf_ref, wr_ref, br_ref,
                 wsgu_ref, wsd_ref, cnt0_ref,
                 xs_ref, hp_ref, idx_ref, wt_ref, rank_ref, cnt_ref, carry):
    tm, d = x_ref.shape
    half = d // 2

    @pl.when(pl.program_id(0) == 0)
    def _():
        carry[...] = cnt0_ref[...]

    x1 = x_ref[...] + gm_ref[0] * _dot_w(a_ref[...], wo_ref[...])
    h2 = _rms_mod(x1, gn_ref[...], sh_ref[0], sc_ref[0])

    hi = lax.bitcast_convert_type(h2[:, :half].astype(BF16).astype(F32), U32)
    lo = lax.bitcast_convert_type(h2[:, half:].astype(BF16).astype(F32), U32)
    hp_ref[...] = hi | (lo >> 16)

    ff = wsd_ref.shape[0]
    gu = _dot(h2, wsgu_ref[...])
    shared = _dot(_silu(gu[:, :ff]) * gu[:, ff:], wsd_ref[...])
    xs_ref[...] = x1 + gf_ref[0] * shared

    logits_t = _dot_w_nt(wr_ref[...], h2)
    idx, wts, picked = _route(logits_t, br_ref[:, 0:1])
    e = picked.shape[0]
    eidx = lax.broadcasted_iota(I32, (e, tm), 0)
    before = (lax.broadcasted_iota(I32, (tm, tm), 0) < lax.broadcasted_iota(I32, (tm, tm), 1)).astype(BF16)
    base = carry[:, 0:1]
    rank_t = jnp.dot(picked.astype(BF16), before, preferred_element_type=F32) + base
    total = wts[0]
    for k in range(1, TOP_K):
        total = total + wts[k]
    for k in range(TOP_K):
        idx_ref[k:k + 1, :] = idx[k]
        wt_ref[k:k + 1, :] = wts[k] / total * ROUTED_SCALE
        rank_ref[k:k + 1, :] = jnp.sum(jnp.where(eidx == idx[k], rank_t, 0.0), axis=0, keepdims=True).astype(I32)
    new = base + jnp.sum(picked, axis=1, keepdims=True)
    carry[...] = jnp.broadcast_to(new, carry.shape)
    cnt_ref[...] = jnp.broadcast_to(new, cnt_ref.shape)


def _post(tok, x, a, w_o, g_m, g_norm, shift, scale, g_f, w_router_t, b_router, ws_gu, ws_d, cnt0):
    n, d = x.shape
    e = w_router_t.shape[0]
    ff = ws_d.shape[0]
    gm, gm_spec = tok.mod(g_m)
    sh, sh_spec = tok.mod(shift)
    sc, sc_spec = tok.mod(scale)
    gf, gf_spec = tok.mod(g_f)
    cols = lambda rows: pl.BlockSpec((rows, tok.tm), lambda i: (0, i))
    return pl.pallas_call(
        _post_kernel,
        grid=(tok.n_tiles,),
        in_specs=[tok.rows(d), tok.rows(a.shape[1]), _full(w_o.shape), gm_spec, _full((1, d)), sh_spec, sc_spec,
                  gf_spec, _full((e, d)), _full((e, 128)), _full((d, 2 * ff)), _full((ff, d)), _full((e, 128))],
        out_specs=[tok.rows(d), tok.rows(d // 2), cols(TOP_K), cols(TOP_K), cols(TOP_K), _full((e, 128))],
        out_shape=[jax.ShapeDtypeStruct((n, d), F32), jax.ShapeDtypeStruct((n, d // 2), U32),
                   jax.ShapeDtypeStruct((TOP_K, n), I32), jax.ShapeDtypeStruct((TOP_K, n), F32),
                   jax.ShapeDtypeStruct((TOP_K, n), I32), jax.ShapeDtypeStruct((e, 128), F32)],
        scratch_shapes=[pltpu.VMEM((e, 128), F32)],
        compiler_params=_cparams("arbitrary"),
        name="post",
    )(x, a, w_o, gm, g_norm.reshape(1, d), sh, sc, gf, w_router_t,
      jnp.broadcast_to(b_router.reshape(e, 1), (e, 128)), ws_gu, ws_d, cnt0)


def _dest_kernel(ps_ref, idx_ref, rank_ref, o_ref):
    idx = idx_ref[...]
    acc = rank_ref[...]
    for e in range(N_EXPERTS):
        acc = acc + jnp.where(idx == e, ps_ref[e], 0)
    o_ref[...] = acc


def _dest(pad_start, idx_t, rank_t, tn=2048):
    k, n = idx_t.shape
    tn = min(tn, n)
    spec = pl.BlockSpec((k, tn), lambda i, ps: (0, i))
    return pl.pallas_call(
        _dest_kernel,
        grid_spec=pltpu.PrefetchScalarGridSpec(num_scalar_prefetch=1, grid=(n // tn,), in_specs=[spec, spec],
                                               out_specs=spec),
        out_shape=jax.ShapeDtypeStruct((k, n), I32),
        compiler_params=_cparams("arbitrary"),
        name="dest",
    )(pad_start, idx_t, rank_t)


def _row_copy(hp_ref, xs_ref, dest_ref, sem, r, k):
    return pltpu.make_async_copy(hp_ref.at[pl.ds(r, 1)], xs_ref.at[pl.ds(dest_ref[k, r], 1)], sem)


def _scatter_kernel(pe_ref, pd_ref, dest_ref, hp_ref, *rest, zero_fill):
    if zero_fill:
        xs_ref, sem, zeros = rest
    else:
        _, xs_ref, sem = rest
    tm = hp_ref.shape[0]

    if zero_fill:
        bm = zeros.shape[0]

        def last_block(e):
            return pltpu.make_async_copy(zeros, xs_ref.at[pl.ds(pl.multiple_of(pe_ref[e] - bm, bm), bm)], sem)

        @pl.when(pl.program_id(0) == 0)
        def _():
            zeros[...] = jnp.zeros_like(zeros)

            def zstart(e, c):
                @pl.when(pd_ref[e] > 0)
                def _():
                    last_block(e).start()
                return c

            def zwait(e, c):
                @pl.when(pd_ref[e] > 0)
                def _():
                    last_block(e).wait()
                return c

            lax.fori_loop(0, N_EXPERTS, zstart, 0)
            lax.fori_loop(0, N_EXPERTS, zwait, 0)

    def start(r, c):
        for k in range(TOP_K):
            _row_copy(hp_ref, xs_ref, dest_ref, sem, r, k).start()
        return c

    def wait(r, c):
        for k in range(TOP_K):
            _row_copy(hp_ref, xs_ref, dest_ref, sem, r, k).wait()
        return c

    lax.fori_loop(0, tm, start, 0)
    lax.fori_loop(0, tm, wait, 0)


def _scatter(hp, dest_t, pad_end, padded, xs, n_rows, tm):
    n, w = hp.shape
    tm = min(tm, n)
    zero_fill = xs is None
    in_specs = [pl.BlockSpec((TOP_K, tm), lambda i, *_: (0, i), memory_space=pltpu.SMEM),
                pl.BlockSpec((tm, w), lambda i, *_: (i, 0))]
    scratch = [pltpu.SemaphoreType.DMA]
    args = [pad_end, padded, dest_t, hp]
    if zero_fill:
        scratch.append(pltpu.VMEM((EXPERT_ROWS, w), hp.dtype))
    else:
        in_specs.append(pl.BlockSpec(memory_space=pl.ANY))
        args.append(xs)
    return pl.pallas_call(
        functools.partial(_scatter_kernel, zero_fill=zero_fill),
        grid_spec=pltpu.PrefetchScalarGridSpec(num_scalar_prefetch=2, grid=(n // tm,), in_specs=in_specs,
                                               out_specs=pl.BlockSpec(memory_space=pl.ANY), scratch_shapes=scratch),
        out_shape=jax.ShapeDtypeStruct((n_rows, w), hp.dtype),
        input_output_aliases={} if zero_fill else {4: 0},
        compiler_params=_cparams("arbitrary"),
        name="dispatch",
    )(*args)


def _expert_kernel(be_ref, first_ref, nu_ref, x_ref, wg_ref, wu_ref, wd_ref, o_ref, wgu_s, wd_s):
    i = pl.program_id(0)
    ff = wd_s.shape[0]
    half = x_ref.shape[1]

    @pl.when(i < nu_ref[0])
    def _():
        @pl.when(first_ref[i] == 1)
        def _():
            wgu_s[:, :ff] = wg_ref[...].astype(BF16)
            wgu_s[:, ff:] = wu_ref[...].astype(BF16)
            wd_s[...] = wd_ref[...].astype(BF16)

        p = x_ref[...]
        hi = lax.bitcast_convert_type(p & jnp.uint32(0xFFFF0000), F32).astype(BF16)
        lo = lax.bitcast_convert_type(p << 16, F32).astype(BF16)
        gu = (jnp.dot(hi, wgu_s[:half, :], preferred_element_type=F32)
              + jnp.dot(lo, wgu_s[half:, :], preferred_element_type=F32))
        act = _silu(gu[:, :ff]) * gu[:, ff:]
        o_ref[...] = jnp.dot(act.astype(BF16), wd_s[...], preferred_element_type=F32)

    @pl.when(i >= nu_ref[0])
    def _():
        o_ref[...] = jnp.zeros_like(o_ref)


def _experts(xs, block_e, first, n_used, w_gate, w_up, w_down, layer):
    n_rows, half = xs.shape
    d = 2 * half
    ff = w_gate.shape[-1]
    bm = EXPERT_ROWS
    grid_spec = pltpu.PrefetchScalarGridSpec(
        num_scalar_prefetch=3,
        grid=(n_rows // bm,),
        in_specs=[pl.BlockSpec((bm, half), lambda i, be, fi, nu: (jnp.minimum(i, nu[0] - 1), 0)),
                  pl.BlockSpec((None, None, d, ff), lambda i, be, fi, nu: (layer, be[i], 0, 0)),
                  pl.BlockSpec((None, None, d, ff), lambda i, be, fi, nu: (layer, be[i], 0, 0)),
                  pl.BlockSpec((None, None, ff, d), lambda i, be, fi, nu: (layer, be[i], 0, 0))],
        out_specs=pl.BlockSpec((bm, d), lambda i, be, fi, nu: (i, 0)),
        scratch_shapes=[pltpu.VMEM((d, 2 * ff), BF16), pltpu.VMEM((ff, d), BF16)],
    )
    return pl.pallas_call(
        _expert_kernel,
        grid_spec=grid_spec,
        out_shape=jax.ShapeDtypeStruct((n_rows, d), F32),
        compiler_params=_cparams("arbitrary"),
        name="experts",
    )(block_e, first, n_used, xs, w_gate, w_up, w_down)


def _combine_kernel(dest_ref, xs_ref, w_ref, gf_ref, gfin_ref, os_ref, y_ref, buf, sem, *, final_norm):
    tm = xs_ref.shape[0]

    def copy(r, k):
        return pltpu.make_async_copy(os_ref.at[pl.ds(dest_ref[k, r], 1)], buf.at[k, pl.ds(r, 1)], sem)

    def start(r, c):
        for k in range(TOP_K):
            copy(r, k).start()
        return c

    def wait(r, c):
        for k in range(TOP_K):
            copy(r, k).wait()
        return c

    lax.fori_loop(0, tm, start, 0)
    lax.fori_loop(0, tm, wait, 0)
    w = w_ref[...]
    acc = _row_to_col(w[0:1, :]) * buf[0]
    for k in range(1, TOP_K):
        acc = acc + _row_to_col(w[k:k + 1, :]) * buf[k]
    x2 = xs_ref[...] + gf_ref[0] * acc
    y_ref[...] = _rms(x2, gfin_ref[...]) if final_norm else x2


def _combine(tok_seq, xs, wts_t, g_f, g_final, out_sorted, dest_t, final_norm, tm):
    n, d = xs.shape
    batch, seq = tok_seq
    tok = _Tok(batch, seq, tm)
    gf, gf_spec = tok.mod(g_f)
    tm = tok.tm
    return pl.pallas_call(
        functools.partial(_combine_kernel, final_norm=final_norm),
        grid=(tok.n_tiles,),
        in_specs=[pl.BlockSpec((TOP_K, tm), lambda i: (0, i), memory_space=pltpu.SMEM),
                  tok.rows(d), pl.BlockSpec((TOP_K, tm), lambda i: (0, i)), gf_spec, _full((1, d)),
                  pl.BlockSpec(memory_space=pl.ANY)],
        out_specs=tok.rows(d),
        out_shape=jax.ShapeDtypeStruct((n, d), F32),
        scratch_shapes=[pltpu.VMEM((TOP_K, tm, d), F32), pltpu.SemaphoreType.DMA],
        compiler_params=_cparams("arbitrary"),
        name="combine",
    )(dest_t, xs, wts_t, gf, g_final.reshape(1, d), out_sorted)


def _rope(x, cos2, sin2):
    half = x.shape[1] // 2
    swapped = jnp.concatenate([x[:, half:], x[:, :half]], axis=1)
    return x * cos2 + swapped * sin2


def _in1_kernel(x_ref, shk_ref, sck_ref, gk_ref, wdkv_ref, glat_ref, cos_ref, sin_ref,
                shm_ref, scm_ref, gm_ref, wdq_ref, gq_ref, wqn_ref, wqr_ref, wuk_ref,
                lat_ref, kr_ref, latb_ref, krb_ref, qa_ref, qr_ref):
    x = x_ref[...]
    cos2, sin2 = cos_ref[...], sin_ref[...]
    att_scale = (NOPE_DIM + ROPE_DIM) ** -0.5

    hk = _rms_mod(x, gk_ref[...], shk_ref[0], sck_ref[0])
    ckr = _dot_w(hk, wdkv_ref[...])
    lat = _rms(ckr[:, :KV_LORA], glat_ref[...])
    kr = _rope(ckr[:, KV_LORA:], cos2, sin2)
    lat_ref[...] = lat
    kr_ref[...] = kr
    latb_ref[...] = lat.astype(BF16)
    krb_ref[...] = kr.astype(BF16)

    hm = _rms_mod(x, gm_ref[...], shm_ref[0], scm_ref[0])
    q_lat = _rms(_dot_w(hm, wdq_ref[...]), gq_ref[...])
    q_nope = _dot_w(q_lat, wqn_ref[...])
    q_rope = _dot_w(q_lat, wqr_ref[...])
    for h in range(MLA_HEADS):
        qa = _dot_w(q_nope[:, h * NOPE_DIM:(h + 1) * NOPE_DIM], wuk_ref[h])
        qa_ref[h] = (qa * att_scale).astype(BF16)
        qr = _rope(q_rope[:, h * ROPE_DIM:(h + 1) * ROPE_DIM], cos2, sin2)
        qr_ref[h] = (qr * att_scale).astype(BF16)


def _in1(tok, x, sh_kv, sc_kv, g_kv_in, w_dkv, g_kv_lat, cos2, sin2, sh_m, sc_m, g_mix, w_dq, g_q_lat,
         w_q_nope, w_q_rope, w_uk_t):
    n, d = x.shape
    q_lora = w_dq.shape[1]
    shk, shk_spec = tok.mod(sh_kv)
    sck, sck_spec = tok.mod(sc_kv)
    shm, shm_spec = tok.mod(sh_m)
    scm, scm_spec = tok.mod(sc_m)
    heads = lambda w: pl.BlockSpec((MLA_HEADS, tok.tm, w), lambda i: (0, i, 0))
    return pl.pallas_call(
        _in1_kernel,
        grid=(tok.n_tiles,),
        in_specs=[tok.rows(d), shk_spec, sck_spec, _full((1, d)), _full(w_dkv.shape), _full((1, KV_LORA)),
                  tok.seq_rows(ROPE_DIM), tok.seq_rows(ROPE_DIM),
                  shm_spec, scm_spec, _full((1, d)), _full(w_dq.shape), _full((1, q_lora)),
                  _full(w_q_nope.shape), _full(w_q_rope.shape), _full(w_uk_t.shape)],
        out_specs=[tok.rows(KV_LORA), tok.rows(ROPE_DIM), tok.rows(KV_LORA), tok.rows(ROPE_DIM),
                   heads(KV_LORA), heads(ROPE_DIM)],
        out_shape=[jax.ShapeDtypeStruct((n, KV_LORA), F32), jax.ShapeDtypeStruct((n, ROPE_DIM), F32),
                   jax.ShapeDtypeStruct((n, KV_LORA), BF16), jax.ShapeDtypeStruct((n, ROPE_DIM), BF16),
                   jax.ShapeDtypeStruct((MLA_HEADS, n, KV_LORA), BF16),
                   jax.ShapeDtypeStruct((MLA_HEADS, n, ROPE_DIM), BF16)],
        compiler_params=_cparams("arbitrary"),
        name="in1",
    )(x, shk, sck, g_kv_in.reshape(1, d), w_dkv, g_kv_lat.reshape(1, KV_LORA), cos2, sin2,
      shm, scm, g_mix.reshape(1, d), w_dq, g_q_lat.reshape(1, q_lora), w_q_nope, w_q_rope, w_uk_t)


def _attn_prompt_kernel(qa_ref, qr_ref, lat_ref, kr_ref, wuv_ref, o_ref, m_s, l_s, acc_s):
    heads, tq, c = qa_ref.shape
    qi = pl.program_id(1)
    n_lt = tq // LANES
    m_s[...] = jnp.full(m_s.shape, NEG_INF, F32)
    l_s[...] = jnp.zeros(l_s.shape, F32)
    acc_s[...] = jnp.zeros(acc_s.shape, F32)
    causal = lax.broadcasted_iota(I32, (tq, tq), 1) <= lax.broadcasted_iota(I32, (tq, tq), 0)

    def chunk(j, diagonal):
        start = pl.multiple_of(j * tq, tq)
        kc = lat_ref[pl.ds(start, tq), :]
        krc = kr_ref[pl.ds(start, tq), :]
        for h in range(heads):
            s = _dot_nt(qa_ref[h], kc) + _dot_nt(qr_ref[h], krc)
            if diagonal:
                s = jnp.where(causal, s, NEG_INF)
            tiles = [s[:, t * LANES:(t + 1) * LANES] for t in range(n_lt)]
            mx = tiles[0]
            for t in tiles[1:]:
                mx = jnp.maximum(mx, t)
            m_old = m_s[h]
            m_new = jnp.maximum(m_old, jnp.broadcast_to(jnp.max(mx, axis=1, keepdims=True), (tq, LANES)))
            alpha = jnp.exp(m_old - m_new)
            ps = [jnp.exp(t - m_new) for t in tiles]
            part = ps[0]
            for p in ps[1:]:
                part = part + p
            l_s[h] = alpha * l_s[h] + part
            pv = _dot(jnp.concatenate(ps, axis=1), kc)
            acc_s[h] = jnp.concatenate([alpha] * (c // LANES), axis=1) * acc_s[h] + pv
            m_s[h] = m_new

    def step(j, carry):
        chunk(j, False)
        return carry

    lax.fori_loop(0, qi, step, 0)
    chunk(qi, True)
    for h in range(heads):
        o_lat = acc_s[h] / jnp.sum(l_s[h], axis=1, keepdims=True)
        o_ref[:, h * V_DIM:(h + 1) * V_DIM] = _dot(o_lat, wuv_ref[h]).astype(o_ref.dtype)


def _attn_prompt(qa, qr, latb, krb, w_uv_t, batch, seq, tq=256):
    tq = min(tq, seq)
    nq = seq // tq
    return pl.pallas_call(
        _attn_prompt_kernel,
        grid=(batch, nq),
        in_specs=[pl.BlockSpec((MLA_HEADS, tq, KV_LORA), lambda b, i: (0, b * nq + i, 0)),
                  pl.BlockSpec((MLA_HEADS, tq, ROPE_DIM), lambda b, i: (0, b * nq + i, 0)),
                  pl.BlockSpec((seq, KV_LORA), lambda b, i: (b, 0)),
                  pl.BlockSpec((seq, ROPE_DIM), lambda b, i: (b, 0)),
                  pl.BlockSpec(w_uv_t.shape, lambda b, i: (0, 0, 0))],
        out_specs=pl.BlockSpec((tq, MLA_HEADS * V_DIM), lambda b, i: (b * nq + i, 0)),
        out_shape=jax.ShapeDtypeStruct((batch * seq, MLA_HEADS * V_DIM), BF16),
        scratch_shapes=[pltpu.VMEM((MLA_HEADS, tq, LANES), F32), pltpu.VMEM((MLA_HEADS, tq, LANES), F32),
                        pltpu.VMEM((MLA_HEADS, tq, KV_LORA), F32)],
        compiler_params=_cparams("arbitrary", "arbitrary"),
        name="attn_prompt",
    )(qa, qr, latb, krb, w_uv_t)


def _attn_decode_kernel(pt_ref, qa_ref, qr_ref, lat1_ref, kr1_ref, wuv_ref, *rest, pages):
    lat_refs = rest[:pages]
    kr_refs = rest[pages:2 * pages]
    o_ref, m_s, l_s, acc_s = rest[2 * pages:]
    j = pl.program_id(1)
    qa = qa_ref[0]
    qr = qr_ref[0]

    @pl.when(j == 0)
    def _():
        lat1 = lat1_ref[0].astype(F32)
        m_s[...] = (jnp.sum(qa.astype(F32) * lat1, axis=1, keepdims=True)
                    + jnp.sum(qr.astype(F32) * kr1_ref[0].astype(F32), axis=1, keepdims=True))
        l_s[...] = jnp.ones(l_s.shape, F32)
        acc_s[...] = jnp.broadcast_to(lat1, acc_s.shape)

    lats = [r[0].astype(BF16) for r in lat_refs]
    s = jnp.concatenate([_dot_nt(qa, lats[p]) + _dot_nt(qr, kr_refs[p][0]) for p in range(pages)], axis=1)
    m_old = m_s[...]
    m_new = jnp.maximum(m_old, jnp.max(s, axis=1, keepdims=True))
    alpha = jnp.exp(m_old - m_new)
    p_all = jnp.exp(s - m_new)
    l_s[...] = alpha * l_s[...] + jnp.sum(p_all, axis=1, keepdims=True)
    page = lats[0].shape[0]
    pv = _dot(p_all[:, :page], lats[0])
    for p in range(1, pages):
        pv = pv + _dot(p_all[:, p * page:(p + 1) * page], lats[p])
    acc_s[...] = alpha * acc_s[...] + pv
    m_s[...] = m_new

    @pl.when(j == pl.num_programs(1) - 1)
    def _():
        o_all = _dot(acc_s[...] / l_s[...], wuv_ref[...])
        for h in range(o_all.shape[0]):
            o_ref[0, :, h * V_DIM:(h + 1) * V_DIM] = o_all[h:h + 1, h * V_DIM:(h + 1) * V_DIM].astype(o_ref.dtype)


def _attn_decode(qa, qr, lat1, kr1, w_uv_flat, cache_latent, cache_krope, page_table, pages=16):
    batch, n_pages = page_table.shape
    pages = min(pages, n_pages)
    page = cache_latent.shape[1]
    steps = n_pages // pages
    one = lambda shape: pl.BlockSpec((1,) + shape, lambda b, j, pt: (b,) + (0,) * len(shape))
    paged = lambda w, p: pl.BlockSpec((1, page, w), lambda b, j, pt: (pt[b, j * pages + p], 0, 0))
    grid_spec = pltpu.PrefetchScalarGridSpec(
        num_scalar_prefetch=1,
        grid=(batch, steps),
        in_specs=[one((MLA_HEADS, KV_LORA)), one((MLA_HEADS, ROPE_DIM)), one((1, KV_LORA)), one((1, ROPE_DIM)),
                  pl.BlockSpec(w_uv_flat.shape, lambda b, j, pt: (0, 0))]
                 + [paged(KV_LORA, p) for p in range(pages)] + [paged(ROPE_DIM, p) for p in range(pages)],
        out_specs=one((1, MLA_HEADS * V_DIM)),
        scratch_shapes=[pltpu.VMEM((MLA_HEADS, 1), F32), pltpu.VMEM((MLA_HEADS, 1), F32),
                        pltpu.VMEM((MLA_HEADS, KV_LORA), F32)],
    )
    return pl.pallas_call(
        functools.partial(_attn_decode_kernel, pages=pages),
        grid_spec=grid_spec,
        out_shape=jax.ShapeDtypeStruct((batch, 1, MLA_HEADS * V_DIM), BF16),
        compiler_params=_cparams("arbitrary", "arbitrary"),
        name="attn_decode",
    )(page_table, qa, qr, lat1, kr1, w_uv_flat, *([cache_latent] * pages), *([cache_krope] * pages))


def _rope_tables(pos):
    half = ROPE_DIM // 2
    inv = ROPE_THETA ** (-jnp.arange(half, dtype=F32) / half)
    ang = pos.astype(F32)[:, None] * inv[None, :]
    cos, sin = jnp.cos(ang), jnp.sin(ang)
    return jnp.concatenate([cos, cos], axis=1), jnp.concatenate([-sin, sin], axis=1)


def _routing_tables(counts, n_rows):
    bm = EXPERT_ROWS
    padded = (counts + bm - 1) // bm * bm
    pad_end = jnp.cumsum(padded).astype(I32)
    pad_start = pad_end - padded
    block_start = jnp.arange(n_rows // bm, dtype=I32) * bm
    block_e = jnp.sum((pad_end[None, :] <= block_start[:, None]).astype(I32), axis=1)
    block_e = jnp.minimum(block_e, N_EXPERTS - 1)
    first = jnp.concatenate([jnp.ones((1,), I32), (block_e[1:] != block_e[:-1]).astype(I32)])
    n_used = pad_end[-1:] // bm
    return pad_start, pad_end, padded, block_e, first, n_used


def kernel(x_prompt, x_sample, state_mlstm_C, state_mlstm_n, state_mlstm_m, cache_latent, cache_krope, page_table, c_prompt, c_sample, w_ada, b_ada, g_mix, g_ffn, w_mlstm_in, b_mlstm_gates, g_mlstm_head, w_mlstm_out, w_ada_kv, b_ada_kv, g_kv_in, w_dkv, g_kv_lat, w_uk, w_uv, w_dq, g_q_lat, w_uq, w_mla_out, w_router, b_router, w_exp_gate, w_exp_up, w_exp_down, w_sh_gate, w_sh_up, w_sh_down, g_final):
    bp, tp, d = x_prompt.shape
    bs, ts, _ = x_sample.shape
    depth = w_ada.shape[0]
    n_a = w_mlstm_in.shape[0]
    past_len = page_table.shape[1] * cache_latent.shape[1]
    hk, hv = ML_HEADS * ML_DK, ML_HEADS * ML_DV
    n_p, n_s = bp * tp, bs * ts
    n_tot = n_p + n_s
    n_rows = (n_tot * TOP_K + N_EXPERTS * (EXPERT_ROWS - 1) + EXPERT_ROWS - 1) // EXPERT_ROWS * EXPERT_ROWS

    groups = [
        dict(batch=bp, seq=tp, tok=_Tok(bp, tp, 256), x=x_prompt.reshape(n_p, d), pos0=0,
             C0=jnp.zeros((n_a, bp, ML_HEADS, ML_DK, ML_DV), F32), n0=jnp.zeros((n_a, bp, ML_HEADS, ML_DK), F32),
             m0=jnp.full((n_a, bp, ML_HEADS), M_INIT, F32)),
        dict(batch=bs, seq=ts, tok=_Tok(bs, ts, 256), x=x_sample.reshape(n_s, d), pos0=past_len,
             C0=state_mlstm_C, n0=state_mlstm_n, m0=state_mlstm_m),
    ]
    c_all = jnp.concatenate([c_prompt, c_sample], axis=0)
    row0 = [0, bp]
    for g in groups:
        g["Cs"], g["ns"], g["ms"] = [], [], []

    for g in groups:
        g["cos2"], g["sin2"] = _rope_tables(g["pos0"] + jnp.arange(g["seq"], dtype=I32))
        if g["seq"] == 1:
            g["cos2"] = jnp.broadcast_to(g["cos2"], (g["batch"], ROPE_DIM))
            g["sin2"] = jnp.broadcast_to(g["sin2"], (g["batch"], ROPE_DIM))

    mod_kv = _ada(c_all, w_ada_kv, b_ada_kv)
    w_uk_t = jnp.transpose(w_uk, (1, 2, 0))
    w_uv_t = jnp.transpose(w_uv, (1, 0, 2)).astype(BF16)

    for l in range(depth):
        mod = _ada(c_all, w_ada, b_ada[l], layer=l)
        w_router_t = w_router[l].T
        ws_gu = jnp.concatenate([w_sh_gate[l], w_sh_up[l]], axis=1).astype(BF16)
        ws_d = w_sh_down[l].astype(BF16)
        cnt = jnp.zeros((N_EXPERTS, 128), F32)
        for gi, g in enumerate(groups):
            r0, nb = row0[gi], g["batch"]
            sh_m, sc_m, g_m, sh_f, sc_f, g_f = [mod[r0:r0 + nb, i * d:(i + 1) * d] for i in range(6)]
            g["g_f"] = g_f
            tok = g["tok"]
            wdt = F32 if g["seq"] == 1 else BF16
            if l < n_a:
                w_in = w_mlstm_in[l]
                w_main = w_in[:, :2 * hk + 2 * hv].astype(wdt)
                w_gates = jnp.pad(w_in[:, 2 * hk + 2 * hv:], ((0, 0), (0, 128 - 2 * ML_HEADS)))
                b_gates = jnp.pad(b_mlstm_gates[l], (0, 128 - 2 * ML_HEADS)).reshape(1, 128)
                if g["seq"] == 1:
                    proj, gates = _in0(tok, g["x"], sh_m, sc_m, g_mix[l], w_main, w_gates, b_gates, F32)
                    a, Cn, nn, mn = _mlstm_step(proj, gates, g["C0"][l], g["n0"][l], g["m0"][l],
                                                g_mlstm_head[l], nb)
                    a = a.reshape(nb, hv)
                else:
                    proj, gates = _in0(tok, g["x"], sh_m, sc_m, g_mix[l], w_main, w_gates, b_gates, BF16)
                    a, Cn, nn, mn = _mlstm_chunks(proj, gates, g["C0"][l], g["n0"][l], g["m0"][l],
                                                  g_mlstm_head[l], nb, g["seq"])
                g["Cs"].append(Cn)
                g["ns"].append(nn)
                g["ms"].append(mn.reshape(nb, ML_HEADS))
                w_o = w_mlstm_out[l].astype(wdt)
            else:
                j = l - n_a
                w_q = w_uq[j].reshape(-1, MLA_HEADS, NOPE_DIM + ROPE_DIM)
                w_q_nope = w_q[:, :, :NOPE_DIM].reshape(-1, MLA_HEADS * NOPE_DIM).astype(wdt)
                w_q_rope = w_q[:, :, NOPE_DIM:].reshape(-1, MLA_HEADS * ROPE_DIM).astype(wdt)
                sh_kv, sc_kv = mod_kv[r0:r0 + nb, :d], mod_kv[r0:r0 + nb, d:]
                lat, kr, latb, krb, qa, qr = _in1(tok, g["x"], sh_kv, sc_kv, g_kv_in, w_dkv.astype(wdt), g_kv_lat,
                                                  g["cos2"], g["sin2"], sh_m, sc_m, g_mix[l], w_dq[j].astype(wdt),
                                                  g_q_lat[j], w_q_nope, w_q_rope, w_uk_t.astype(wdt))
                if l == n_a:
                    g["lat"], g["kr"], g["latb"], g["krb"] = lat, kr, latb, krb
                if g["seq"] == 1:
                    a = _attn_decode(jnp.transpose(qa, (1, 0, 2)), jnp.transpose(qr, (1, 0, 2)),
                                     g["latb"].reshape(nb, 1, KV_LORA), g["krb"].reshape(nb, 1, ROPE_DIM),
                                     w_uv.reshape(KV_LORA, MLA_HEADS * V_DIM).astype(BF16),
                                     cache_latent, cache_krope, page_table).reshape(nb, MLA_HEADS * V_DIM)
                else:
                    a = _attn_prompt(qa, qr, g["latb"], g["krb"], w_uv_t, nb, g["seq"])
                w_o = w_mla_out[j].astype(wdt)
            g["xs"], g["hp"], g["idx"], g["wt"], g["rank"], cnt = _post(
                tok, g["x"], a, w_o, g_m, g_ffn[l], sh_f, sc_f, g_f, w_router_t.astype(wdt), b_router[l], ws_gu, ws_d,
                cnt)

        pad_start, pad_end, padded, block_e, first, n_used = _routing_tables(cnt[:, 0].astype(I32), n_rows)
        xs_sorted = None
        for g in groups:
            g["dest"] = _dest(pad_start, g["idx"], g["rank"])
            xs_sorted = _scatter(g["hp"], g["dest"], pad_end, padded, xs_sorted, n_rows, 256)
        out_sorted = _experts(xs_sorted, block_e, first, n_used, w_exp_gate, w_exp_up, w_exp_down, l)
        for g in groups:
            g["x"] = _combine((g["batch"], g["seq"]), g["xs"], g["wt"], g["g_f"], g_final, out_sorted, g["dest"],
                              l == depth - 1, 128)

    outs = []
    for g in groups:
        nb, seq = g["batch"], g["seq"]
        outs.append((g["x"].reshape(nb, seq, d), jnp.stack(g["Cs"]), jnp.stack(g["ns"]), jnp.stack(g["ms"]),
                     g["lat"].reshape(nb, seq, KV_LORA), g["kr"].reshape(nb, seq, ROPE_DIM)))
    p, s = outs
    return (p[0], s[0], p[1], p[2], p[3], p[4], p[5], s[1], s[2], s[3], s[4], s[5])
```

```python
import functools

import jax
import jax.numpy as jnp
from jax import lax
from jax.experimental import pallas as pl
from jax.experimental.pallas import tpu as pltpu

F32 = jnp.float32
BF16 = jnp.bfloat16
I32 = jnp.int32
HIGHEST = lax.Precision.HIGHEST
NEG_INF = float("-inf")

ML_HEADS = 4
ML_DK = 128
ML_DV = 256
ML_CHUNK = 128
M_INIT = -1e30
MLA_HEADS = 8
NOPE_DIM = 128
ROPE_DIM = 64
V_DIM = 128
KV_LORA = 256
ROPE_THETA = 10000.0
N_EXPERTS = 64
TOP_K = 8
N_GROUPS = 8
TOPK_GROUPS = 4
ROUTED_SCALE = 2.5
NORM_EPS = 1e-6

LANES = 128
EXPERT_ROWS = 512
VMEM_LIMIT = 56 * 1024 * 1024


def _cparams(*sem):
    return pltpu.CompilerParams(dimension_semantics=sem, vmem_limit_bytes=VMEM_LIMIT)


def _dot(a, b):
    return jnp.dot(a.astype(BF16), b.astype(BF16), preferred_element_type=F32)


def _dot_nt(a, b):
    return lax.dot_general(a.astype(BF16), b.astype(BF16), (((1,), (1,)), ((), ())),
                           preferred_element_type=F32)


def _dot_f32(a, b):
    return jnp.dot(a, b, precision=HIGHEST, preferred_element_type=F32)


def _dot_w(a, w):
    return _dot_f32(a.astype(F32), w) if w.dtype == F32 else _dot(a, w)


def _dot_w_nt(w, a):
    if w.dtype == F32:
        return lax.dot_general(w, a.astype(F32), (((1,), (1,)), ((), ())), precision=HIGHEST,
                               preferred_element_type=F32)
    return _dot_nt(w, a)


def _rms(x, g):
    return x * lax.rsqrt(jnp.mean(x * x, axis=-1, keepdims=True) + NORM_EPS) * g


def _rms_mod(x, g, shift, scale):
    return _rms(x, g) * (1.0 + scale) + shift


def _sigmoid(x):
    return 1.0 / (1.0 + jnp.exp(-x))


def _silu(x):
    return x * _sigmoid(x)


def _log_sigmoid(x):
    return jnp.minimum(x, 0.0) - jnp.log1p(jnp.exp(-jnp.abs(x)))


def _ada_kernel(c_ref, w_ref, b_ref, o_ref):
    o_ref[...] = _dot_f32(_silu(c_ref[...]), w_ref[...]) + b_ref[...]


def _ada(c, w, b, layer=None, tn=1024):
    m, d = c.shape
    n_out = w.shape[-1]
    if layer is None:
        w_spec = pl.BlockSpec((d, tn), lambda j: (0, j))
    else:
        w_spec = pl.BlockSpec((None, d, tn), lambda j: (layer, 0, j))
    return pl.pallas_call(
        _ada_kernel,
        grid=(n_out // tn,),
        in_specs=[pl.BlockSpec((m, d), lambda j: (0, 0)), w_spec, pl.BlockSpec((1, tn), lambda j: (0, j))],
        out_specs=pl.BlockSpec((m, tn), lambda j: (0, j)),
        out_shape=jax.ShapeDtypeStruct((m, n_out), F32),
        compiler_params=_cparams("arbitrary"),
        name="ada",
    )(c, w, b.reshape(1, n_out))


class _Tok:
    def __init__(self, batch, seq, tile):
        if seq == 1:
            self.tm, self.n_tiles, self.per_seq = batch, 1, 1
            self.mod_block = (1, batch, None)
        else:
            self.tm = min(tile, seq)
            self.per_seq = seq // self.tm
            self.n_tiles = batch * self.per_seq
            self.mod_block = (1, 1, None)
        self.batch, self.seq = batch, seq

    def mod(self, v):
        d = v.shape[-1]
        if self.seq == 1:
            return v.reshape(1, self.batch, d), pl.BlockSpec((1, self.batch, d), lambda i, *_: (0, 0, 0))
        per = self.per_seq
        return v.reshape(self.batch, 1, d), pl.BlockSpec((1, 1, d), lambda i, *_: (i // per, 0, 0))

    def rows(self, width):
        return pl.BlockSpec((self.tm, width), lambda i, *_: (i, 0))

    def seq_rows(self, width):
        per = self.per_seq
        return pl.BlockSpec((self.tm, width), lambda i, *_: (i % per, 0))


def _full(shape):
    nd = len(shape)
    return pl.BlockSpec(shape, lambda i, *_: (0,) * nd)


def _in0_kernel(x_ref, sh_ref, sc_ref, g_ref, w_ref, wg_ref, bg_ref, o_ref, gt_ref):
    h = _rms_mod(x_ref[...], g_ref[...], sh_ref[0], sc_ref[0])
    o_ref[...] = _dot_w(h, w_ref[...]).astype(o_ref.dtype)
    gt_ref[...] = _dot_f32(h, wg_ref[...]) + bg_ref[...]


def _in0(tok, x, shift, scale, g, w, w_gates, b_gates, out_dtype):
    n, d = x.shape
    wo = w.shape[1]
    sh, sh_spec = tok.mod(shift)
    sc, sc_spec = tok.mod(scale)
    return pl.pallas_call(
        _in0_kernel,
        grid=(tok.n_tiles,),
        in_specs=[tok.rows(d), sh_spec, sc_spec, _full((1, d)), _full((d, wo)), _full((d, 128)), _full((1, 128))],
        out_specs=[tok.rows(wo), tok.rows(128)],
        out_shape=[jax.ShapeDtypeStruct((n, wo), out_dtype), jax.ShapeDtypeStruct((n, 128), F32)],
        compiler_params=_cparams("arbitrary"),
        name="in0",
    )(x, sh, sc, g.reshape(1, d), w, w_gates, b_gates)


def _mlstm_chunk_kernel(q_ref, k_ref, v_ref, og_ref, gt_ref, c0_ref, n0_ref, m0_ref, gh_ref,
                        hh_ref, c_ref, n_ref, m_ref):
    L = q_ref.shape[0]
    scale = ML_DK ** -0.5

    @pl.when(pl.program_id(1) == 0)
    def _():
        c_ref[...] = c0_ref[...]
        n_ref[...] = n0_ref[...]
        m_ref[...] = m0_ref[...]

    gates = gt_ref[...]
    logf = _log_sigmoid(gates)
    row = lax.broadcasted_iota(I32, (L, L), 0)
    col = lax.broadcasted_iota(I32, (L, L), 1)
    causal = col <= row
    tril = causal.astype(F32)
    b_cols = _dot_f32(tril, logf)
    gates_t = gates.T
    b_rows = _dot_f32(logf.T, (row <= col).astype(F32))

    for h in range(ML_HEADS):
        i_row = gates_t[h:h + 1, :]
        i_col = gates[:, h:h + 1]
        b_row = b_rows[ML_HEADS + h:ML_HEADS + h + 1, :]
        b_col = b_cols[:, ML_HEADS + h:ML_HEADS + h + 1]
        m_prev = m_ref[0, :, h:h + 1]
        c_prev = c_ref[0, h]
        n_prev = n_ref[0, h:h + 1, :]
        q = q_ref[:, h * ML_DK:(h + 1) * ML_DK]
        k = k_ref[:, h * ML_DK:(h + 1) * ML_DK]
        v = v_ref[:, h * ML_DV:(h + 1) * ML_DV]

        g_col = b_col + m_prev
        dmat = jnp.where(causal, b_col - b_row + i_row, NEG_INF)
        m_t = jnp.maximum(g_col, jnp.max(dmat, axis=1, keepdims=True))
        w_intra = jnp.exp(dmat - m_t)
        w_inter = jnp.exp(g_col - m_t)
        s = _dot_nt(q, k) * (w_intra * scale)
        num = _dot(s, v) + w_inter * _dot(q, c_prev)
        qn = jnp.sum(q.astype(F32) * n_prev, axis=1, keepdims=True)
        den = jnp.sum(s, axis=1, keepdims=True) + w_inter * qn
        cell = num / jnp.maximum(jnp.abs(den), jnp.exp(-m_t))

        b_last = b_col[L - 1:L, :]
        m_new = m_t[L - 1:L, :]
        a_inter = jnp.exp(b_last + m_prev - m_new)
        a_intra = jnp.exp(b_last - b_col + i_col - m_new) * scale
        kw = k.astype(F32) * a_intra
        c_ref[0, h] = a_inter * c_prev + _dot(kw.T, v)
        n_ref[0, h:h + 1, :] = a_inter * n_prev + jnp.sum(kw, axis=0, keepdims=True)
        m_ref[0, :, h:h + 1] = m_new

        y = _rms(cell, gh_ref[:, h * ML_DV:(h + 1) * ML_DV])
        og = og_ref[:, h * ML_DV:(h + 1) * ML_DV].astype(F32)
        hh_ref[:, h * ML_DV:(h + 1) * ML_DV] = (y * _sigmoid(og)).astype(hh_ref.dtype)


def _mlstm_chunks(proj, gates, c0, n0, m0, g_head, batch, seq):
    L = seq if seq <= ML_CHUNK else ML_CHUNK
    nc = seq // L
    hk, hv = ML_HEADS * ML_DK, ML_HEADS * ML_DV
    rows = lambda w, cb: pl.BlockSpec((L, w), lambda b, j: (b * nc + j, cb))
    state = lambda shape: pl.BlockSpec((1,) + shape, lambda b, j: (b,) + (0,) * len(shape))
    return pl.pallas_call(
        _mlstm_chunk_kernel,
        grid=(batch, nc),
        in_specs=[rows(hk, 0), rows(hk, 1), rows(hv, 2 * hk // hv), rows(hv, 2 * hk // hv + 1), rows(128, 0),
                  state((ML_HEADS, ML_DK, ML_DV)), state((ML_HEADS, ML_DK)), state((1, ML_HEADS)),
                  pl.BlockSpec((1, hv), lambda b, j: (0, 0))],
        out_specs=[rows(hv, 0), state((ML_HEADS, ML_DK, ML_DV)), state((ML_HEADS, ML_DK)), state((1, ML_HEADS))],
        out_shape=[jax.ShapeDtypeStruct((batch * seq, hv), BF16),
                   jax.ShapeDtypeStruct((batch, ML_HEADS, ML_DK, ML_DV), F32),
                   jax.ShapeDtypeStruct((batch, ML_HEADS, ML_DK), F32),
                   jax.ShapeDtypeStruct((batch, 1, ML_HEADS), F32)],
        compiler_params=_cparams("arbitrary", "arbitrary"),
        name="mlstm_chunks",
    )(proj, proj, proj, proj, gates, c0, n0, m0.reshape(batch, 1, ML_HEADS), g_head.reshape(1, hv))


def _row_to_col(r):
    n = r.shape[1]
    eye = lax.broadcasted_iota(I32, (n, n), 0) == lax.broadcasted_iota(I32, (n, n), 1)
    return jnp.sum(jnp.where(eye, jnp.broadcast_to(r, (n, n)), 0.0), axis=1, keepdims=True)


def _mlstm_step_kernel(p_ref, gt_ref, c0_ref, n0_ref, m0_ref, gh_ref, hh_ref, c_ref, n_ref, m_ref):
    scale = ML_DK ** -0.5
    hk, hv = ML_HEADS * ML_DK, ML_HEADS * ML_DV
    gates = gt_ref[0]
    logf = _log_sigmoid(gates)
    for h in range(ML_HEADS):
        q = p_ref[0, :, h * ML_DK:(h + 1) * ML_DK].astype(F32)
        k = p_ref[0, :, hk + h * ML_DK:hk + (h + 1) * ML_DK].astype(F32)
        v = p_ref[0, :, 2 * hk + h * ML_DV:2 * hk + (h + 1) * ML_DV].astype(F32)
        og = p_ref[0, :, 2 * hk + hv + h * ML_DV:2 * hk + hv + (h + 1) * ML_DV].astype(F32)
        log_i = gates[:, h:h + 1]
        m_prev = m0_ref[0, :, h:h + 1]
        c_prev = c0_ref[0, h]
        n_prev = n0_ref[0, h:h + 1, :]

        g = logf[:, ML_HEADS + h:ML_HEADS + h + 1] + m_prev
        m_t = jnp.maximum(g, log_i)
        w_intra = jnp.exp(log_i - m_t)
        w_inter = jnp.exp(g - m_t)
        s = jnp.sum(q * k, axis=1, keepdims=True) * (w_intra * scale)
        q_c = _row_to_col(q)
        num = s * v + w_inter * jnp.sum(q_c * c_prev, axis=0, keepdims=True)
        den = s + w_inter * jnp.sum(q * n_prev, axis=1, keepdims=True)
        cell = num / jnp.maximum(jnp.abs(den), jnp.exp(-m_t))

        kw = k * (w_intra * scale)
        c_ref[0, h] = w_inter * c_prev + _row_to_col(kw) * v
        n_ref[0, h:h + 1, :] = w_inter * n_prev + kw
        m_ref[0, :, h:h + 1] = m_t

        y = _rms(cell, gh_ref[:, h * ML_DV:(h + 1) * ML_DV])
        hh_ref[0, :, h * ML_DV:(h + 1) * ML_DV] = (y * _sigmoid(og)).astype(hh_ref.dtype)


def _mlstm_step(proj, gates, c0, n0, m0, g_head, batch):
    hv = ML_HEADS * ML_DV
    wp = proj.shape[1]
    one = lambda shape: pl.BlockSpec((1,) + shape, lambda b: (b,) + (0,) * len(shape))
    return pl.pallas_call(
        _mlstm_step_kernel,
        grid=(batch,),
        in_specs=[one((1, wp)), one((1, 128)), one((ML_HEADS, ML_DK, ML_DV)), one((ML_HEADS, ML_DK)),
                  one((1, ML_HEADS)), pl.BlockSpec((1, hv), lambda b: (0, 0))],
        out_specs=[one((1, hv)), one((ML_HEADS, ML_DK, ML_DV)), one((ML_HEADS, ML_DK)), one((1, ML_HEADS))],
        out_shape=[jax.ShapeDtypeStruct((batch, 1, hv), F32),
                   jax.ShapeDtypeStruct((batch, ML_HEADS, ML_DK, ML_DV), F32),
                   jax.ShapeDtypeStruct((batch, ML_HEADS, ML_DK), F32),
                   jax.ShapeDtypeStruct((batch, 1, ML_HEADS), F32)],
        compiler_params=_cparams("arbitrary"),
        name="mlstm_step",
    )(proj.reshape(batch, 1, wp), gates.reshape(batch, 1, 128), c0, n0, m0.reshape(batch, 1, ML_HEADS),
      g_head.reshape(1, hv))


def _group_reduce(x, op):
    e, tm = x.shape
    per = e // N_GROUPS
    x3 = x.reshape(N_GROUPS, per, tm)
    r = op(x3, axis=1, keepdims=True)
    return jnp.broadcast_to(r, (N_GROUPS, per, tm)).reshape(e, tm)


def _route(logits_t, bias_t):
    e, tm = logits_t.shape
    per = e // N_GROUPS
    scores = _sigmoid(logits_t)
    sel = scores + bias_t
    eidx = lax.broadcasted_iota(I32, (e, tm), 0)
    jidx = eidx % per
    gidx = eidx // per
    m1 = _group_reduce(sel, jnp.max)
    first1 = _group_reduce(jnp.where(sel == m1, jidx, per), jnp.min)
    m2 = _group_reduce(jnp.where(jidx == first1, NEG_INF, sel), jnp.max)
    gs = m1 + m2
    chosen = jnp.zeros((e, tm), F32)
    for _ in range(TOPK_GROUPS):
        mx = jnp.max(gs, axis=0, keepdims=True)
        f = jnp.min(jnp.where(gs == mx, gidx, N_GROUPS), axis=0, keepdims=True)
        hit = gidx == f
        chosen = jnp.where(hit, 1.0, chosen)
        gs = jnp.where(hit, NEG_INF, gs)
    cand = jnp.where(chosen > 0.0, sel, NEG_INF)
    picked = jnp.zeros((e, tm), F32)
    idx, wts = [], []
    for _ in range(TOP_K):
        mx = jnp.max(cand, axis=0, keepdims=True)
        f = jnp.min(jnp.where(cand == mx, eidx, e), axis=0, keepdims=True)
        hit = eidx == f
        idx.append(f)
        wts.append(jnp.sum(jnp.where(hit, scores, 0.0), axis=0, keepdims=True))
        cand = jnp.where(hit, NEG_INF, cand)
        picked = jnp.where(hit, 1.0, picked)
    return idx, wts, picked


def _post_kernel(x_ref, a_ref, wo_ref, gm_ref, gn_ref, sh_ref, sc_ref, gf_ref, wr_ref, br_ref,
                 wsgu_ref, wsd_ref,
                 xs_ref, hp_ref, idx_ref, wt_ref, rank_ref, cnt_ref):
    tm, d = x_ref.shape

    x1 = x_ref[...] + gm_ref[0] * _dot_w(a_ref[...], wo_ref[...])
    h2 = _rms_mod(x1, gn_ref[...], sh_ref[0], sc_ref[0])
    hp_ref[...] = h2.astype(BF16)

    ff = wsd_ref.shape[0]
    gu = _dot(h2, wsgu_ref[...])
    shared = _dot(_silu(gu[:, :ff]) * gu[:, ff:], wsd_ref[...])
    xs_ref[...] = x1 + gf_ref[0] * shared

    logits_t = _dot_w_nt(wr_ref[...], h2)
    idx, wts, picked = _route(logits_t, br_ref[:, 0:1])
    e = picked.shape[0]
    eidx = lax.broadcasted_iota(I32, (e, tm), 0)
    before = (lax.broadcasted_iota(I32, (tm, tm), 0) < lax.broadcasted_iota(I32, (tm, tm), 1)).astype(BF16)
    rank_t = jnp.dot(picked.astype(BF16), before, preferred_element_type=F32)
    total = wts[0]
    for k in range(1, TOP_K):
        total = total + wts[k]
    for k in range(TOP_K):
        idx_ref[k:k + 1, :] = idx[k]
        wt_ref[k:k + 1, :] = wts[k] / total * ROUTED_SCALE
        rank_ref[k:k + 1, :] = jnp.sum(jnp.where(eidx == idx[k], rank_t, 0.0), axis=0, keepdims=True).astype(I32)
    cnt_ref[0] = jnp.broadcast_to(jnp.sum(picked, axis=1, keepdims=True), cnt_ref.shape[1:])


def _post(tok, x, a, w_o, g_m, g_norm, shift, scale, g_f, w_router_t, b_router, ws_gu, ws_d):
    n, d = x.shape
    e = w_router_t.shape[0]
    ff = ws_d.shape[0]
    gm, gm_spec = tok.mod(g_m)
    sh, sh_spec = tok.mod(shift)
    sc, sc_spec = tok.mod(scale)
    gf, gf_spec = tok.mod(g_f)
    cols = lambda rows: pl.BlockSpec((rows, tok.tm), lambda i: (0, i))
    return pl.pallas_call(
        _post_kernel,
        grid=(tok.n_tiles,),
        in_specs=[tok.rows(d), tok.rows(a.shape[1]), _full(w_o.shape), gm_spec, _full((1, d)), sh_spec, sc_spec,
                  gf_spec, _full((e, d)), _full((e, 128)), _full((d, 2 * ff)), _full((ff, d))],
        out_specs=[tok.rows(d), tok.rows(d), cols(TOP_K), cols(TOP_K), cols(TOP_K),
                   pl.BlockSpec((1, e, LANES), lambda i: (i, 0, 0))],
        out_shape=[jax.ShapeDtypeStruct((n, d), F32), jax.ShapeDtypeStruct((n, d), BF16),
                   jax.ShapeDtypeStruct((TOP_K, n), I32), jax.ShapeDtypeStruct((TOP_K, n), F32),
                   jax.ShapeDtypeStruct((TOP_K, n), I32), jax.ShapeDtypeStruct((tok.n_tiles, e, LANES), F32)],
        compiler_params=_cparams("arbitrary"),
        name="post",
    )(x, a, w_o, gm, g_norm.reshape(1, d), sh, sc, gf, w_router_t,
      jnp.broadcast_to(b_router.reshape(e, 1), (e, 128)), ws_gu, ws_d)


MOE_TILE = 256
RUN_ALIGN = 16
RUN_PIECES = tuple(1 << b for b in range(MOE_TILE.bit_length() - 1, RUN_ALIGN.bit_length() - 2, -1))
POS_CHUNK = 512
TILE_ROWS = -(-(MOE_TILE * TOP_K + N_EXPERTS * (RUN_ALIGN - 1)) // POS_CHUNK) * POS_CHUNK


def _tile_positions(idx_ref, rank_ref, tsrc_ref, tile):
    idx = idx_ref[...]

    def add(e, acc):
        return acc + jnp.where(idx == e, tsrc_ref[tile, e], 0)

    return lax.fori_loop(0, N_EXPERTS, add, rank_ref[...])


def _start_runs(tile, tsrc_ref, tdst_ref, tcnt_ref, make_copy):
    def body(e, c):
        n, s, d = tcnt_ref[tile, e], tsrc_ref[tile, e], tdst_ref[tile, e]
        for piece in RUN_PIECES:
            done = n - (n & (2 * piece - 1))

            @pl.when((n & piece) != 0)
            def _():
                make_copy(pl.multiple_of(s + done, RUN_ALIGN), pl.multiple_of(d + done, RUN_ALIGN), piece).start()
        return c

    lax.fori_loop(0, N_EXPERTS, body, 0)


def _wait_runs(tile, tsrc_ref, tcnt_ref, make_copy):
    total = tsrc_ref[tile, N_EXPERTS - 1] + tcnt_ref[tile, N_EXPERTS - 1]
    piece = 1 << (TILE_ROWS.bit_length() - 1)
    while piece >= RUN_ALIGN:
        @pl.when((total & piece) != 0)
        def _():
            make_copy(0, 0, piece).wait()
        piece //= 2


def _dispatch_kernel(pe_ref, pd_ref, tsrc_ref, tdst_ref, tcnt_ref, idx_ref, rank_ref, hpp_ref, hps_ref,
                     xs_ref, srt, zeros, sem):
    i = pl.program_id(0)
    tm = hpp_ref.shape[0]
    bm = zeros.shape[0]

    def last_block(e):
        return pltpu.make_async_copy(zeros, xs_ref.at[pl.ds(pl.multiple_of(pe_ref[e] - bm, bm), bm)], sem)

    @pl.when(i == 0)
    def _():
        zeros[...] = jnp.zeros_like(zeros)

        def zstart(e, c):
            @pl.when(pd_ref[e] > 0)
            def _():
                last_block(e).start()
            return c

        def zwait(e, c):
            @pl.when(pd_ref[e] > 0)
            def _():
                last_block(e).wait()
            return c

        lax.fori_loop(0, N_EXPERTS, zstart, 0)
        lax.fori_loop(0, N_EXPERTS, zwait, 0)

    pos = _tile_positions(idx_ref, rank_ref, tsrc_ref, i)
    x = jnp.where(i == pl.num_programs(0) - 1, hps_ref[...], hpp_ref[...])
    for c in range(TILE_ROWS // POS_CHUNK):
        p_iota = c * POS_CHUNK + lax.broadcasted_iota(I32, (POS_CHUNK, tm), 0)
        hit = pos[0:1, :] == p_iota
        for k in range(1, TOP_K):
            hit = jnp.logical_or(hit, pos[k:k + 1, :] == p_iota)
        onehot = jnp.where(hit, 1.0, 0.0).astype(BF16)
        srt[c * POS_CHUNK:(c + 1) * POS_CHUNK, :] = jnp.dot(onehot, x, preferred_element_type=F32).astype(BF16)

    def copy(s, d, rows):
        return pltpu.make_async_copy(srt.at[pl.ds(s, rows)], xs_ref.at[pl.ds(d, rows)], sem)

    _start_runs(i, tsrc_ref, tdst_ref, tcnt_ref, copy)
    _wait_runs(i, tsrc_ref, tcnt_ref, copy)


def _dispatch(hp_p, hp_s, idx_all, rank_all, pad_end, padded, tsrc, tdst, tcnt, n_rows):
    tm = MOE_TILE
    d = hp_p.shape[1]
    n_tiles = tsrc.shape[0]
    cols = pl.BlockSpec((TOP_K, tm), lambda i, *_: (0, i))
    return pl.pallas_call(
        _dispatch_kernel,
        grid_spec=pltpu.PrefetchScalarGridSpec(
            num_scalar_prefetch=5, grid=(n_tiles,),
            in_specs=[cols, cols,
                      pl.BlockSpec((tm, d), lambda i, *_: (jnp.minimum(i, n_tiles - 2), 0)),
                      pl.BlockSpec((tm, d), lambda i, *_: (0, 0))],
            out_specs=pl.BlockSpec(memory_space=pl.ANY),
            scratch_shapes=[pltpu.VMEM((TILE_ROWS, d), BF16), pltpu.VMEM((EXPERT_ROWS, d), BF16),
                            pltpu.SemaphoreType.DMA]),
        out_shape=jax.ShapeDtypeStruct((n_rows, d), BF16),
        compiler_params=_cparams("arbitrary"),
        name="dispatch",
    )(pad_end, padded, tsrc, tdst, tcnt, idx_all, rank_all, hp_p, hp_s)


def _expert_kernel(be_ref, first_ref, nu_ref, x_ref, wg_ref, wu_ref, wd_ref, o_ref, wgu_s, wd_s):
    i = pl.program_id(0)
    ff = wd_s.shape[0]

    @pl.when(i < nu_ref[0])
    def _():
        @pl.when(first_ref[i] == 1)
        def _():
            wgu_s[:, :ff] = wg_ref[...].astype(BF16)
            wgu_s[:, ff:] = wu_ref[...].astype(BF16)
            wd_s[...] = wd_ref[...].astype(BF16)

        gu = jnp.dot(x_ref[...], wgu_s[...], preferred_element_type=F32)
        act = _silu(gu[:, :ff]) * gu[:, ff:]
        o_ref[...] = jnp.dot(act.astype(BF16), wd_s[...], preferred_element_type=F32).astype(o_ref.dtype)

    @pl.when(i >= nu_ref[0])
    def _():
        o_ref[...] = jnp.zeros_like(o_ref)


def _experts(xs, block_e, first, n_used, w_gate, w_up, w_down, layer):
    n_rows, d = xs.shape
    ff = w_gate.shape[-1]
    bm = EXPERT_ROWS
    grid_spec = pltpu.PrefetchScalarGridSpec(
        num_scalar_prefetch=3,
        grid=(n_rows // bm,),
        in_specs=[pl.BlockSpec((bm, d), lambda i, be, fi, nu: (jnp.minimum(i, nu[0] - 1), 0)),
                  pl.BlockSpec((None, None, d, ff), lambda i, be, fi, nu: (layer, be[i], 0, 0)),
                  pl.BlockSpec((None, None, d, ff), lambda i, be, fi, nu: (layer, be[i], 0, 0)),
                  pl.BlockSpec((None, None, ff, d), lambda i, be, fi, nu: (layer, be[i], 0, 0))],
        out_specs=pl.BlockSpec((bm, d), lambda i, be, fi, nu: (i, 0)),
        scratch_shapes=[pltpu.VMEM((d, 2 * ff), BF16), pltpu.VMEM((ff, d), BF16)],
    )
    return pl.pallas_call(
        _expert_kernel,
        grid_spec=grid_spec,
        out_shape=jax.ShapeDtypeStruct((n_rows, d), BF16),
        compiler_params=_cparams("arbitrary"),
        name="experts",
    )(block_e, first, n_used, xs, w_gate, w_up, w_down)


def _combine_kernel(tsrc_ref, tdst_ref, tcnt_ref, idx_ref, rank_ref, w_ref, xs_ref, gf_ref, gfin_ref, os_ref,
                    y_ref, buf, sem, *, tile0, clear, final_norm):
    tile = tile0 + pl.program_id(0)
    tm, d = xs_ref.shape
    first_free = 0 if clear else tm * TOP_K
    buf[first_free:, :] = jnp.zeros((buf.shape[0] - first_free, d), buf.dtype)

    def copy(s, d, rows):
        return pltpu.make_async_copy(os_ref.at[pl.ds(d, rows)], buf.at[pl.ds(s, rows)], sem)

    _start_runs(tile, tsrc_ref, tdst_ref, tcnt_ref, copy)
    pos = _tile_positions(idx_ref, rank_ref, tsrc_ref, tile).astype(F32)
    w = w_ref[...]
    pos_c = [_row_to_col(pos[k:k + 1, :]) for k in range(TOP_K)]
    w_c = [_row_to_col(w[k:k + 1, :]) for k in range(TOP_K)]
    _wait_runs(tile, tsrc_ref, tcnt_ref, copy)

    acc = jnp.zeros((tm, d), F32)
    for c in range(TILE_ROWS // POS_CHUNK):
        p_iota = (c * POS_CHUNK + lax.broadcasted_iota(I32, (tm, POS_CHUNK), 1)).astype(F32)
        wmat = jnp.where(pos_c[0] == p_iota, w_c[0], 0.0)
        for k in range(1, TOP_K):
            wmat = wmat + jnp.where(pos_c[k] == p_iota, w_c[k], 0.0)
        acc = acc + _dot(wmat, buf[c * POS_CHUNK:(c + 1) * POS_CHUNK, :])
    x2 = xs_ref[...] + gf_ref[0] * acc
    y_ref[...] = _rms(x2, gfin_ref[...]) if final_norm else x2


def _combine(tok, tile0, xs, g_f, g_final, idx_all, rank_all, wt_all, out_sorted, tsrc, tdst, tcnt, final_norm,
             clear):
    n, d = xs.shape
    gf, gf_spec = tok.mod(g_f)
    tm = MOE_TILE
    cols = pl.BlockSpec((TOP_K, tm), lambda i, *_: (0, tile0 + i))
    return pl.pallas_call(
        functools.partial(_combine_kernel, tile0=tile0, clear=clear, final_norm=final_norm),
        grid_spec=pltpu.PrefetchScalarGridSpec(
            num_scalar_prefetch=3, grid=(tok.n_tiles,),
            in_specs=[cols, cols, cols, tok.rows(d), gf_spec, _full((1, d)), pl.BlockSpec(memory_space=pl.ANY)],
            out_specs=tok.rows(d),
            scratch_shapes=[pltpu.VMEM((TILE_ROWS, d), BF16), pltpu.SemaphoreType.DMA]),
        out_shape=jax.ShapeDtypeStruct((n, d), F32),
        compiler_params=_cparams("arbitrary"),
        name="combine",
    )(tsrc, tdst, tcnt, idx_all, rank_all, wt_all, xs, gf, g_final.reshape(1, d), out_sorted)


def _rope(x, cos2, sin2):
    half = x.shape[1] // 2
    swapped = jnp.concatenate([x[:, half:], x[:, :half]], axis=1)
    return x * cos2 + swapped * sin2


def _in1_kernel(x_ref, shk_ref, sck_ref, gk_ref, wdkv_ref, glat_ref, cos_ref, sin_ref,
                shm_ref, scm_ref, gm_ref, wdq_ref, gq_ref, wqn_ref, wqr_ref, wuk_ref,
                lat_ref, kr_ref, latb_ref, krb_ref, qa_ref, qr_ref):
    x = x_ref[...]
    cos2, sin2 = cos_ref[...], sin_ref[...]
    att_scale = (NOPE_DIM + ROPE_DIM) ** -0.5

    hk = _rms_mod(x, gk_ref[...], shk_ref[0], sck_ref[0])
    ckr = _dot_w(hk, wdkv_ref[...])
    lat = _rms(ckr[:, :KV_LORA], glat_ref[...])
    kr = _rope(ckr[:, KV_LORA:], cos2, sin2)
    lat_ref[...] = lat
    kr_ref[...] = kr
    latb_ref[...] = lat.astype(BF16)
    krb_ref[...] = kr.astype(BF16)

    hm = _rms_mod(x, gm_ref[...], shm_ref[0], scm_ref[0])
    q_lat = _rms(_dot_w(hm, wdq_ref[...]), gq_ref[...])
    q_nope = _dot_w(q_lat, wqn_ref[...])
    q_rope = _dot_w(q_lat, wqr_ref[...])
    for h in range(MLA_HEADS):
        qa = _dot_w(q_nope[:, h * NOPE_DIM:(h + 1) * NOPE_DIM], wuk_ref[h])
        qa_ref[h] = (qa * att_scale).astype(BF16)
        qr = _rope(q_rope[:, h * ROPE_DIM:(h + 1) * ROPE_DIM], cos2, sin2)
        qr_ref[h] = (qr * att_scale).astype(BF16)


def _in1(tok, x, sh_kv, sc_kv, g_kv_in, w_dkv, g_kv_lat, cos2, sin2, sh_m, sc_m, g_mix, w_dq, g_q_lat,
         w_q_nope, w_q_rope, w_uk_t):
    n, d = x.shape
    q_lora = w_dq.shape[1]
    shk, shk_spec = tok.mod(sh_kv)
    sck, sck_spec = tok.mod(sc_kv)
    shm, shm_spec = tok.mod(sh_m)
    scm, scm_spec = tok.mod(sc_m)
    heads = lambda w: pl.BlockSpec((MLA_HEADS, tok.tm, w), lambda i: (0, i, 0))
    return pl.pallas_call(
        _in1_kernel,
        grid=(tok.n_tiles,),
        in_specs=[tok.rows(d), shk_spec, sck_spec, _full((1, d)), _full(w_dkv.shape), _full((1, KV_LORA)),
                  tok.seq_rows(ROPE_DIM), tok.seq_rows(ROPE_DIM),
                  shm_spec, scm_spec, _full((1, d)), _full(w_dq.shape), _full((1, q_lora)),
                  _full(w_q_nope.shape), _full(w_q_rope.shape), _full(w_uk_t.shape)],
        out_specs=[tok.rows(KV_LORA), tok.rows(ROPE_DIM), tok.rows(KV_LORA), tok.rows(ROPE_DIM),
                   heads(KV_LORA), heads(ROPE_DIM)],
        out_shape=[jax.ShapeDtypeStruct((n, KV_LORA), F32), jax.ShapeDtypeStruct((n, ROPE_DIM), F32),
                   jax.ShapeDtypeStruct((n, KV_LORA), BF16), jax.ShapeDtypeStruct((n, ROPE_DIM), BF16),
                   jax.ShapeDtypeStruct((MLA_HEADS, n, KV_LORA), BF16),
                   jax.ShapeDtypeStruct((MLA_HEADS, n, ROPE_DIM), BF16)],
        compiler_params=_cparams("arbitrary"),
        name="in1",
    )(x, shk, sck, g_kv_in.reshape(1, d), w_dkv, g_kv_lat.reshape(1, KV_LORA), cos2, sin2,
      shm, scm, g_mix.reshape(1, d), w_dq, g_q_lat.reshape(1, q_lora), w_q_nope, w_q_rope, w_uk_t)


def _attn_prompt_kernel(qa_ref, qr_ref, lat_ref, kr_ref, wuv_ref, o_ref, m_s, l_s, acc_s):
    heads, tq, c = qa_ref.shape
    qi = pl.program_id(1)
    n_lt = tq // LANES
    m_s[...] = jnp.full(m_s.shape, NEG_INF, F32)
    l_s[...] = jnp.zeros(l_s.shape, F32)
    acc_s[...] = jnp.zeros(acc_s.shape, F32)
    causal = lax.broadcasted_iota(I32, (tq, tq), 1) <= lax.broadcasted_iota(I32, (tq, tq), 0)

    def chunk(j, diagonal):
        start = pl.multiple_of(j * tq, tq)
        kc = lat_ref[pl.ds(start, tq), :]
        krc = kr_ref[pl.ds(start, tq), :]
        def scores(h):
            return _dot_nt(qa_ref[h], kc) + _dot_nt(qr_ref[h], krc)

        s_next = scores(0)
        for h in range(heads):
            s, s_next = s_next, (scores(h + 1) if h + 1 < heads else None)
            if diagonal:
                s = jnp.where(causal, s, NEG_INF)
            tiles = [s[:, t * LANES:(t + 1) * LANES] for t in range(n_lt)]
            mx = tiles[0]
            for t in tiles[1:]:
                mx = jnp.maximum(mx, t)
            m_old = m_s[h]
            m_new = jnp.maximum(m_old, jnp.broadcast_to(jnp.max(mx, axis=1, keepdims=True), (tq, LANES)))
            alpha = jnp.exp(m_old - m_new)
            ps = [jnp.exp(t - m_new) for t in tiles]
            part = ps[0]
            for p in ps[1:]:
                part = part + p
            l_s[h] = alpha * l_s[h] + part
            pv = _dot(jnp.concatenate(ps, axis=1), kc)
            acc_s[h] = jnp.concatenate([alpha] * (c // LANES), axis=1) * acc_s[h] + pv
            m_s[h] = m_new

    def step(j, carry):
        chunk(j, False)
        return carry

    lax.fori_loop(0, qi, step, 0)
    chunk(qi, True)
    for h in range(heads):
        o_lat = acc_s[h] / jnp.sum(l_s[h], axis=1, keepdims=True)
        o_ref[:, h * V_DIM:(h + 1) * V_DIM] = _dot(o_lat, wuv_ref[h]).astype(o_ref.dtype)


def _attn_prompt(qa, qr, latb, krb, w_uv_t, batch, seq, tq=256):
    tq = min(tq, seq)
    nq = seq // tq
    return pl.pallas_call(
        _attn_prompt_kernel,
        grid=(batch, nq),
        in_specs=[pl.BlockSpec((MLA_HEADS, tq, KV_LORA), lambda b, i: (0, b * nq + i, 0)),
                  pl.BlockSpec((MLA_HEADS, tq, ROPE_DIM), lambda b, i: (0, b * nq + i, 0)),
                  pl.BlockSpec((seq, KV_LORA), lambda b, i: (b, 0)),
                  pl.BlockSpec((seq, ROPE_DIM), lambda b, i: (b, 0)),
                  pl.BlockSpec(w_uv_t.shape, lambda b, i: (0, 0, 0))],
        out_specs=pl.BlockSpec((tq, MLA_HEADS * V_DIM), lambda b, i: (b * nq + i, 0)),
        out_shape=jax.ShapeDtypeStruct((batch * seq, MLA_HEADS * V_DIM), BF16),
        scratch_shapes=[pltpu.VMEM((MLA_HEADS, tq, LANES), F32), pltpu.VMEM((MLA_HEADS, tq, LANES), F32),
                        pltpu.VMEM((MLA_HEADS, tq, KV_LORA), F32)],
        compiler_params=_cparams("arbitrary", "arbitrary"),
        name="attn_prompt",
    )(qa, qr, latb, krb, w_uv_t)


def _attn_decode_kernel(pt_ref, qa_ref, qr_ref, lat1_ref, kr1_ref, wuv_ref, *rest, pages):
    lat_refs = rest[:pages]
    kr_refs = rest[pages:2 * pages]
    o_ref, m_s, l_s, acc_s = rest[2 * pages:]
    j = pl.program_id(1)
    qa = qa_ref[0]
    qr = qr_ref[0]

    @pl.when(j == 0)
    def _():
        lat1 = lat1_ref[0].astype(F32)
        m_s[...] = (jnp.sum(qa.astype(F32) * lat1, axis=1, keepdims=True)
                    + jnp.sum(qr.astype(F32) * kr1_ref[0].astype(F32), axis=1, keepdims=True))
        l_s[...] = jnp.ones(l_s.shape, F32)
        acc_s[...] = jnp.broadcast_to(lat1, acc_s.shape)

    lats = [r[0].astype(BF16) for r in lat_refs]
    s = jnp.concatenate([_dot_nt(qa, lats[p]) + _dot(qr, kr_refs[p][0]) for p in range(pages)], axis=1)
    m_old = m_s[...]
    m_new = jnp.maximum(m_old, jnp.max(s, axis=1, keepdims=True))
    alpha = jnp.exp(m_old - m_new)
    p_all = jnp.exp(s - m_new)
    l_s[...] = alpha * l_s[...] + jnp.sum(p_all, axis=1, keepdims=True)
    page = lats[0].shape[0]
    pv = _dot(p_all[:, :page], lats[0])
    for p in range(1, pages):
        pv = pv + _dot(p_all[:, p * page:(p + 1) * page], lats[p])
    acc_s[...] = alpha * acc_s[...] + pv
    m_s[...] = m_new

    @pl.when(j == pl.num_programs(1) - 1)
    def _():
        o_all = _dot(acc_s[...] / l_s[...], wuv_ref[...])
        for h in range(o_all.shape[0]):
            o_ref[0, :, h * V_DIM:(h + 1) * V_DIM] = o_all[h:h + 1, h * V_DIM:(h + 1) * V_DIM].astype(o_ref.dtype)


def _attn_decode(qa, qr, lat1, kr1, w_uv_flat, cache_latent, cache_krope_t, page_table, pages=32):
    batch, n_pages = page_table.shape
    pages = min(pages, n_pages)
    page = cache_latent.shape[1]
    steps = n_pages // pages
    one = lambda shape: pl.BlockSpec((1,) + shape, lambda b, j, pt: (b,) + (0,) * len(shape))
    paged = lambda w, p: pl.BlockSpec((1, page, w) if w == KV_LORA else (1, w, page),
                                      lambda b, j, pt: (pt[b, j * pages + p], 0, 0))
    grid_spec = pltpu.PrefetchScalarGridSpec(
        num_scalar_prefetch=1,
        grid=(batch, steps),
        in_specs=[one((MLA_HEADS, KV_LORA)), one((MLA_HEADS, ROPE_DIM)), one((1, KV_LORA)), one((1, ROPE_DIM)),
                  pl.BlockSpec(w_uv_flat.shape, lambda b, j, pt: (0, 0))]
                 + [paged(KV_LORA, p) for p in range(pages)] + [paged(ROPE_DIM, p) for p in range(pages)],
        out_specs=one((1, MLA_HEADS * V_DIM)),
        scratch_shapes=[pltpu.VMEM((MLA_HEADS, 1), F32), pltpu.VMEM((MLA_HEADS, 1), F32),
                        pltpu.VMEM((MLA_HEADS, KV_LORA), F32)],
    )
    return pl.pallas_call(
        functools.partial(_attn_decode_kernel, pages=pages),
        grid_spec=grid_spec,
        out_shape=jax.ShapeDtypeStruct((batch, 1, MLA_HEADS * V_DIM), BF16),
        compiler_params=_cparams("arbitrary", "arbitrary"),
        name="attn_decode",
    )(page_table, qa, qr, lat1, kr1, w_uv_flat, *([cache_latent] * pages), *([cache_krope_t] * pages))


def _rope_tables(pos):
    half = ROPE_DIM // 2
    inv = ROPE_THETA ** (-jnp.arange(half, dtype=F32) / half)
    ang = pos.astype(F32)[:, None] * inv[None, :]
    cos, sin = jnp.cos(ang), jnp.sin(ang)
    return jnp.concatenate([cos, cos], axis=1), jnp.concatenate([-sin, sin], axis=1)


def _routing_tables(tile_counts, n_rows):
    bm = EXPERT_ROWS
    tile_counts = (tile_counts + RUN_ALIGN - 1) // RUN_ALIGN * RUN_ALIGN
    counts = jnp.sum(tile_counts, axis=0)
    padded = (counts + bm - 1) // bm * bm
    pad_end = jnp.cumsum(padded).astype(I32)
    pad_start = pad_end - padded
    tile_dst = pad_start[None, :] + jnp.cumsum(tile_counts, axis=0) - tile_counts
    tile_src = jnp.cumsum(tile_counts, axis=1) - tile_counts
    block_start = jnp.arange(n_rows // bm, dtype=I32) * bm
    block_e = jnp.sum((pad_end[None, :] <= block_start[:, None]).astype(I32), axis=1)
    block_e = jnp.minimum(block_e, N_EXPERTS - 1)
    first = jnp.concatenate([jnp.ones((1,), I32), (block_e[1:] != block_e[:-1]).astype(I32)])
    n_used = pad_end[-1:] // bm
    return pad_end, padded, tile_src.astype(I32), tile_dst.astype(I32), tile_counts, block_e, first, n_used


def kernel(x_prompt, x_sample, state_mlstm_C, state_mlstm_n, state_mlstm_m, cache_latent, cache_krope, page_table, c_prompt, c_sample, w_ada, b_ada, g_mix, g_ffn, w_mlstm_in, b_mlstm_gates, g_mlstm_head, w_mlstm_out, w_ada_kv, b_ada_kv, g_kv_in, w_dkv, g_kv_lat, w_uk, w_uv, w_dq, g_q_lat, w_uq, w_mla_out, w_router, b_router, w_exp_gate, w_exp_up, w_exp_down, w_sh_gate, w_sh_up, w_sh_down, g_final):
    bp, tp, d = x_prompt.shape
    bs, ts, _ = x_sample.shape
    depth = w_ada.shape[0]
    n_a = w_mlstm_in.shape[0]
    past_len = page_table.shape[1] * cache_latent.shape[1]
    hk, hv = ML_HEADS * ML_DK, ML_HEADS * ML_DV
    n_p, n_s = bp * tp, bs * ts
    n_tot = n_p + n_s
    n_moe_tiles = n_p // MOE_TILE + 1
    n_rows = n_tot * TOP_K + n_moe_tiles * N_EXPERTS * (RUN_ALIGN - 1) + N_EXPERTS * (EXPERT_ROWS - 1)
    n_rows = -(-n_rows // EXPERT_ROWS) * EXPERT_ROWS

    groups = [
        dict(batch=bp, seq=tp, tok=_Tok(bp, tp, 256), x=x_prompt.reshape(n_p, d), pos0=0,
             C0=jnp.zeros((n_a, bp, ML_HEADS, ML_DK, ML_DV), F32), n0=jnp.zeros((n_a, bp, ML_HEADS, ML_DK), F32),
             m0=jnp.full((n_a, bp, ML_HEADS), M_INIT, F32)),
        dict(batch=bs, seq=ts, tok=_Tok(bs, ts, 256), x=x_sample.reshape(n_s, d), pos0=past_len,
             C0=state_mlstm_C, n0=state_mlstm_n, m0=state_mlstm_m),
    ]
    c_all = jnp.concatenate([c_prompt, c_sample], axis=0)
    row0 = [0, bp]
    for g in groups:
        g["Cs"], g["ns"], g["ms"] = [], [], []

    for g in groups:
        g["cos2"], g["sin2"] = _rope_tables(g["pos0"] + jnp.arange(g["seq"], dtype=I32))
        if g["seq"] == 1:
            g["cos2"] = jnp.broadcast_to(g["cos2"], (g["batch"], ROPE_DIM))
            g["sin2"] = jnp.broadcast_to(g["sin2"], (g["batch"], ROPE_DIM))

    mod_kv = _ada(c_all, w_ada_kv, b_ada_kv)
    w_uk_t = jnp.transpose(w_uk, (1, 2, 0))
    w_uv_t = jnp.transpose(w_uv, (1, 0, 2)).astype(BF16)

    for l in range(depth):
        mod = _ada(c_all, w_ada, b_ada[l], layer=l)
        w_router_t = w_router[l].T
        ws_gu = jnp.concatenate([w_sh_gate[l], w_sh_up[l]], axis=1).astype(BF16)
        ws_d = w_sh_down[l].astype(BF16)
        for gi, g in enumerate(groups):
            r0, nb = row0[gi], g["batch"]
            sh_m, sc_m, g_m, sh_f, sc_f, g_f = [mod[r0:r0 + nb, i * d:(i + 1) * d] for i in range(6)]
            g["g_f"] = g_f
            tok = g["tok"]
            wdt = F32 if g["seq"] == 1 else BF16
            if l < n_a:
                w_in = w_mlstm_in[l]
                w_main = w_in[:, :2 * hk + 2 * hv].astype(wdt)
                w_gates = jnp.pad(w_in[:, 2 * hk + 2 * hv:], ((0, 0), (0, 128 - 2 * ML_HEADS)))
                b_gates = jnp.pad(b_mlstm_gates[l], (0, 128 - 2 * ML_HEADS)).reshape(1, 128)
                if g["seq"] == 1:
                    proj, gates = _in0(tok, g["x"], sh_m, sc_m, g_mix[l], w_main, w_gates, b_gates, F32)
                    a, Cn, nn, mn = _mlstm_step(proj, gates, g["C0"][l], g["n0"][l], g["m0"][l],
                                                g_mlstm_head[l], nb)
                    a = a.reshape(nb, hv)
                else:
                    proj, gates = _in0(tok, g["x"], sh_m, sc_m, g_mix[l], w_main, w_gates, b_gates, BF16)
                    a, Cn, nn, mn = _mlstm_chunks(proj, gates, g["C0"][l], g["n0"][l], g["m0"][l],
                                                  g_mlstm_head[l], nb, g["seq"])
                g["Cs"].append(Cn)
                g["ns"].append(nn)
                g["ms"].append(mn.reshape(nb, ML_HEADS))
                w_o = w_mlstm_out[l].astype(wdt)
            else:
                j = l - n_a
                w_q = w_uq[j].reshape(-1, MLA_HEADS, NOPE_DIM + ROPE_DIM)
                w_q_nope = w_q[:, :, :NOPE_DIM].reshape(-1, MLA_HEADS * NOPE_DIM).astype(wdt)
                w_q_rope = w_q[:, :, NOPE_DIM:].reshape(-1, MLA_HEADS * ROPE_DIM).astype(wdt)
                sh_kv, sc_kv = mod_kv[r0:r0 + nb, :d], mod_kv[r0:r0 + nb, d:]
                lat, kr, latb, krb, qa, qr = _in1(tok, g["x"], sh_kv, sc_kv, g_kv_in, w_dkv.astype(wdt), g_kv_lat,
                                                  g["cos2"], g["sin2"], sh_m, sc_m, g_mix[l], w_dq[j].astype(wdt),
                                                  g_q_lat[j], w_q_nope, w_q_rope, w_uk_t.astype(wdt))
                if l == n_a:
                    g["lat"], g["kr"], g["latb"], g["krb"] = lat, kr, latb, krb
                if g["seq"] == 1:
                    a = _attn_decode(jnp.transpose(qa, (1, 0, 2)), jnp.transpose(qr, (1, 0, 2)),
                                     g["latb"].reshape(nb, 1, KV_LORA), g["krb"].reshape(nb, 1, ROPE_DIM),
                                     w_uv.reshape(KV_LORA, MLA_HEADS * V_DIM).astype(BF16),
                                     cache_latent, jnp.swapaxes(cache_krope, 1, 2),
                                     page_table).reshape(nb, MLA_HEADS * V_DIM)
                else:
                    a = _attn_prompt(qa, qr, g["latb"], g["krb"], w_uv_t, nb, g["seq"])
                w_o = w_mla_out[j].astype(wdt)
            g["xs"], g["hp"], g["idx"], g["wt"], g["rank"], g["cnt"] = _post(
                tok, g["x"], a, w_o, g_m, g_ffn[l], sh_f, sc_f, g_f, w_router_t.astype(wdt), b_router[l], ws_gu, ws_d)

        gp, gs = groups
        pad_t = MOE_TILE - n_s
        lanes = lambda v, fill: jnp.concatenate(
            [gp[v], jnp.pad(gs[v], ((0, 0), (0, pad_t)), constant_values=fill)], axis=1)
        idx_all, wt_all = lanes("idx", 0), lanes("wt", 0)
        rank_all = lanes("rank", TILE_ROWS)
        tile_counts = jnp.concatenate([gp["cnt"][:, :, 0], gs["cnt"][:, :, 0]], axis=0).astype(I32)
        pad_end, padded, tsrc, tdst, tile_counts, block_e, first, n_used = _routing_tables(tile_counts, n_rows)
        rows = lambda v: jnp.pad(v, ((0, pad_t), (0, 0)))
        xs_sorted = _dispatch(gp["hp"], rows(gs["hp"]), idx_all, rank_all, pad_end, padded, tsrc, tdst, tile_counts,
                              n_rows)
        out_sorted = _experts(xs_sorted, block_e, first, n_used, w_exp_gate, w_exp_up, w_exp_down, l)
        last = l == depth - 1
        gp["x"] = _combine(gp["tok"], 0, gp["xs"], gp["g_f"], g_final, idx_all, rank_all, wt_all, out_sorted,
                           tsrc, tdst, tile_counts, last, clear=False)
        gs["x"] = _combine(_Tok(MOE_TILE, 1, MOE_TILE), gp["tok"].n_tiles, rows(gs["xs"]), rows(gs["g_f"]), g_final,
                           idx_all, rank_all, wt_all, out_sorted, tsrc, tdst, tile_counts, last, clear=True)[:n_s]

    outs = []
    for g in groups:
        nb, seq = g["batch"], g["seq"]
        outs.append((g["x"].reshape(nb, seq, d), jnp.stack(g["Cs"]), jnp.stack(g["ns"]), jnp.stack(g["ms"]),
                     g["lat"].reshape(nb, seq, KV_LORA), g["kr"].reshape(nb, seq, ROPE_DIM)))
    p, s = outs
    return (p[0], s[0], p[1], p[2], p[3], p[4], p[5], s[1], s[2], s[3], s[4], s[5])
```

```python
import functools

import jax
import jax.numpy as jnp
from jax import lax
from jax.experimental import pallas as pl
from jax.experimental.pallas import tpu as pltpu

F32 = jnp.float32
BF16 = jnp.bfloat16
I32 = jnp.int32
HIGHEST = lax.Precision.HIGHEST
NEG_INF = float("-inf")

ML_HEADS = 4
ML_DK = 128
ML_DV = 256
ML_CHUNK = 128
M_INIT = -1e30
MLA_HEADS = 8
NOPE_DIM = 128
ROPE_DIM = 64
V_DIM = 128
KV_LORA = 256
ROPE_THETA = 10000.0
N_EXPERTS = 64
TOP_K = 8
N_GROUPS = 8
TOPK_GROUPS = 4
ROUTED_SCALE = 2.5
NORM_EPS = 1e-6

LANES = 128
EXPERT_ROWS = 1024
VMEM_LIMIT = 56 * 1024 * 1024


def _cparams(*sem):
    return pltpu.CompilerParams(dimension_semantics=sem, vmem_limit_bytes=VMEM_LIMIT)


def _dot(a, b):
    return jnp.dot(a.astype(BF16), b.astype(BF16), preferred_element_type=F32)


def _dot_nt(a, b):
    return lax.dot_general(a.astype(BF16), b.astype(BF16), (((1,), (1,)), ((), ())),
                           preferred_element_type=F32)


def _dot_f32(a, b):
    return jnp.dot(a, b, precision=HIGHEST, preferred_element_type=F32)


def _dot_w(a, w):
    return _dot_f32(a.astype(F32), w) if w.dtype == F32 else _dot(a, w)


def _dot_w_nt(w, a):
    if w.dtype == F32:
        return lax.dot_general(w, a.astype(F32), (((1,), (1,)), ((), ())), precision=HIGHEST,
                               preferred_element_type=F32)
    return _dot_nt(w, a)


def _rms(x, g):
    return x * lax.rsqrt(jnp.mean(x * x, axis=-1, keepdims=True) + NORM_EPS) * g


def _rms_mod(x, g, shift, scale):
    return _rms(x, g) * (1.0 + scale) + shift


def _sigmoid(x):
    return 1.0 / (1.0 + jnp.exp(-x))


def _silu(x):
    return x * _sigmoid(x)


def _log_sigmoid(x):
    return jnp.minimum(x, 0.0) - jnp.log1p(jnp.exp(-jnp.abs(x)))


def _ada_kernel(c_ref, w_ref, b_ref, o_ref):
    o_ref[...] = _dot_f32(_silu(c_ref[...]), w_ref[...]) + b_ref[...]


def _ada(c, w, b, layer=None, tn=1024):
    m, d = c.shape
    n_out = w.shape[-1]
    if layer is None:
        w_spec = pl.BlockSpec((d, tn), lambda j: (0, j))
    else:
        w_spec = pl.BlockSpec((None, d, tn), lambda j: (layer, 0, j))
    return pl.pallas_call(
        _ada_kernel,
        grid=(n_out // tn,),
        in_specs=[pl.BlockSpec((m, d), lambda j: (0, 0)), w_spec, pl.BlockSpec((1, tn), lambda j: (0, j))],
        out_specs=pl.BlockSpec((m, tn), lambda j: (0, j)),
        out_shape=jax.ShapeDtypeStruct((m, n_out), F32),
        compiler_params=_cparams("arbitrary"),
        name="ada",
    )(c, w, b.reshape(1, n_out))


class _Tok:
    def __init__(self, batch, seq, tile):
        if seq == 1:
            self.tm, self.n_tiles, self.per_seq = batch, 1, 1
            self.mod_block = (1, batch, None)
        else:
            self.tm = min(tile, seq)
            self.per_seq = seq // self.tm
            self.n_tiles = batch * self.per_seq
            self.mod_block = (1, 1, None)
        self.batch, self.seq = batch, seq

    def mod(self, v):
        d = v.shape[-1]
        if self.seq == 1:
            return v.reshape(1, self.batch, d), pl.BlockSpec((1, self.batch, d), lambda i, *_: (0, 0, 0))
        per = self.per_seq
        return v.reshape(self.batch, 1, d), pl.BlockSpec((1, 1, d), lambda i, *_: (i // per, 0, 0))

    def rows(self, width):
        return pl.BlockSpec((self.tm, width), lambda i, *_: (i, 0))

    def seq_rows(self, width):
        per = self.per_seq
        return pl.BlockSpec((self.tm, width), lambda i, *_: (i % per, 0))


def _full(shape):
    nd = len(shape)
    return pl.BlockSpec(shape, lambda i, *_: (0,) * nd)


def _in0_kernel(x_ref, sh_ref, sc_ref, g_ref, w_ref, wg_ref, bg_ref, o_ref, gt_ref):
    h = _rms_mod(x_ref[...], g_ref[...], sh_ref[0], sc_ref[0])
    o_ref[...] = _dot_w(h, w_ref[...]).astype(o_ref.dtype)
    gt_ref[...] = _dot_f32(h, wg_ref[...]) + bg_ref[...]


def _in0(tok, x, shift, scale, g, w, w_gates, b_gates, out_dtype):
    n, d = x.shape
    wo = w.shape[1]
    sh, sh_spec = tok.mod(shift)
    sc, sc_spec = tok.mod(scale)
    return pl.pallas_call(
        _in0_kernel,
        grid=(tok.n_tiles,),
        in_specs=[tok.rows(d), sh_spec, sc_spec, _full((1, d)), _full((d, wo)), _full((d, 128)), _full((1, 128))],
        out_specs=[tok.rows(wo), tok.rows(128)],
        out_shape=[jax.ShapeDtypeStruct((n, wo), out_dtype), jax.ShapeDtypeStruct((n, 128), F32)],
        compiler_params=_cparams("arbitrary"),
        name="in0",
    )(x, sh, sc, g.reshape(1, d), w, w_gates, b_gates)


def _mlstm_chunk_kernel(q_ref, k_ref, v_ref, og_ref, gt_ref, c0_ref, n0_ref, m0_ref, gh_ref,
                        hh_ref, c_ref, n_ref, m_ref):
    L = q_ref.shape[0]
    scale = ML_DK ** -0.5

    @pl.when(pl.program_id(1) == 0)
    def _():
        c_ref[...] = c0_ref[...]
        n_ref[...] = n0_ref[...]
        m_ref[...] = m0_ref[...]

    gates = gt_ref[...]
    logf = _log_sigmoid(gates)
    row = lax.broadcasted_iota(I32, (L, L), 0)
    col = lax.broadcasted_iota(I32, (L, L), 1)
    causal = col <= row
    tril = causal.astype(F32)
    b_cols = _dot_f32(tril, logf)
    gates_t = gates.T
    b_rows = _dot_f32(logf.T, (row <= col).astype(F32))

    for h in range(ML_HEADS):
        i_row = gates_t[h:h + 1, :]
        i_col = gates[:, h:h + 1]
        b_row = b_rows[ML_HEADS + h:ML_HEADS + h + 1, :]
        b_col = b_cols[:, ML_HEADS + h:ML_HEADS + h + 1]
        m_prev = m_ref[0, :, h:h + 1]
        c_prev = c_ref[0, h]
        n_prev = n_ref[0, h:h + 1, :]
        q = q_ref[:, h * ML_DK:(h + 1) * ML_DK]
        k = k_ref[:, h * ML_DK:(h + 1) * ML_DK]
        v = v_ref[:, h * ML_DV:(h + 1) * ML_DV]

        g_col = b_col + m_prev
        dmat = jnp.where(causal, b_col - b_row + i_row, NEG_INF)
        m_t = jnp.maximum(g_col, jnp.max(dmat, axis=1, keepdims=True))
        w_intra = jnp.exp(dmat - m_t)
        w_inter = jnp.exp(g_col - m_t)
        s = _dot_nt(q, k) * (w_intra * scale)
        num = _dot(s, v) + w_inter * _dot(q, c_prev)
        qn = jnp.sum(q.astype(F32) * n_prev, axis=1, keepdims=True)
        den = jnp.sum(s, axis=1, keepdims=True) + w_inter * qn
        cell = num / jnp.maximum(jnp.abs(den), jnp.exp(-m_t))

        b_last = b_col[L - 1:L, :]
        m_new = m_t[L - 1:L, :]
        a_inter = jnp.exp(b_last + m_prev - m_new)
        a_intra = jnp.exp(b_last - b_col + i_col - m_new) * scale
        kw = k.astype(F32) * a_intra
        c_ref[0, h] = a_inter * c_prev + _dot(kw.T, v)
        n_ref[0, h:h + 1, :] = a_inter * n_prev + jnp.sum(kw, axis=0, keepdims=True)
        m_ref[0, :, h:h + 1] = m_new

        y = _rms(cell, gh_ref[:, h * ML_DV:(h + 1) * ML_DV])
        og = og_ref[:, h * ML_DV:(h + 1) * ML_DV].astype(F32)
        hh_ref[:, h * ML_DV:(h + 1) * ML_DV] = (y * _sigmoid(og)).astype(hh_ref.dtype)


def _mlstm_chunks(proj, gates, c0, n0, m0, g_head, batch, seq):
    L = seq if seq <= ML_CHUNK else ML_CHUNK
    nc = seq // L
    hk, hv = ML_HEADS * ML_DK, ML_HEADS * ML_DV
    rows = lambda w, cb: pl.BlockSpec((L, w), lambda b, j: (b * nc + j, cb))
    state = lambda shape: pl.BlockSpec((1,) + shape, lambda b, j: (b,) + (0,) * len(shape))
    return pl.pallas_call(
        _mlstm_chunk_kernel,
        grid=(batch, nc),
        in_specs=[rows(hk, 0), rows(hk, 1), rows(hv, 2 * hk // hv), rows(hv, 2 * hk // hv + 1), rows(128, 0),
                  state((ML_HEADS, ML_DK, ML_DV)), state((ML_HEADS, ML_DK)), state((1, ML_HEADS)),
                  pl.BlockSpec((1, hv), lambda b, j: (0, 0))],
        out_specs=[rows(hv, 0), state((ML_HEADS, ML_DK, ML_DV)), state((ML_HEADS, ML_DK)), state((1, ML_HEADS))],
        out_shape=[jax.ShapeDtypeStruct((batch * seq, hv), BF16),
                   jax.ShapeDtypeStruct((batch, ML_HEADS, ML_DK, ML_DV), F32),
                   jax.ShapeDtypeStruct((batch, ML_HEADS, ML_DK), F32),
                   jax.ShapeDtypeStruct((batch, 1, ML_HEADS), F32)],
        compiler_params=_cparams("arbitrary", "arbitrary"),
        name="mlstm_chunks",
    )(proj, proj, proj, proj, gates, c0, n0, m0.reshape(batch, 1, ML_HEADS), g_head.reshape(1, hv))


def _row_to_col(r):
    n = r.shape[1]
    eye = lax.broadcasted_iota(I32, (n, n), 0) == lax.broadcasted_iota(I32, (n, n), 1)
    return jnp.sum(jnp.where(eye, jnp.broadcast_to(r, (n, n)), 0.0), axis=1, keepdims=True)


def _mlstm_step_kernel(p_ref, gt_ref, c0_ref, n0_ref, m0_ref, gh_ref, hh_ref, c_ref, n_ref, m_ref):
    scale = ML_DK ** -0.5
    hk, hv = ML_HEADS * ML_DK, ML_HEADS * ML_DV
    gates = gt_ref[0]
    logf = _log_sigmoid(gates)
    for h in range(ML_HEADS):
        q = p_ref[0, :, h * ML_DK:(h + 1) * ML_DK].astype(F32)
        k = p_ref[0, :, hk + h * ML_DK:hk + (h + 1) * ML_DK].astype(F32)
        v = p_ref[0, :, 2 * hk + h * ML_DV:2 * hk + (h + 1) * ML_DV].astype(F32)
        og = p_ref[0, :, 2 * hk + hv + h * ML_DV:2 * hk + hv + (h + 1) * ML_DV].astype(F32)
        log_i = gates[:, h:h + 1]
        m_prev = m0_ref[0, :, h:h + 1]
        c_prev = c0_ref[0, h]
        n_prev = n0_ref[0, h:h + 1, :]

        g = logf[:, ML_HEADS + h:ML_HEADS + h + 1] + m_prev
        m_t = jnp.maximum(g, log_i)
        w_intra = jnp.exp(log_i - m_t)
        w_inter = jnp.exp(g - m_t)
        s = jnp.sum(q * k, axis=1, keepdims=True) * (w_intra * scale)
        q_c = _row_to_col(q)
        num = s * v + w_inter * jnp.sum(q_c * c_prev, axis=0, keepdims=True)
        den = s + w_inter * jnp.sum(q * n_prev, axis=1, keepdims=True)
        cell = num / jnp.maximum(jnp.abs(den), jnp.exp(-m_t))

        kw = k * (w_intra * scale)
        c_ref[0, h] = w_inter * c_prev + _row_to_col(kw) * v
        n_ref[0, h:h + 1, :] = w_inter * n_prev + kw
        m_ref[0, :, h:h + 1] = m_t

        y = _rms(cell, gh_ref[:, h * ML_DV:(h + 1) * ML_DV])
        hh_ref[0, :, h * ML_DV:(h + 1) * ML_DV] = (y * _sigmoid(og)).astype(hh_ref.dtype)


def _mlstm_step(proj, gates, c0, n0, m0, g_head, batch):
    hv = ML_HEADS * ML_DV
    wp = proj.shape[1]
    one = lambda shape: pl.BlockSpec((1,) + shape, lambda b: (b,) + (0,) * len(shape))
    return pl.pallas_call(
        _mlstm_step_kernel,
        grid=(batch,),
        in_specs=[one((1, wp)), one((1, 128)), one((ML_HEADS, ML_DK, ML_DV)), one((ML_HEADS, ML_DK)),
                  one((1, ML_HEADS)), pl.BlockSpec((1, hv), lambda b: (0, 0))],
        out_specs=[one((1, hv)), one((ML_HEADS, ML_DK, ML_DV)), one((ML_HEADS, ML_DK)), one((1, ML_HEADS))],
        out_shape=[jax.ShapeDtypeStruct((batch, 1, hv), F32),
                   jax.ShapeDtypeStruct((batch, ML_HEADS, ML_DK, ML_DV), F32),
                   jax.ShapeDtypeStruct((batch, ML_HEADS, ML_DK), F32),
                   jax.ShapeDtypeStruct((batch, 1, ML_HEADS), F32)],
        compiler_params=_cparams("arbitrary"),
        name="mlstm_step",
    )(proj.reshape(batch, 1, wp), gates.reshape(batch, 1, 128), c0, n0, m0.reshape(batch, 1, ML_HEADS),
      g_head.reshape(1, hv))


def _group_reduce(x, op):
    e, tm = x.shape
    per = e // N_GROUPS
    x3 = x.reshape(N_GROUPS, per, tm)
    r = op(x3, axis=1, keepdims=True)
    return jnp.broadcast_to(r, (N_GROUPS, per, tm)).reshape(e, tm)


def _route(logits_t, bias_t):
    e, tm = logits_t.shape
    per = e // N_GROUPS
    scores = _sigmoid(logits_t)
    sel = scores + bias_t
    eidx = lax.broadcasted_iota(I32, (e, tm), 0)
    jidx = eidx % per
    gidx = eidx // per
    m1 = _group_reduce(sel, jnp.max)
    first1 = _group_reduce(jnp.where(sel == m1, jidx, per), jnp.min)
    m2 = _group_reduce(jnp.where(jidx == first1, NEG_INF, sel), jnp.max)
    gs = m1 + m2
    chosen = jnp.zeros((e, tm), F32)
    for _ in range(TOPK_GROUPS):
        mx = jnp.max(gs, axis=0, keepdims=True)
        f = jnp.min(jnp.where(gs == mx, gidx, N_GROUPS), axis=0, keepdims=True)
        hit = gidx == f
        chosen = jnp.where(hit, 1.0, chosen)
        gs = jnp.where(hit, NEG_INF, gs)
    cand = jnp.where(chosen > 0.0, sel, NEG_INF)
    picked = jnp.zeros((e, tm), F32)
    idx, wts = [], []
    for _ in range(TOP_K):
        mx = jnp.max(cand, axis=0, keepdims=True)
        f = jnp.min(jnp.where(cand == mx, eidx, e), axis=0, keepdims=True)
        hit = eidx == f
        idx.append(f)
        wts.append(jnp.sum(jnp.where(hit, scores, 0.0), axis=0, keepdims=True))
        cand = jnp.where(hit, NEG_INF, cand)
        picked = jnp.where(hit, 1.0, picked)
    return idx, wts, picked


def _post_kernel(x_ref, a_ref, wo_ref, gm_ref, gn_ref, sh_ref, sc_ref, gf_ref, wr_ref, br_ref,
                 wsgu_ref, wsd_ref,
                 xs_ref, hp_ref, idx_ref, wt_ref, rank_ref, cnt_ref):
    tm, d = x_ref.shape

    x1 = x_ref[...] + gm_ref[0] * _dot_w(a_ref[...], wo_ref[...])
    h2 = _rms_mod(x1, gn_ref[...], sh_ref[0], sc_ref[0])
    hp_ref[...] = h2.astype(BF16)

    ff = wsd_ref.shape[0]
    gu = _dot(h2, wsgu_ref[...])
    shared = _dot(_silu(gu[:, :ff]) * gu[:, ff:], wsd_ref[...])
    xs_ref[...] = x1 + gf_ref[0] * shared

    logits_t = _dot_w_nt(wr_ref[...], h2)
    idx, wts, picked = _route(logits_t, br_ref[:, 0:1])
    e = picked.shape[0]
    eidx = lax.broadcasted_iota(I32, (e, tm), 0)
    before = (lax.broadcasted_iota(I32, (tm, tm), 0) < lax.broadcasted_iota(I32, (tm, tm), 1)).astype(BF16)
    rank_t = jnp.dot(picked.astype(BF16), before, preferred_element_type=F32)
    total = wts[0]
    for k in range(1, TOP_K):
        total = total + wts[k]
    for k in range(TOP_K):
        idx_ref[k:k + 1, :] = idx[k]
        wt_ref[k:k + 1, :] = wts[k] / total * ROUTED_SCALE
        rank_ref[k:k + 1, :] = jnp.sum(jnp.where(eidx == idx[k], rank_t, 0.0), axis=0, keepdims=True).astype(I32)
    cnt_ref[0] = jnp.broadcast_to(jnp.sum(picked, axis=1, keepdims=True), cnt_ref.shape[1:])


def _post(tok, x, a, w_o, g_m, g_norm, shift, scale, g_f, w_router_t, b_router, ws_gu, ws_d):
    n, d = x.shape
    e = w_router_t.shape[0]
    ff = ws_d.shape[0]
    gm, gm_spec = tok.mod(g_m)
    sh, sh_spec = tok.mod(shift)
    sc, sc_spec = tok.mod(scale)
    gf, gf_spec = tok.mod(g_f)
    cols = lambda rows: pl.BlockSpec((rows, tok.tm), lambda i: (0, i))
    return pl.pallas_call(
        _post_kernel,
        grid=(tok.n_tiles,),
        in_specs=[tok.rows(d), tok.rows(a.shape[1]), _full(w_o.shape), gm_spec, _full((1, d)), sh_spec, sc_spec,
                  gf_spec, _full((e, d)), _full((e, 128)), _full((d, 2 * ff)), _full((ff, d))],
        out_specs=[tok.rows(d), tok.rows(d), cols(TOP_K), cols(TOP_K), cols(TOP_K),
                   pl.BlockSpec((1, e, LANES), lambda i: (i, 0, 0))],
        out_shape=[jax.ShapeDtypeStruct((n, d), F32), jax.ShapeDtypeStruct((n, d), BF16),
                   jax.ShapeDtypeStruct((TOP_K, n), I32), jax.ShapeDtypeStruct((TOP_K, n), F32),
                   jax.ShapeDtypeStruct((TOP_K, n), I32), jax.ShapeDtypeStruct((tok.n_tiles, e, LANES), F32)],
        compiler_params=_cparams("arbitrary"),
        name="post",
    )(x, a, w_o, gm, g_norm.reshape(1, d), sh, sc, gf, w_router_t,
      jnp.broadcast_to(b_router.reshape(e, 1), (e, 128)), ws_gu, ws_d)


MOE_TILE = 256
RUN_ALIGN = 16
POS_CHUNK = 512
TILE_ROWS = -(-(MOE_TILE * TOP_K + N_EXPERTS * (RUN_ALIGN - 1)) // POS_CHUNK) * POS_CHUNK
TILE_PIECES = TILE_ROWS // RUN_ALIGN


def _tile_positions(idx_ref, rank_ref, tsrc_ref, tile):
    idx = idx_ref[...]

    def add(e, acc):
        return acc + jnp.where(idx == e, tsrc_ref[tile, e], 0)

    return lax.fori_loop(0, N_EXPERTS, add, rank_ref[...])


def _start_piece(tile, j, pdst_ref, make_copy):
    make_copy(pl.multiple_of(j * RUN_ALIGN, RUN_ALIGN),
              pl.multiple_of(pdst_ref[tile * TILE_PIECES + j], RUN_ALIGN), RUN_ALIGN).start()


def _start_pieces(tile, pdst_ref, npc_ref, make_copy, first=0):
    def body(j, c):
        _start_piece(tile, j, pdst_ref, make_copy)
        return c

    lax.fori_loop(first, npc_ref[tile], body, 0)


FULL_TILE_PIECES = MOE_TILE * TOP_K // RUN_ALIGN
CHUNK_PIECES = POS_CHUNK // RUN_ALIGN


def _wait_pieces(tile, npc_ref, make_copy):
    total = npc_ref[tile] * RUN_ALIGN
    chunk = 1 << (TILE_ROWS.bit_length() - 1)
    while chunk >= RUN_ALIGN:
        @pl.when((total & chunk) != 0)
        def _():
            make_copy(0, 0, chunk).wait()
        chunk //= 2


def _dispatch_kernel(pe_ref, pd_ref, tsrc_ref, pdst_ref, npc_ref, idx_ref, rank_ref, hpp_ref, hps_ref,
                     xs_ref, srt, zeros, sem):
    i = pl.program_id(0)
    last = pl.num_programs(0) - 1
    slot = i % 2
    tm = hpp_ref.shape[0]
    bm = zeros.shape[0]

    def last_block(e):
        return pltpu.make_async_copy(zeros, xs_ref.at[pl.ds(pl.multiple_of(pe_ref[e] - bm, bm), bm)], sem.at[2])

    @pl.when(i == 0)
    def _():
        zeros[...] = jnp.zeros_like(zeros)

        def zstart(e, c):
            @pl.when(pd_ref[e] > 0)
            def _():
                last_block(e).start()
            return c

        def zwait(e, c):
            @pl.when(pd_ref[e] > 0)
            def _():
                last_block(e).wait()
            return c

        lax.fori_loop(0, N_EXPERTS, zstart, 0)
        lax.fori_loop(0, N_EXPERTS, zwait, 0)

    pos = _tile_positions(idx_ref, rank_ref, tsrc_ref, i)
    x = jnp.where(i == last, hps_ref[...], hpp_ref[...])
    def sort_chunk(c):
        p_iota = c * POS_CHUNK + lax.broadcasted_iota(I32, (POS_CHUNK, tm), 0)
        hit = pos[0:1, :] == p_iota
        for k in range(1, TOP_K):
            hit = jnp.logical_or(hit, pos[k:k + 1, :] == p_iota)
        onehot = jnp.where(hit, 1.0, 0.0).astype(BF16)
        srt[slot, c * POS_CHUNK:(c + 1) * POS_CHUNK, :] = jnp.dot(onehot, x,
                                                                  preferred_element_type=F32).astype(BF16)

    def copier(buf_slot):
        def copy(s, d, rows):
            return pltpu.make_async_copy(srt.at[buf_slot, pl.ds(s, rows)], xs_ref.at[pl.ds(d, rows)],
                                         sem.at[buf_slot])
        return copy

    def sort_tile(full):
        always = tm * TOP_K // POS_CHUNK if full else 0
        for c in range(TILE_ROWS // POS_CHUNK):
            if c < always:
                sort_chunk(c)
                for j in range(c * CHUNK_PIECES, (c + 1) * CHUNK_PIECES):
                    _start_piece(i, j, pdst_ref, copier(slot))
            else:
                pl.when(npc_ref[i] * RUN_ALIGN > c * POS_CHUNK)(functools.partial(sort_chunk, c))
        _start_pieces(i, pdst_ref, npc_ref, copier(slot), first=FULL_TILE_PIECES if full else 0)

    pl.when(i < last)(functools.partial(sort_tile, True))
    pl.when(i == last)(functools.partial(sort_tile, False))

    @pl.when(i > 0)
    def _():
        _wait_pieces(i - 1, npc_ref, copier(1 - slot))

    @pl.when(i == last)
    def _():
        _wait_pieces(i, npc_ref, copier(slot))


def _dispatch(hp_p, hp_s, idx_all, rank_all, pad_end, padded, tsrc, pdst, npc, n_rows):
    tm = MOE_TILE
    d = hp_p.shape[1]
    n_tiles = tsrc.shape[0]
    cols = pl.BlockSpec((TOP_K, tm), lambda i, *_: (0, i))
    return pl.pallas_call(
        _dispatch_kernel,
        grid_spec=pltpu.PrefetchScalarGridSpec(
            num_scalar_prefetch=5, grid=(n_tiles,),
            in_specs=[cols, cols,
                      pl.BlockSpec((tm, d), lambda i, *_: (jnp.minimum(i, n_tiles - 2), 0)),
                      pl.BlockSpec((tm, d), lambda i, *_: (0, 0))],
            out_specs=pl.BlockSpec(memory_space=pl.ANY),
            scratch_shapes=[pltpu.VMEM((2, TILE_ROWS, d), BF16), pltpu.VMEM((EXPERT_ROWS, d), BF16),
                            pltpu.SemaphoreType.DMA((3,))]),
        out_shape=jax.ShapeDtypeStruct((n_rows, d), BF16),
        compiler_params=_cparams("arbitrary"),
        name="dispatch",
    )(pad_end, padded, tsrc, pdst, npc, idx_all, rank_all, hp_p, hp_s)


def _expert_kernel(be_ref, first_ref, nu_ref, x_ref, wg_ref, wu_ref, wd_ref, o_ref, wgu_s, wd_s):
    i = pl.program_id(0)
    ff = wd_s.shape[0]

    @pl.when(i < nu_ref[0])
    def _():
        @pl.when(first_ref[i] == 1)
        def _():
            wgu_s[:, :ff] = wg_ref[...].astype(BF16)
            wgu_s[:, ff:] = wu_ref[...].astype(BF16)
            wd_s[...] = wd_ref[...].astype(BF16)

        gu = jnp.dot(x_ref[...], wgu_s[...], preferred_element_type=F32)
        act = _silu(gu[:, :ff]) * gu[:, ff:]
        o_ref[...] = jnp.dot(act.astype(BF16), wd_s[...], preferred_element_type=F32).astype(o_ref.dtype)

    @pl.when(i >= nu_ref[0])
    def _():
        o_ref[...] = jnp.zeros_like(o_ref)


def _experts(xs, block_e, first, n_used, w_gate, w_up, w_down, layer):
    n_rows, d = xs.shape
    ff = w_gate.shape[-1]
    bm = EXPERT_ROWS
    grid_spec = pltpu.PrefetchScalarGridSpec(
        num_scalar_prefetch=3,
        grid=(n_rows // bm,),
        in_specs=[pl.BlockSpec((bm, d), lambda i, be, fi, nu: (jnp.minimum(i, nu[0] - 1), 0)),
                  pl.BlockSpec((None, None, d, ff), lambda i, be, fi, nu: (layer, be[i], 0, 0)),
                  pl.BlockSpec((None, None, d, ff), lambda i, be, fi, nu: (layer, be[i], 0, 0)),
                  pl.BlockSpec((None, None, ff, d), lambda i, be, fi, nu: (layer, be[i], 0, 0))],
        out_specs=pl.BlockSpec((bm, d), lambda i, be, fi, nu: (i, 0)),
        scratch_shapes=[pltpu.VMEM((d, 2 * ff), BF16), pltpu.VMEM((ff, d), BF16)],
    )
    return pl.pallas_call(
        _expert_kernel,
        grid_spec=grid_spec,
        out_shape=jax.ShapeDtypeStruct((n_rows, d), BF16),
        compiler_params=_cparams("arbitrary"),
        name="experts",
    )(block_e, first, n_used, xs, w_gate, w_up, w_down)


def _combine_kernel(tsrc_ref, pdst_ref, npc_ref, idx_ref, rank_ref, w_ref, xs_ref, gf_ref, gfin_ref, os_ref,
                    y_ref, buf, sem, *, tile0, clear, final_norm):
    i = pl.program_id(0)
    tile = tile0 + i
    slot = i % 2
    tm, d = xs_ref.shape

    def copier(buf_slot):
        def copy(s, d, rows):
            return pltpu.make_async_copy(os_ref.at[pl.ds(d, rows)], buf.at[buf_slot, pl.ds(s, rows)],
                                         sem.at[buf_slot])
        return copy

    @pl.when(i == 0)
    def _():
        first_free = 0 if clear else tm * TOP_K
        buf[:, first_free:, :] = jnp.zeros((2, buf.shape[1] - first_free, d), buf.dtype)
        _start_pieces(tile, pdst_ref, npc_ref, copier(0))

    full = not clear
    n_steps = pl.num_programs(0)
    nxt = jnp.minimum(tile + 1, tile0 + n_steps - 1)

    pos = _tile_positions(idx_ref, rank_ref, tsrc_ref, tile).astype(F32)
    w = w_ref[...]
    pos_c = [_row_to_col(pos[k:k + 1, :]) for k in range(TOP_K)]
    w_c = [_row_to_col(w[k:k + 1, :]) for k in range(TOP_K)]
    _wait_pieces(tile, npc_ref, copier(slot))

    def chunk_sum(c):
        p_iota = (c * POS_CHUNK + lax.broadcasted_iota(I32, (tm, POS_CHUNK), 1)).astype(F32)
        wmat = jnp.where(pos_c[0] == p_iota, w_c[0], 0.0)
        for k in range(1, TOP_K):
            wmat = wmat + jnp.where(pos_c[k] == p_iota, w_c[k], 0.0)
        return _dot(wmat, buf[slot, c * POS_CHUNK:(c + 1) * POS_CHUNK, :])

    always = tm * TOP_K // POS_CHUNK
    acc = None
    for c in range(always):
        if full:
            for j in range(c * CHUNK_PIECES, (c + 1) * CHUNK_PIECES):
                _start_piece(nxt, j, pdst_ref, copier(1 - slot))
        part = chunk_sum(c)
        acc = part if acc is None else acc + part
    if full:
        _start_pieces(nxt, pdst_ref, npc_ref, copier(1 - slot), first=FULL_TILE_PIECES)

        @pl.when(i == n_steps - 1)
        def _():
            _wait_pieces(nxt, npc_ref, copier(1 - slot))
    y_ref[...] = acc
    for c in range(always, TILE_ROWS // POS_CHUNK):
        @pl.when(npc_ref[tile] * RUN_ALIGN > c * POS_CHUNK)
        def _():
            y_ref[...] += chunk_sum(c)
    x2 = xs_ref[...] + gf_ref[0] * y_ref[...]
    y_ref[...] = _rms(x2, gfin_ref[...]) if final_norm else x2


def _combine(tok, tile0, xs, g_f, g_final, idx_all, rank_all, wt_all, out_sorted, tsrc, pdst, npc, final_norm,
             clear):
    n, d = xs.shape
    gf, gf_spec = tok.mod(g_f)
    tm = MOE_TILE
    cols = pl.BlockSpec((TOP_K, tm), lambda i, *_: (0, tile0 + i))
    return pl.pallas_call(
        functools.partial(_combine_kernel, tile0=tile0, clear=clear, final_norm=final_norm),
        grid_spec=pltpu.PrefetchScalarGridSpec(
            num_scalar_prefetch=3, grid=(tok.n_tiles,),
            in_specs=[cols, cols, cols, tok.rows(d), gf_spec, _full((1, d)), pl.BlockSpec(memory_space=pl.ANY)],
            out_specs=tok.rows(d),
            scratch_shapes=[pltpu.VMEM((2, TILE_ROWS, d), BF16), pltpu.SemaphoreType.DMA((2,))]),
        out_shape=jax.ShapeDtypeStruct((n, d), F32),
        compiler_params=_cparams("arbitrary"),
        name="combine",
    )(tsrc, pdst, npc, idx_all, rank_all, wt_all, xs, gf, g_final.reshape(1, d), out_sorted)


def _rope(x, cos2, sin2):
    half = x.shape[1] // 2
    swapped = jnp.concatenate([x[:, half:], x[:, :half]], axis=1)
    return x * cos2 + swapped * sin2


def _in1_kernel(x_ref, shk_ref, sck_ref, gk_ref, wdkv_ref, glat_ref, cos_ref, sin_ref,
                shm_ref, scm_ref, gm_ref, wdq_ref, gq_ref, wqn_ref, wqr_ref, wuk_ref,
                lat_ref, kr_ref, latb_ref, krb_ref, qa_ref, qr_ref):
    x = x_ref[...]
    cos2, sin2 = cos_ref[...], sin_ref[...]
    att_scale = (NOPE_DIM + ROPE_DIM) ** -0.5

    hk = _rms_mod(x, gk_ref[...], shk_ref[0], sck_ref[0])
    ckr = _dot_w(hk, wdkv_ref[...])
    lat = _rms(ckr[:, :KV_LORA], glat_ref[...])
    kr = _rope(ckr[:, KV_LORA:], cos2, sin2)
    lat_ref[...] = lat
    kr_ref[...] = kr
    latb_ref[...] = lat.astype(BF16)
    krb_ref[...] = kr.astype(BF16)

    hm = _rms_mod(x, gm_ref[...], shm_ref[0], scm_ref[0])
    q_lat = _rms(_dot_w(hm, wdq_ref[...]), gq_ref[...])
    q_nope = _dot_w(q_lat, wqn_ref[...])
    q_rope = _dot_w(q_lat, wqr_ref[...])
    for h in range(MLA_HEADS):
        qa = _dot_w(q_nope[:, h * NOPE_DIM:(h + 1) * NOPE_DIM], wuk_ref[h])
        qa_ref[h] = (qa * att_scale).astype(BF16)
        qr = _rope(q_rope[:, h * ROPE_DIM:(h + 1) * ROPE_DIM], cos2, sin2)
        qr_ref[h] = (qr * att_scale).astype(BF16)


def _in1(tok, x, sh_kv, sc_kv, g_kv_in, w_dkv, g_kv_lat, cos2, sin2, sh_m, sc_m, g_mix, w_dq, g_q_lat,
         w_q_nope, w_q_rope, w_uk_t):
    n, d = x.shape
    q_lora = w_dq.shape[1]
    shk, shk_spec = tok.mod(sh_kv)
    sck, sck_spec = tok.mod(sc_kv)
    shm, shm_spec = tok.mod(sh_m)
    scm, scm_spec = tok.mod(sc_m)
    heads = lambda w: pl.BlockSpec((MLA_HEADS, tok.tm, w), lambda i: (0, i, 0))
    return pl.pallas_call(
        _in1_kernel,
        grid=(tok.n_tiles,),
        in_specs=[tok.rows(d), shk_spec, sck_spec, _full((1, d)), _full(w_dkv.shape), _full((1, KV_LORA)),
                  tok.seq_rows(ROPE_DIM), tok.seq_rows(ROPE_DIM),
                  shm_spec, scm_spec, _full((1, d)), _full(w_dq.shape), _full((1, q_lora)),
                  _full(w_q_nope.shape), _full(w_q_rope.shape), _full(w_uk_t.shape)],
        out_specs=[tok.rows(KV_LORA), tok.rows(ROPE_DIM), tok.rows(KV_LORA), tok.rows(ROPE_DIM),
                   heads(KV_LORA), heads(ROPE_DIM)],
        out_shape=[jax.ShapeDtypeStruct((n, KV_LORA), F32), jax.ShapeDtypeStruct((n, ROPE_DIM), F32),
                   jax.ShapeDtypeStruct((n, KV_LORA), BF16), jax.ShapeDtypeStruct((n, ROPE_DIM), BF16),
                   jax.ShapeDtypeStruct((MLA_HEADS, n, KV_LORA), BF16),
                   jax.ShapeDtypeStruct((MLA_HEADS, n, ROPE_DIM), BF16)],
        compiler_params=_cparams("arbitrary"),
        name="in1",
    )(x, shk, sck, g_kv_in.reshape(1, d), w_dkv, g_kv_lat.reshape(1, KV_LORA), cos2, sin2,
      shm, scm, g_mix.reshape(1, d), w_dq, g_q_lat.reshape(1, q_lora), w_q_nope, w_q_rope, w_uk_t)


def _attn_prompt_kernel(qa_ref, qr_ref, lat_ref, kr_ref, wuv_ref, o_ref, m_s, l_s, acc_s):
    heads, tq, c = qa_ref.shape
    qi = pl.program_id(1)
    n_lt = tq // LANES
    m_s[...] = jnp.full(m_s.shape, NEG_INF, F32)
    l_s[...] = jnp.zeros(l_s.shape, F32)
    acc_s[...] = jnp.zeros(acc_s.shape, F32)
    causal = lax.broadcasted_iota(I32, (tq, tq), 1) <= lax.broadcasted_iota(I32, (tq, tq), 0)

    def chunk(j, diagonal):
        start = pl.multiple_of(j * tq, tq)
        kc = lat_ref[pl.ds(start, tq), :]
        krc = kr_ref[pl.ds(start, tq), :]
        def scores(h):
            return _dot_nt(qa_ref[h], kc) + _dot_nt(qr_ref[h], krc)

        s_next = scores(0)
        for h in range(heads):
            s, s_next = s_next, (scores(h + 1) if h + 1 < heads else None)
            if diagonal:
                s = jnp.where(causal, s, NEG_INF)
            tiles = [s[:, t * LANES:(t + 1) * LANES] for t in range(n_lt)]
            mx = tiles[0]
            for t in tiles[1:]:
                mx = jnp.maximum(mx, t)
            m_old = m_s[h]
            m_new = jnp.maximum(m_old, jnp.broadcast_to(jnp.max(mx, axis=1, keepdims=True), (tq, LANES)))
            alpha = jnp.exp(m_old - m_new)
            ps = [jnp.exp(t - m_new) for t in tiles]
            part = ps[0]
            for p in ps[1:]:
                part = part + p
            l_s[h] = alpha * l_s[h] + part
            pv = _dot(jnp.concatenate(ps, axis=1), kc)
            acc_s[h] = jnp.concatenate([alpha] * (c // LANES), axis=1) * acc_s[h] + pv
            m_s[h] = m_new

    def step(j, carry):
        chunk(j, False)
        return carry

    lax.fori_loop(0, qi, step, 0)
    chunk(qi, True)
    for h in range(heads):
        o_lat = acc_s[h] / jnp.sum(l_s[h], axis=1, keepdims=True)
        o_ref[:, h * V_DIM:(h + 1) * V_DIM] = _dot(o_lat, wuv_ref[h]).astype(o_ref.dtype)


def _attn_prompt(qa, qr, latb, krb, w_uv_t, batch, seq, tq=256):
    tq = min(tq, seq)
    nq = seq // tq
    return pl.pallas_call(
        _attn_prompt_kernel,
        grid=(batch, nq),
        in_specs=[pl.BlockSpec((MLA_HEADS, tq, KV_LORA), lambda b, i: (0, b * nq + i, 0)),
                  pl.BlockSpec((MLA_HEADS, tq, ROPE_DIM), lambda b, i: (0, b * nq + i, 0)),
                  pl.BlockSpec((seq, KV_LORA), lambda b, i: (b, 0)),
                  pl.BlockSpec((seq, ROPE_DIM), lambda b, i: (b, 0)),
                  pl.BlockSpec(w_uv_t.shape, lambda b, i: (0, 0, 0))],
        out_specs=pl.BlockSpec((tq, MLA_HEADS * V_DIM), lambda b, i: (b * nq + i, 0)),
        out_shape=jax.ShapeDtypeStruct((batch * seq, MLA_HEADS * V_DIM), BF16),
        scratch_shapes=[pltpu.VMEM((MLA_HEADS, tq, LANES), F32), pltpu.VMEM((MLA_HEADS, tq, LANES), F32),
                        pltpu.VMEM((MLA_HEADS, tq, KV_LORA), F32)],
        compiler_params=_cparams("arbitrary", "arbitrary"),
        name="attn_prompt",
    )(qa, qr, latb, krb, w_uv_t)


def _attn_decode_kernel(pt_ref, qa_ref, qr_ref, lat1_ref, kr1_ref, wuv_ref, *rest, pages):
    lat_refs = rest[:pages]
    kr_refs = rest[pages:2 * pages]
    o_ref, m_s, l_s, acc_s = rest[2 * pages:]
    j = pl.program_id(1)
    qa = qa_ref[0]
    qr = qr_ref[0]

    @pl.when(j == 0)
    def _():
        lat1 = lat1_ref[0].astype(F32)
        m_s[...] = (jnp.sum(qa.astype(F32) * lat1, axis=1, keepdims=True)
                    + jnp.sum(qr.astype(F32) * kr1_ref[0].astype(F32), axis=1, keepdims=True))
        l_s[...] = jnp.ones(l_s.shape, F32)
        acc_s[...] = jnp.broadcast_to(lat1, acc_s.shape)

    lats = [r[0].astype(BF16) for r in lat_refs]
    s = jnp.concatenate([_dot_nt(qa, lats[p]) + _dot(qr, kr_refs[p][0]) for p in range(pages)], axis=1)
    m_old = m_s[...]
    m_new = jnp.maximum(m_old, jnp.max(s, axis=1, keepdims=True))
    alpha = jnp.exp(m_old - m_new)
    p_all = jnp.exp(s - m_new)
    l_s[...] = alpha * l_s[...] + jnp.sum(p_all, axis=1, keepdims=True)
    page = lats[0].shape[0]
    pv = _dot(p_all[:, :page], lats[0])
    for p in range(1, pages):
        pv = pv + _dot(p_all[:, p * page:(p + 1) * page], lats[p])
    acc_s[...] = alpha * acc_s[...] + pv
    m_s[...] = m_new

    @pl.when(j == pl.num_programs(1) - 1)
    def _():
        o_all = _dot(acc_s[...] / l_s[...], wuv_ref[...])
        for h in range(o_all.shape[0]):
            o_ref[0, :, h * V_DIM:(h + 1) * V_DIM] = o_all[h:h + 1, h * V_DIM:(h + 1) * V_DIM].astype(o_ref.dtype)


def _attn_decode(qa, qr, lat1, kr1, w_uv_flat, cache_latent, cache_krope_t, page_table, pages=64):
    batch, n_pages = page_table.shape
    pages = min(pages, n_pages)
    page = cache_latent.shape[1]
    steps = n_pages // pages
    one = lambda shape: pl.BlockSpec((1,) + shape, lambda b, j, pt: (b,) + (0,) * len(shape))
    paged = lambda w, p: pl.BlockSpec((1, page, w) if w == KV_LORA else (1, w, page),
                                      lambda b, j, pt: (pt[b, j * pages + p], 0, 0))
    grid_spec = pltpu.PrefetchScalarGridSpec(
        num_scalar_prefetch=1,
        grid=(batch, steps),
        in_specs=[one((MLA_HEADS, KV_LORA)), one((MLA_HEADS, ROPE_DIM)), one((1, KV_LORA)), one((1, ROPE_DIM)),
                  pl.BlockSpec(w_uv_flat.shape, lambda b, j, pt: (0, 0))]
                 + [paged(KV_LORA, p) for p in range(pages)] + [paged(ROPE_DIM, p) for p in range(pages)],
        out_specs=one((1, MLA_HEADS * V_DIM)),
        scratch_shapes=[pltpu.VMEM((MLA_HEADS, 1), F32), pltpu.VMEM((MLA_HEADS, 1), F32),
                        pltpu.VMEM((MLA_HEADS, KV_LORA), F32)],
    )
    return pl.pallas_call(
        functools.partial(_attn_decode_kernel, pages=pages),
        grid_spec=grid_spec,
        out_shape=jax.ShapeDtypeStruct((batch, 1, MLA_HEADS * V_DIM), BF16),
        compiler_params=_cparams("arbitrary", "arbitrary"),
        name="attn_decode",
    )(page_table, qa, qr, lat1, kr1, w_uv_flat, *([cache_latent] * pages), *([cache_krope_t] * pages))


def _rope_tables(pos):
    half = ROPE_DIM // 2
    inv = ROPE_THETA ** (-jnp.arange(half, dtype=F32) / half)
    ang = pos.astype(F32)[:, None] * inv[None, :]
    cos, sin = jnp.cos(ang), jnp.sin(ang)
    return jnp.concatenate([cos, cos], axis=1), jnp.concatenate([-sin, sin], axis=1)


def _routing_tables(tile_counts, n_rows):
    bm = EXPERT_ROWS
    tile_counts = (tile_counts + RUN_ALIGN - 1) // RUN_ALIGN * RUN_ALIGN
    counts = jnp.sum(tile_counts, axis=0)
    padded = (counts + bm - 1) // bm * bm
    pad_end = jnp.cumsum(padded).astype(I32)
    pad_start = pad_end - padded
    tile_dst = pad_start[None, :] + jnp.cumsum(tile_counts, axis=0) - tile_counts
    tile_src = jnp.cumsum(tile_counts, axis=1) - tile_counts
    run_end = tile_src + tile_counts
    piece_row = jnp.arange(TILE_PIECES, dtype=I32) * RUN_ALIGN
    piece_e = jnp.sum((run_end[:, None, :] <= piece_row[None, :, None]).astype(I32), axis=2)
    owner = (piece_e[:, :, None] == jnp.arange(N_EXPERTS, dtype=I32)[None, None, :]).astype(I32)
    piece_dst = jnp.sum(owner * (tile_dst - tile_src)[:, None, :], axis=2) + piece_row[None, :]
    n_pieces = run_end[:, -1] // RUN_ALIGN
    block_start = jnp.arange(n_rows // bm, dtype=I32) * bm
    block_e = jnp.sum((pad_end[None, :] <= block_start[:, None]).astype(I32), axis=1)
    block_e = jnp.minimum(block_e, N_EXPERTS - 1)
    first = jnp.concatenate([jnp.ones((1,), I32), (block_e[1:] != block_e[:-1]).astype(I32)])
    n_used = pad_end[-1:] // bm
    return (pad_end, padded, tile_src.astype(I32), piece_dst.reshape(-1).astype(I32), n_pieces.astype(I32),
            block_e, first, n_used)


def kernel(x_prompt, x_sample, state_mlstm_C, state_mlstm_n, state_mlstm_m, cache_latent, cache_krope, page_table, c_prompt, c_sample, w_ada, b_ada, g_mix, g_ffn, w_mlstm_in, b_mlstm_gates, g_mlstm_head, w_mlstm_out, w_ada_kv, b_ada_kv, g_kv_in, w_dkv, g_kv_lat, w_uk, w_uv, w_dq, g_q_lat, w_uq, w_mla_out, w_router, b_router, w_exp_gate, w_exp_up, w_exp_down, w_sh_gate, w_sh_up, w_sh_down, g_final):
    bp, tp, d = x_prompt.shape
    bs, ts, _ = x_sample.shape
    depth = w_ada.shape[0]
    n_a = w_mlstm_in.shape[0]
    past_len = page_table.shape[1] * cache_latent.shape[1]
    hk, hv = ML_HEADS * ML_DK, ML_HEADS * ML_DV
    n_p, n_s = bp * tp, bs * ts
    n_tot = n_p + n_s
    n_moe_tiles = n_p // MOE_TILE + 1
    n_rows = n_tot * TOP_K + n_moe_tiles * N_EXPERTS * (RUN_ALIGN - 1) + N_EXPERTS * (EXPERT_ROWS - 1)
    n_rows = -(-n_rows // EXPERT_ROWS) * EXPERT_ROWS

    groups = [
        dict(batch=bp, seq=tp, tok=_Tok(bp, tp, 256), x=x_prompt.reshape(n_p, d), pos0=0,
             C0=jnp.zeros((n_a, bp, ML_HEADS, ML_DK, ML_DV), F32), n0=jnp.zeros((n_a, bp, ML_HEADS, ML_DK), F32),
             m0=jnp.full((n_a, bp, ML_HEADS), M_INIT, F32)),
        dict(batch=bs, seq=ts, tok=_Tok(bs, ts, 256), x=x_sample.reshape(n_s, d), pos0=past_len,
             C0=state_mlstm_C, n0=state_mlstm_n, m0=state_mlstm_m),
    ]
    c_all = jnp.concatenate([c_prompt, c_sample], axis=0)
    row0 = [0, bp]
    for g in groups:
        g["Cs"], g["ns"], g["ms"] = [], [], []

    for g in groups:
        g["cos2"], g["sin2"] = _rope_tables(g["pos0"] + jnp.arange(g["seq"], dtype=I32))
        if g["seq"] == 1:
            g["cos2"] = jnp.broadcast_to(g["cos2"], (g["batch"], ROPE_DIM))
            g["sin2"] = jnp.broadcast_to(g["sin2"], (g["batch"], ROPE_DIM))

    mod_kv = _ada(c_all, w_ada_kv, b_ada_kv)
    w_uk_t = jnp.transpose(w_uk, (1, 2, 0))
    w_uv_t = jnp.transpose(w_uv, (1, 0, 2)).astype(BF16)

    for l in range(depth):
        mod = _ada(c_all, w_ada, b_ada[l], layer=l)
        w_router_t = w_router[l].T
        ws_gu = jnp.concatenate([w_sh_gate[l], w_sh_up[l]], axis=1).astype(BF16)
        ws_d = w_sh_down[l].astype(BF16)
        for gi, g in enumerate(groups):
            r0, nb = row0[gi], g["batch"]
            sh_m, sc_m, g_m, sh_f, sc_f, g_f = [mod[r0:r0 + nb, i * d:(i + 1) * d] for i in range(6)]
            g["g_f"] = g_f
            tok = g["tok"]
            wdt = F32 if g["seq"] == 1 else BF16
            if l < n_a:
                w_in = w_mlstm_in[l]
                w_main = w_in[:, :2 * hk + 2 * hv].astype(wdt)
                w_gates = jnp.pad(w_in[:, 2 * hk + 2 * hv:], ((0, 0), (0, 128 - 2 * ML_HEADS)))
                b_gates = jnp.pad(b_mlstm_gates[l], (0, 128 - 2 * ML_HEADS)).reshape(1, 128)
                if g["seq"] == 1:
                    proj, gates = _in0(tok, g["x"], sh_m, sc_m, g_mix[l], w_main, w_gates, b_gates, F32)
                    a, Cn, nn, mn = _mlstm_step(proj, gates, g["C0"][l], g["n0"][l], g["m0"][l],
                                                g_mlstm_head[l], nb)
                    a = a.reshape(nb, hv)
                else:
                    proj, gates = _in0(tok, g["x"], sh_m, sc_m, g_mix[l], w_main, w_gates, b_gates, BF16)
                    a, Cn, nn, mn = _mlstm_chunks(proj, gates, g["C0"][l], g["n0"][l], g["m0"][l],
                                                  g_mlstm_head[l], nb, g["seq"])
                g["Cs"].append(Cn)
                g["ns"].append(nn)
                g["ms"].append(mn.reshape(nb, ML_HEADS))
                w_o = w_mlstm_out[l].astype(wdt)
            else:
                j = l - n_a
                w_q = w_uq[j].reshape(-1, MLA_HEADS, NOPE_DIM + ROPE_DIM)
                w_q_nope = w_q[:, :, :NOPE_DIM].reshape(-1, MLA_HEADS * NOPE_DIM).astype(wdt)
                w_q_rope = w_q[:, :, NOPE_DIM:].reshape(-1, MLA_HEADS * ROPE_DIM).astype(wdt)
                sh_kv, sc_kv = mod_kv[r0:r0 + nb, :d], mod_kv[r0:r0 + nb, d:]
                lat, kr, latb, krb, qa, qr = _in1(tok, g["x"], sh_kv, sc_kv, g_kv_in, w_dkv.astype(wdt), g_kv_lat,
                                                  g["cos2"], g["sin2"], sh_m, sc_m, g_mix[l], w_dq[j].astype(wdt),
                                                  g_q_lat[j], w_q_nope, w_q_rope, w_uk_t.astype(wdt))
                if l == n_a:
                    g["lat"], g["kr"], g["latb"], g["krb"] = lat, kr, latb, krb
                if g["seq"] == 1:
                    a = _attn_decode(jnp.transpose(qa, (1, 0, 2)), jnp.transpose(qr, (1, 0, 2)),
                                     g["latb"].reshape(nb, 1, KV_LORA), g["krb"].reshape(nb, 1, ROPE_DIM),
                                     w_uv.reshape(KV_LORA, MLA_HEADS * V_DIM).astype(BF16),
                                     cache_latent, jnp.swapaxes(cache_krope, 1, 2),
                                     page_table).reshape(nb, MLA_HEADS * V_DIM)
                else:
                    a = _attn_prompt(qa, qr, g["latb"], g["krb"], w_uv_t, nb, g["seq"])
                w_o = w_mla_out[j].astype(wdt)
            g["xs"], g["hp"], g["idx"], g["wt"], g["rank"], g["cnt"] = _post(
                tok, g["x"], a, w_o, g_m, g_ffn[l], sh_f, sc_f, g_f, w_router_t.astype(wdt), b_router[l], ws_gu, ws_d)

        gp, gs = groups
        pad_t = MOE_TILE - n_s
        lanes = lambda v, fill: jnp.concatenate(
            [gp[v], jnp.pad(gs[v], ((0, 0), (0, pad_t)), constant_values=fill)], axis=1)
        idx_all, wt_all = lanes("idx", 0), lanes("wt", 0)
        rank_all = lanes("rank", TILE_ROWS)
        tile_counts = jnp.concatenate([gp["cnt"][:, :, 0], gs["cnt"][:, :, 0]], axis=0).astype(I32)
        pad_end, padded, tsrc, pdst, npc, block_e, first, n_used = _routing_tables(tile_counts, n_rows)
        rows = lambda v: jnp.pad(v, ((0, pad_t), (0, 0)))
        xs_sorted = _dispatch(gp["hp"], rows(gs["hp"]), idx_all, rank_all, pad_end, padded, tsrc, pdst, npc, n_rows)
        out_sorted = _experts(xs_sorted, block_e, first, n_used, w_exp_gate, w_exp_up, w_exp_down, l)
        last = l == depth - 1
        gp["x"] = _combine(gp["tok"], 0, gp["xs"], gp["g_f"], g_final, idx_all, rank_all, wt_all, out_sorted,
                           tsrc, pdst, npc, last, clear=False)
        gs["x"] = _combine(_Tok(MOE_TILE, 1, MOE_TILE), gp["tok"].n_tiles, rows(gs["xs"]), rows(gs["g_f"]), g_final,
                           idx_all, rank_all, wt_all, out_sorted, tsrc, pdst, npc, last, clear=True)[:n_s]

    outs = []
    for g in groups:
        nb, seq = g["batch"], g["seq"]
        outs.append((g["x"].reshape(nb, seq, d), jnp.stack(g["Cs"]), jnp.stack(g["ns"]), jnp.stack(g["ms"]),
                     g["lat"].reshape(nb, seq, KV_LORA), g["kr"].reshape(nb, seq, ROPE_DIM)))
    p, s = outs
    return (p[0], s[0], p[1], p[2], p[3], p[4], p[5], s[1], s[2], s[3], s[4], s[5])
```

```python
import functools

import jax
import jax.numpy as jnp
from jax import lax
from jax.experimental import pallas as pl
from jax.experimental.pallas import tpu as pltpu

F32 = jnp.float32
BF16 = jnp.bfloat16
I32 = jnp.int32
HIGHEST = lax.Precision.HIGHEST
NEG_INF = float("-inf")

ML_HEADS = 4
ML_DK = 128
ML_DV = 256
ML_CHUNK = 128
M_INIT = -1e30
MLA_HEADS = 8
NOPE_DIM = 128
ROPE_DIM = 64
V_DIM = 128
KV_LORA = 256
ROPE_THETA = 10000.0
N_EXPERTS = 64
TOP_K = 8
N_GROUPS = 8
TOPK_GROUPS = 4
ROUTED_SCALE = 2.5
NORM_EPS = 1e-6

LANES = 128
EXPERT_ROWS = 1024
VMEM_LIMIT = 56 * 1024 * 1024


def _cparams(*sem):
    return pltpu.CompilerParams(dimension_semantics=sem, vmem_limit_bytes=VMEM_LIMIT)


def _dot(a, b):
    return jnp.dot(a.astype(BF16), b.astype(BF16), preferred_element_type=F32)


def _dot_nt(a, b):
    return lax.dot_general(a.astype(BF16), b.astype(BF16), (((1,), (1,)), ((), ())),
                           preferred_element_type=F32)


def _dot_f32(a, b):
    return jnp.dot(a, b, precision=HIGHEST, preferred_element_type=F32)


def _dot_w(a, w):
    return _dot_f32(a.astype(F32), w) if w.dtype == F32 else _dot(a, w)


def _dot_w_nt(w, a):
    if w.dtype == F32:
        return lax.dot_general(w, a.astype(F32), (((1,), (1,)), ((), ())), precision=HIGHEST,
                               preferred_element_type=F32)
    return _dot_nt(w, a)


def _rms(x, g):
    return x * lax.rsqrt(jnp.mean(x * x, axis=-1, keepdims=True) + NORM_EPS) * g


def _rms_mod(x, g, shift, scale):
    return _rms(x, g) * (1.0 + scale) + shift


def _sigmoid(x):
    return 1.0 / (1.0 + jnp.exp(-x))


def _silu(x):
    return x * _sigmoid(x)


def _log_sigmoid(x):
    return jnp.minimum(x, 0.0) - jnp.log1p(jnp.exp(-jnp.abs(x)))


def _ada_kernel(c_ref, w_ref, b_ref, o_ref):
    o_ref[...] = _dot_f32(_silu(c_ref[...]), w_ref[...]) + b_ref[...]


def _ada(c, w, b, layer=None, tn=1024):
    m, d = c.shape
    n_out = w.shape[-1]
    if layer is None:
        w_spec = pl.BlockSpec((d, tn), lambda j: (0, j))
    else:
        w_spec = pl.BlockSpec((None, d, tn), lambda j: (layer, 0, j))
    return pl.pallas_call(
        _ada_kernel,
        grid=(n_out // tn,),
        in_specs=[pl.BlockSpec((m, d), lambda j: (0, 0)), w_spec, pl.BlockSpec((1, tn), lambda j: (0, j))],
        out_specs=pl.BlockSpec((m, tn), lambda j: (0, j)),
        out_shape=jax.ShapeDtypeStruct((m, n_out), F32),
        compiler_params=_cparams("arbitrary"),
        name="ada",
    )(c, w, b.reshape(1, n_out))


class _Tok:
    def __init__(self, batch, seq, tile):
        if seq == 1:
            self.tm, self.n_tiles, self.per_seq = batch, 1, 1
            self.mod_block = (1, batch, None)
        else:
            self.tm = min(tile, seq)
            self.per_seq = seq // self.tm
            self.n_tiles = batch * self.per_seq
            self.mod_block = (1, 1, None)
        self.batch, self.seq = batch, seq

    def mod(self, v):
        d = v.shape[-1]
        if self.seq == 1:
            return v.reshape(1, self.batch, d), pl.BlockSpec((1, self.batch, d), lambda i, *_: (0, 0, 0))
        per = self.per_seq
        return v.reshape(self.batch, 1, d), pl.BlockSpec((1, 1, d), lambda i, *_: (i // per, 0, 0))

    def rows(self, width):
        return pl.BlockSpec((self.tm, width), lambda i, *_: (i, 0))

    def seq_rows(self, width):
        per = self.per_seq
        return pl.BlockSpec((self.tm, width), lambda i, *_: (i % per, 0))


def _full(shape):
    nd = len(shape)
    return pl.BlockSpec(shape, lambda i, *_: (0,) * nd)


def _in0_kernel(x_ref, sh_ref, sc_ref, g_ref, w_ref, wg_ref, bg_ref, o_ref, gt_ref):
    h = _rms_mod(x_ref[...], g_ref[...], sh_ref[0], sc_ref[0])
    o_ref[...] = _dot_w(h, w_ref[...]).astype(o_ref.dtype)
    gt_ref[...] = _dot_f32(h, wg_ref[...]) + bg_ref[...]


def _in0(tok, x, shift, scale, g, w, w_gates, b_gates, out_dtype):
    n, d = x.shape
    wo = w.shape[1]
    sh, sh_spec = tok.mod(shift)
    sc, sc_spec = tok.mod(scale)
    return pl.pallas_call(
        _in0_kernel,
        grid=(tok.n_tiles,),
        in_specs=[tok.rows(d), sh_spec, sc_spec, _full((1, d)), _full((d, wo)), _full((d, 128)), _full((1, 128))],
        out_specs=[tok.rows(wo), tok.rows(128)],
        out_shape=[jax.ShapeDtypeStruct((n, wo), out_dtype), jax.ShapeDtypeStruct((n, 128), F32)],
        compiler_params=_cparams("arbitrary"),
        name="in0",
    )(x, sh, sc, g.reshape(1, d), w, w_gates, b_gates)


def _mlstm_chunk_kernel(q_ref, k_ref, v_ref, og_ref, gt_ref, c0_ref, n0_ref, m0_ref, gh_ref,
                        hh_ref, c_ref, n_ref, m_ref):
    L = q_ref.shape[0]
    scale = ML_DK ** -0.5

    @pl.when(pl.program_id(1) == 0)
    def _():
        c_ref[...] = c0_ref[...]
        n_ref[...] = n0_ref[...]
        m_ref[...] = m0_ref[...]

    gates = gt_ref[...]
    logf = _log_sigmoid(gates)
    row = lax.broadcasted_iota(I32, (L, L), 0)
    col = lax.broadcasted_iota(I32, (L, L), 1)
    causal = col <= row
    tril = causal.astype(F32)
    b_cols = _dot_f32(tril, logf)
    gates_t = gates.T
    b_rows = _dot_f32(logf.T, (row <= col).astype(F32))

    for h in range(ML_HEADS):
        i_row = gates_t[h:h + 1, :]
        i_col = gates[:, h:h + 1]
        b_row = b_rows[ML_HEADS + h:ML_HEADS + h + 1, :]
        b_col = b_cols[:, ML_HEADS + h:ML_HEADS + h + 1]
        m_prev = m_ref[0, :, h:h + 1]
        c_prev = c_ref[0, h]
        n_prev = n_ref[0, h:h + 1, :]
        q = q_ref[:, h * ML_DK:(h + 1) * ML_DK]
        k = k_ref[:, h * ML_DK:(h + 1) * ML_DK]
        v = v_ref[:, h * ML_DV:(h + 1) * ML_DV]

        g_col = b_col + m_prev
        dmat = jnp.where(causal, b_col - b_row + i_row, NEG_INF)
        m_t = jnp.maximum(g_col, jnp.max(dmat, axis=1, keepdims=True))
        w_intra = jnp.exp(dmat - m_t)
        w_inter = jnp.exp(g_col - m_t)
        s = _dot_nt(q, k) * (w_intra * scale)
        num = _dot(s, v) + w_inter * _dot(q, c_prev)
        qn = jnp.sum(q.astype(F32) * n_prev, axis=1, keepdims=True)
        den = jnp.sum(s, axis=1, keepdims=True) + w_inter * qn
        cell = num / jnp.maximum(jnp.abs(den), jnp.exp(-m_t))

        b_last = b_col[L - 1:L, :]
        m_new = m_t[L - 1:L, :]
        a_inter = jnp.exp(b_last + m_prev - m_new)
        a_intra = jnp.exp(b_last - b_col + i_col - m_new) * scale
        kw = k.astype(F32) * a_intra
        c_ref[0, h] = a_inter * c_prev + _dot(kw.T, v)
        n_ref[0, h:h + 1, :] = a_inter * n_prev + jnp.sum(kw, axis=0, keepdims=True)
        m_ref[0, :, h:h + 1] = m_new

        y = _rms(cell, gh_ref[:, h * ML_DV:(h + 1) * ML_DV])
        og = og_ref[:, h * ML_DV:(h + 1) * ML_DV].astype(F32)
        hh_ref[:, h * ML_DV:(h + 1) * ML_DV] = (y * _sigmoid(og)).astype(hh_ref.dtype)


def _mlstm_chunks(proj, gates, c0, n0, m0, g_head, batch, seq):
    L = seq if seq <= ML_CHUNK else ML_CHUNK
    nc = seq // L
    hk, hv = ML_HEADS * ML_DK, ML_HEADS * ML_DV
    rows = lambda w, cb: pl.BlockSpec((L, w), lambda b, j: (b * nc + j, cb))
    state = lambda shape: pl.BlockSpec((1,) + shape, lambda b, j: (b,) + (0,) * len(shape))
    return pl.pallas_call(
        _mlstm_chunk_kernel,
        grid=(batch, nc),
        in_specs=[rows(hk, 0), rows(hk, 1), rows(hv, 2 * hk // hv), rows(hv, 2 * hk // hv + 1), rows(128, 0),
                  state((ML_HEADS, ML_DK, ML_DV)), state((ML_HEADS, ML_DK)), state((1, ML_HEADS)),
                  pl.BlockSpec((1, hv), lambda b, j: (0, 0))],
        out_specs=[rows(hv, 0), state((ML_HEADS, ML_DK, ML_DV)), state((ML_HEADS, ML_DK)), state((1, ML_HEADS))],
        out_shape=[jax.ShapeDtypeStruct((batch * seq, hv), BF16),
                   jax.ShapeDtypeStruct((batch, ML_HEADS, ML_DK, ML_DV), F32),
                   jax.ShapeDtypeStruct((batch, ML_HEADS, ML_DK), F32),
                   jax.ShapeDtypeStruct((batch, 1, ML_HEADS), F32)],
        compiler_params=_cparams("arbitrary", "arbitrary"),
        name="mlstm_chunks",
    )(proj, proj, proj, proj, gates, c0, n0, m0.reshape(batch, 1, ML_HEADS), g_head.reshape(1, hv))


def _row_to_col(r):
    n = r.shape[1]
    eye = lax.broadcasted_iota(I32, (n, n), 0) == lax.broadcasted_iota(I32, (n, n), 1)
    return jnp.sum(jnp.where(eye, jnp.broadcast_to(r, (n, n)), 0.0), axis=1, keepdims=True)


def _mlstm_step_kernel(p_ref, gt_ref, c0_ref, n0_ref, m0_ref, gh_ref, hh_ref, c_ref, n_ref, m_ref):
    scale = ML_DK ** -0.5
    hk, hv = ML_HEADS * ML_DK, ML_HEADS * ML_DV
    gates = gt_ref[0]
    logf = _log_sigmoid(gates)
    for h in range(ML_HEADS):
        q = p_ref[0, :, h * ML_DK:(h + 1) * ML_DK].astype(F32)
        k = p_ref[0, :, hk + h * ML_DK:hk + (h + 1) * ML_DK].astype(F32)
        v = p_ref[0, :, 2 * hk + h * ML_DV:2 * hk + (h + 1) * ML_DV].astype(F32)
        og = p_ref[0, :, 2 * hk + hv + h * ML_DV:2 * hk + hv + (h + 1) * ML_DV].astype(F32)
        log_i = gates[:, h:h + 1]
        m_prev = m0_ref[0, :, h:h + 1]
        c_prev = c0_ref[0, h]
        n_prev = n0_ref[0, h:h + 1, :]

        g = logf[:, ML_HEADS + h:ML_HEADS + h + 1] + m_prev
        m_t = jnp.maximum(g, log_i)
        w_intra = jnp.exp(log_i - m_t)
        w_inter = jnp.exp(g - m_t)
        s = jnp.sum(q * k, axis=1, keepdims=True) * (w_intra * scale)
        q_c = _row_to_col(q)
        num = s * v + w_inter * jnp.sum(q_c * c_prev, axis=0, keepdims=True)
        den = s + w_inter * jnp.sum(q * n_prev, axis=1, keepdims=True)
        cell = num / jnp.maximum(jnp.abs(den), jnp.exp(-m_t))

        kw = k * (w_intra * scale)
        c_ref[0, h] = w_inter * c_prev + _row_to_col(kw) * v
        n_ref[0, h:h + 1, :] = w_inter * n_prev + kw
        m_ref[0, :, h:h + 1] = m_t

        y = _rms(cell, gh_ref[:, h * ML_DV:(h + 1) * ML_DV])
        hh_ref[0, :, h * ML_DV:(h + 1) * ML_DV] = (y * _sigmoid(og)).astype(hh_ref.dtype)


def _mlstm_step(proj, gates, c0, n0, m0, g_head, batch):
    hv = ML_HEADS * ML_DV
    wp = proj.shape[1]
    one = lambda shape: pl.BlockSpec((1,) + shape, lambda b: (b,) + (0,) * len(shape))
    return pl.pallas_call(
        _mlstm_step_kernel,
        grid=(batch,),
        in_specs=[one((1, wp)), one((1, 128)), one((ML_HEADS, ML_DK, ML_DV)), one((ML_HEADS, ML_DK)),
                  one((1, ML_HEADS)), pl.BlockSpec((1, hv), lambda b: (0, 0))],
        out_specs=[one((1, hv)), one((ML_HEADS, ML_DK, ML_DV)), one((ML_HEADS, ML_DK)), one((1, ML_HEADS))],
        out_shape=[jax.ShapeDtypeStruct((batch, 1, hv), F32),
                   jax.ShapeDtypeStruct((batch, ML_HEADS, ML_DK, ML_DV), F32),
                   jax.ShapeDtypeStruct((batch, ML_HEADS, ML_DK), F32),
                   jax.ShapeDtypeStruct((batch, 1, ML_HEADS), F32)],
        compiler_params=_cparams("arbitrary"),
        name="mlstm_step",
    )(proj.reshape(batch, 1, wp), gates.reshape(batch, 1, 128), c0, n0, m0.reshape(batch, 1, ML_HEADS),
      g_head.reshape(1, hv))


def _group_reduce(x, op):
    e, tm = x.shape
    per = e // N_GROUPS
    x3 = x.reshape(N_GROUPS, per, tm)
    r = op(x3, axis=1, keepdims=True)
    return jnp.broadcast_to(r, (N_GROUPS, per, tm)).reshape(e, tm)


def _route(logits_t, bias_t):
    e, tm = logits_t.shape
    per = e // N_GROUPS
    scores = _sigmoid(logits_t)
    sel = scores + bias_t
    eidx = lax.broadcasted_iota(I32, (e, tm), 0)
    jidx = eidx % per
    gidx = eidx // per
    m1 = _group_reduce(sel, jnp.max)
    first1 = _group_reduce(jnp.where(sel == m1, jidx, per), jnp.min)
    m2 = _group_reduce(jnp.where(jidx == first1, NEG_INF, sel), jnp.max)
    gs = m1 + m2
    chosen = jnp.zeros((e, tm), F32)
    for _ in range(TOPK_GROUPS):
        mx = jnp.max(gs, axis=0, keepdims=True)
        f = jnp.min(jnp.where(gs == mx, gidx, N_GROUPS), axis=0, keepdims=True)
        hit = gidx == f
        chosen = jnp.where(hit, 1.0, chosen)
        gs = jnp.where(hit, NEG_INF, gs)
    cand = jnp.where(chosen > 0.0, sel, NEG_INF)
    picked = jnp.zeros((e, tm), F32)
    idx, wts = [], []
    for _ in range(TOP_K):
        mx = jnp.max(cand, axis=0, keepdims=True)
        f = jnp.min(jnp.where(cand == mx, eidx, e), axis=0, keepdims=True)
        hit = eidx == f
        idx.append(f)
        wts.append(jnp.sum(jnp.where(hit, scores, 0.0), axis=0, keepdims=True))
        cand = jnp.where(hit, NEG_INF, cand)
        picked = jnp.where(hit, 1.0, picked)
    return idx, wts, picked


def _post_kernel(x_ref, a_ref, wo_ref, gm_ref, gn_ref, sh_ref, sc_ref, gf_ref, wr_ref, br_ref,
                 wsgu_ref, wsd_ref,
                 xs_ref, hp_ref, idx_ref, wt_ref, rank_ref, cnt_ref):
    tm, d = x_ref.shape

    x1 = x_ref[...] + gm_ref[0] * _dot_w(a_ref[...], wo_ref[...])
    h2 = _rms_mod(x1, gn_ref[...], sh_ref[0], sc_ref[0])
    hp_ref[...] = h2.astype(BF16)

    ff = wsd_ref.shape[0]
    gu = _dot(h2, wsgu_ref[...])
    shared = _dot(_silu(gu[:, :ff]) * gu[:, ff:], wsd_ref[...])
    xs_ref[...] = x1 + gf_ref[0] * shared

    logits_t = _dot_w_nt(wr_ref[...], h2)
    idx, wts, picked = _route(logits_t, br_ref[:, 0:1])
    e = picked.shape[0]
    eidx = lax.broadcasted_iota(I32, (e, tm), 0)
    before = (lax.broadcasted_iota(I32, (tm, tm), 0) < lax.broadcasted_iota(I32, (tm, tm), 1)).astype(BF16)
    rank_t = jnp.dot(picked.astype(BF16), before, preferred_element_type=F32)
    total = wts[0]
    for k in range(1, TOP_K):
        total = total + wts[k]
    for k in range(TOP_K):
        idx_ref[k:k + 1, :] = idx[k]
        wt_ref[k:k + 1, :] = wts[k] / total * ROUTED_SCALE
        rank_ref[k:k + 1, :] = jnp.sum(jnp.where(eidx == idx[k], rank_t, 0.0), axis=0, keepdims=True).astype(I32)
    cnt_ref[0] = jnp.broadcast_to(jnp.sum(picked, axis=1, keepdims=True), cnt_ref.shape[1:])


def _post(tok, x, a, w_o, g_m, g_norm, shift, scale, g_f, w_router_t, b_router, ws_gu, ws_d):
    n, d = x.shape
    e = w_router_t.shape[0]
    ff = ws_d.shape[0]
    gm, gm_spec = tok.mod(g_m)
    sh, sh_spec = tok.mod(shift)
    sc, sc_spec = tok.mod(scale)
    gf, gf_spec = tok.mod(g_f)
    cols = lambda rows: pl.BlockSpec((rows, tok.tm), lambda i: (0, i))
    return pl.pallas_call(
        _post_kernel,
        grid=(tok.n_tiles,),
        in_specs=[tok.rows(d), tok.rows(a.shape[1]), _full(w_o.shape), gm_spec, _full((1, d)), sh_spec, sc_spec,
                  gf_spec, _full((e, d)), _full((e, 128)), _full((d, 2 * ff)), _full((ff, d))],
        out_specs=[tok.rows(d), tok.rows(d), cols(TOP_K), cols(TOP_K), cols(TOP_K),
                   pl.BlockSpec((1, e, LANES), lambda i: (i, 0, 0))],
        out_shape=[jax.ShapeDtypeStruct((n, d), F32), jax.ShapeDtypeStruct((n, d), BF16),
                   jax.ShapeDtypeStruct((TOP_K, n), I32), jax.ShapeDtypeStruct((TOP_K, n), F32),
                   jax.ShapeDtypeStruct((TOP_K, n), I32), jax.ShapeDtypeStruct((tok.n_tiles, e, LANES), F32)],
        compiler_params=_cparams("arbitrary"),
        name="post",
    )(x, a, w_o, gm, g_norm.reshape(1, d), sh, sc, gf, w_router_t,
      jnp.broadcast_to(b_router.reshape(e, 1), (e, 128)), ws_gu, ws_d)


MOE_TILE = 256
RUN_ALIGN = 16
POS_CHUNK = 512
TILE_ROWS = -(-(MOE_TILE * TOP_K + N_EXPERTS * (RUN_ALIGN - 1)) // POS_CHUNK) * POS_CHUNK
TILE_PIECES = TILE_ROWS // RUN_ALIGN


def _tile_positions(idx_ref, rank_ref, tsrc_ref, tile):
    idx = idx_ref[...]

    def add(e, acc):
        return acc + jnp.where(idx == e, tsrc_ref[tile, e], 0)

    return lax.fori_loop(0, N_EXPERTS, add, rank_ref[...])


def _start_piece(tile, j, pdst_ref, make_copy):
    make_copy(pl.multiple_of(j * RUN_ALIGN, RUN_ALIGN),
              pl.multiple_of(pdst_ref[tile * TILE_PIECES + j], RUN_ALIGN), RUN_ALIGN).start()


def _start_pieces(tile, pdst_ref, npc_ref, make_copy, first=0):
    def body(j, c):
        _start_piece(tile, j, pdst_ref, make_copy)
        return c

    lax.fori_loop(first, npc_ref[tile], body, 0)


FULL_TILE_PIECES = MOE_TILE * TOP_K // RUN_ALIGN
CHUNK_PIECES = POS_CHUNK // RUN_ALIGN


def _wait_pieces(tile, npc_ref, make_copy):
    total = npc_ref[tile] * RUN_ALIGN
    chunk = 1 << (TILE_ROWS.bit_length() - 1)
    while chunk >= RUN_ALIGN:
        @pl.when((total & chunk) != 0)
        def _():
            make_copy(0, 0, chunk).wait()
        chunk //= 2


def _dispatch_kernel(pe_ref, pd_ref, tsrc_ref, pdst_ref, npc_ref, idx_ref, rank_ref, hpp_ref, hps_ref,
                     xs_ref, srt, zeros, sem):
    i = pl.program_id(0)
    last = pl.num_programs(0) - 1
    slot = i % 2
    tm = hpp_ref.shape[0]
    bm = zeros.shape[0]

    def last_block(e):
        return pltpu.make_async_copy(zeros, xs_ref.at[pl.ds(pl.multiple_of(pe_ref[e] - bm, bm), bm)], sem.at[2])

    @pl.when(i == 0)
    def _():
        zeros[...] = jnp.zeros_like(zeros)

        def zstart(e, c):
            @pl.when(pd_ref[e] > 0)
            def _():
                last_block(e).start()
            return c

        def zwait(e, c):
            @pl.when(pd_ref[e] > 0)
            def _():
                last_block(e).wait()
            return c

        lax.fori_loop(0, N_EXPERTS, zstart, 0)
        lax.fori_loop(0, N_EXPERTS, zwait, 0)

    pos = _tile_positions(idx_ref, rank_ref, tsrc_ref, i)
    x = jnp.where(i == last, hps_ref[...], hpp_ref[...])
    def sort_chunk(c):
        p_iota = c * POS_CHUNK + lax.broadcasted_iota(I32, (POS_CHUNK, tm), 0)
        hit = pos[0:1, :] == p_iota
        for k in range(1, TOP_K):
            hit = jnp.logical_or(hit, pos[k:k + 1, :] == p_iota)
        onehot = jnp.where(hit, 1.0, 0.0).astype(BF16)
        srt[slot, c * POS_CHUNK:(c + 1) * POS_CHUNK, :] = jnp.dot(onehot, x,
                                                                  preferred_element_type=F32).astype(BF16)

    def copier(buf_slot):
        def copy(s, d, rows):
            return pltpu.make_async_copy(srt.at[buf_slot, pl.ds(s, rows)], xs_ref.at[pl.ds(d, rows)],
                                         sem.at[buf_slot])
        return copy

    def sort_tile(full):
        always = tm * TOP_K // POS_CHUNK if full else 0
        for c in range(TILE_ROWS // POS_CHUNK):
            if c < always:
                sort_chunk(c)
                for j in range(c * CHUNK_PIECES, (c + 1) * CHUNK_PIECES):
                    _start_piece(i, j, pdst_ref, copier(slot))
            else:
                pl.when(npc_ref[i] * RUN_ALIGN > c * POS_CHUNK)(functools.partial(sort_chunk, c))
        _start_pieces(i, pdst_ref, npc_ref, copier(slot), first=FULL_TILE_PIECES if full else 0)

    pl.when(i < last)(functools.partial(sort_tile, True))
    pl.when(i == last)(functools.partial(sort_tile, False))

    @pl.when(i > 0)
    def _():
        _wait_pieces(i - 1, npc_ref, copier(1 - slot))

    @pl.when(i == last)
    def _():
        _wait_pieces(i, npc_ref, copier(slot))


def _dispatch(hp_p, hp_s, idx_all, rank_all, pad_end, padded, tsrc, pdst, npc, n_rows):
    tm = MOE_TILE
    d = hp_p.shape[1]
    n_tiles = tsrc.shape[0]
    cols = pl.BlockSpec((TOP_K, tm), lambda i, *_: (0, i))
    return pl.pallas_call(
        _dispatch_kernel,
        grid_spec=pltpu.PrefetchScalarGridSpec(
            num_scalar_prefetch=5, grid=(n_tiles,),
            in_specs=[cols, cols,
                      pl.BlockSpec((tm, d), lambda i, *_: (jnp.minimum(i, n_tiles - 2), 0)),
                      pl.BlockSpec((tm, d), lambda i, *_: (0, 0))],
            out_specs=pl.BlockSpec(memory_space=pl.ANY),
            scratch_shapes=[pltpu.VMEM((2, TILE_ROWS, d), BF16), pltpu.VMEM((EXPERT_ROWS, d), BF16),
                            pltpu.SemaphoreType.DMA((3,))]),
        out_shape=jax.ShapeDtypeStruct((n_rows, d), BF16),
        compiler_params=_cparams("arbitrary"),
        name="dispatch",
    )(pad_end, padded, tsrc, pdst, npc, idx_all, rank_all, hp_p, hp_s)


def _expert_kernel(be_ref, first_ref, nu_ref, x_ref, wg_ref, wu_ref, wd_ref, o_ref, wgu_s, wd_s):
    i = pl.program_id(0)
    ff = wd_s.shape[0]

    @pl.when(i < nu_ref[0])
    def _():
        @pl.when(first_ref[i] == 1)
        def _():
            wgu_s[:, :ff] = wg_ref[...].astype(BF16)
            wgu_s[:, ff:] = wu_ref[...].astype(BF16)
            wd_s[...] = wd_ref[...].astype(BF16)

        gu = jnp.dot(x_ref[...], wgu_s[...], preferred_element_type=F32)
        act = _silu(gu[:, :ff]) * gu[:, ff:]
        o_ref[...] = jnp.dot(act.astype(BF16), wd_s[...], preferred_element_type=F32).astype(o_ref.dtype)


def _experts(xs, block_e, first, n_used, w_gate, w_up, w_down, layer):
    n_rows, d = xs.shape
    ff = w_gate.shape[-1]
    bm = EXPERT_ROWS
    grid_spec = pltpu.PrefetchScalarGridSpec(
        num_scalar_prefetch=3,
        grid=(n_rows // bm,),
        in_specs=[pl.BlockSpec((bm, d), lambda i, be, fi, nu: (jnp.minimum(i, nu[0] - 1), 0)),
                  pl.BlockSpec((None, None, d, ff), lambda i, be, fi, nu: (layer, be[i], 0, 0)),
                  pl.BlockSpec((None, None, d, ff), lambda i, be, fi, nu: (layer, be[i], 0, 0)),
                  pl.BlockSpec((None, None, ff, d), lambda i, be, fi, nu: (layer, be[i], 0, 0))],
        out_specs=pl.BlockSpec((bm, d), lambda i, be, fi, nu: (jnp.minimum(i, nu[0] - 1), 0)),
        scratch_shapes=[pltpu.VMEM((d, 2 * ff), BF16), pltpu.VMEM((ff, d), BF16)],
    )
    return pl.pallas_call(
        _expert_kernel,
        grid_spec=grid_spec,
        out_shape=jax.ShapeDtypeStruct((n_rows, d), BF16),
        compiler_params=_cparams("arbitrary"),
        name="experts",
    )(block_e, first, n_used, xs, w_gate, w_up, w_down)


def _combine_kernel(tsrc_ref, pdst_ref, npc_ref, idx_ref, rank_ref, w_ref, xs_ref, gf_ref, gfin_ref, os_ref,
                    y_ref, buf, sem, *, tile0, clear, final_norm):
    i = pl.program_id(0)
    tile = tile0 + i
    slot = i % 2
    tm, d = xs_ref.shape

    def copier(buf_slot):
        def copy(s, d, rows):
            return pltpu.make_async_copy(os_ref.at[pl.ds(d, rows)], buf.at[buf_slot, pl.ds(s, rows)],
                                         sem.at[buf_slot])
        return copy

    @pl.when(i == 0)
    def _():
        first_free = 0 if clear else tm * TOP_K
        buf[:, first_free:, :] = jnp.zeros((2, buf.shape[1] - first_free, d), buf.dtype)
        _start_pieces(tile, pdst_ref, npc_ref, copier(0))

    full = not clear
    n_steps = pl.num_programs(0)
    nxt = jnp.minimum(tile + 1, tile0 + n_steps - 1)

    pos = _tile_positions(idx_ref, rank_ref, tsrc_ref, tile).astype(F32)
    w = w_ref[...]
    pos_c = [_row_to_col(pos[k:k + 1, :]) for k in range(TOP_K)]
    w_c = [_row_to_col(w[k:k + 1, :]) for k in range(TOP_K)]
    _wait_pieces(tile, npc_ref, copier(slot))

    def chunk_sum(c):
        p_iota = (c * POS_CHUNK + lax.broadcasted_iota(I32, (tm, POS_CHUNK), 1)).astype(F32)
        wmat = jnp.where(pos_c[0] == p_iota, w_c[0], 0.0)
        for k in range(1, TOP_K):
            wmat = wmat + jnp.where(pos_c[k] == p_iota, w_c[k], 0.0)
        return _dot(wmat, buf[slot, c * POS_CHUNK:(c + 1) * POS_CHUNK, :])

    always = tm * TOP_K // POS_CHUNK
    acc = None
    for c in range(always):
        if full:
            for j in range(c * CHUNK_PIECES, (c + 1) * CHUNK_PIECES):
                _start_piece(nxt, j, pdst_ref, copier(1 - slot))
        part = chunk_sum(c)
        acc = part if acc is None else acc + part
    if full:
        _start_pieces(nxt, pdst_ref, npc_ref, copier(1 - slot), first=FULL_TILE_PIECES)

        @pl.when(i == n_steps - 1)
        def _():
            _wait_pieces(nxt, npc_ref, copier(1 - slot))
    y_ref[...] = acc
    for c in range(always, TILE_ROWS // POS_CHUNK):
        @pl.when(npc_ref[tile] * RUN_ALIGN > c * POS_CHUNK)
        def _():
            y_ref[...] += chunk_sum(c)
    x2 = xs_ref[...] + gf_ref[0] * y_ref[...]
    y_ref[...] = _rms(x2, gfin_ref[...]) if final_norm else x2


def _combine(tok, tile0, xs, g_f, g_final, idx_all, rank_all, wt_all, out_sorted, tsrc, pdst, npc, final_norm,
             clear):
    n, d = xs.shape
    gf, gf_spec = tok.mod(g_f)
    tm = MOE_TILE
    cols = pl.BlockSpec((TOP_K, tm), lambda i, *_: (0, tile0 + i))
    return pl.pallas_call(
        functools.partial(_combine_kernel, tile0=tile0, clear=clear, final_norm=final_norm),
        grid_spec=pltpu.PrefetchScalarGridSpec(
            num_scalar_prefetch=3, grid=(tok.n_tiles,),
            in_specs=[cols, cols, cols, tok.rows(d), gf_spec, _full((1, d)), pl.BlockSpec(memory_space=pl.ANY)],
            out_specs=tok.rows(d),
            scratch_shapes=[pltpu.VMEM((2, TILE_ROWS, d), BF16), pltpu.SemaphoreType.DMA((2,))]),
        out_shape=jax.ShapeDtypeStruct((n, d), F32),
        compiler_params=_cparams("arbitrary"),
        name="combine",
    )(tsrc, pdst, npc, idx_all, rank_all, wt_all, xs, gf, g_final.reshape(1, d), out_sorted)


def _rope(x, cos2, sin2):
    half = x.shape[1] // 2
    swapped = jnp.concatenate([x[:, half:], x[:, :half]], axis=1)
    return x * cos2 + swapped * sin2


def _in1_kernel(x_ref, shk_ref, sck_ref, gk_ref, wdkv_ref, glat_ref, cos_ref, sin_ref,
                shm_ref, scm_ref, gm_ref, wdq_ref, gq_ref, wqn_ref, wqr_ref, wuk_ref,
                lat_ref, kr_ref, latb_ref, krb_ref, qa_ref, qr_ref):
    x = x_ref[...]
    cos2, sin2 = cos_ref[...], sin_ref[...]
    att_scale = (NOPE_DIM + ROPE_DIM) ** -0.5

    hk = _rms_mod(x, gk_ref[...], shk_ref[0], sck_ref[0])
    ckr = _dot_w(hk, wdkv_ref[...])
    lat = _rms(ckr[:, :KV_LORA], glat_ref[...])
    kr = _rope(ckr[:, KV_LORA:], cos2, sin2)
    lat_ref[...] = lat
    kr_ref[...] = kr
    latb_ref[...] = lat.astype(BF16)
    krb_ref[...] = kr.astype(BF16)

    hm = _rms_mod(x, gm_ref[...], shm_ref[0], scm_ref[0])
    q_lat = _rms(_dot_w(hm, wdq_ref[...]), gq_ref[...])
    q_nope = _dot_w(q_lat, wqn_ref[...])
    q_rope = _dot_w(q_lat, wqr_ref[...])
    for h in range(MLA_HEADS):
        qa = _dot_w(q_nope[:, h * NOPE_DIM:(h + 1) * NOPE_DIM], wuk_ref[h])
        qa_ref[h] = (qa * att_scale).astype(BF16)
        qr = _rope(q_rope[:, h * ROPE_DIM:(h + 1) * ROPE_DIM], cos2, sin2)
        qr_ref[h] = (qr * att_scale).astype(BF16)


def _in1(tok, x, sh_kv, sc_kv, g_kv_in, w_dkv, g_kv_lat, cos2, sin2, sh_m, sc_m, g_mix, w_dq, g_q_lat,
         w_q_nope, w_q_rope, w_uk_t):
    n, d = x.shape
    q_lora = w_dq.shape[1]
    shk, shk_spec = tok.mod(sh_kv)
    sck, sck_spec = tok.mod(sc_kv)
    shm, shm_spec = tok.mod(sh_m)
    scm, scm_spec = tok.mod(sc_m)
    heads = lambda w: pl.BlockSpec((MLA_HEADS, tok.tm, w), lambda i: (0, i, 0))
    return pl.pallas_call(
        _in1_kernel,
        grid=(tok.n_tiles,),
        in_specs=[tok.rows(d), shk_spec, sck_spec, _full((1, d)), _full(w_dkv.shape), _full((1, KV_LORA)),
                  tok.seq_rows(ROPE_DIM), tok.seq_rows(ROPE_DIM),
                  shm_spec, scm_spec, _full((1, d)), _full(w_dq.shape), _full((1, q_lora)),
                  _full(w_q_nope.shape), _full(w_q_rope.shape), _full(w_uk_t.shape)],
        out_specs=[tok.rows(KV_LORA), tok.rows(ROPE_DIM), tok.rows(KV_LORA), tok.rows(ROPE_DIM),
                   heads(KV_LORA), heads(ROPE_DIM)],
        out_shape=[jax.ShapeDtypeStruct((n, KV_LORA), F32), jax.ShapeDtypeStruct((n, ROPE_DIM), F32),
                   jax.ShapeDtypeStruct((n, KV_LORA), BF16), jax.ShapeDtypeStruct((n, ROPE_DIM), BF16),
                   jax.ShapeDtypeStruct((MLA_HEADS, n, KV_LORA), BF16),
                   jax.ShapeDtypeStruct((MLA_HEADS, n, ROPE_DIM), BF16)],
        compiler_params=_cparams("arbitrary"),
        name="in1",
    )(x, shk, sck, g_kv_in.reshape(1, d), w_dkv, g_kv_lat.reshape(1, KV_LORA), cos2, sin2,
      shm, scm, g_mix.reshape(1, d), w_dq, g_q_lat.reshape(1, q_lora), w_q_nope, w_q_rope, w_uk_t)


def _attn_prompt_kernel(qa_ref, qr_ref, lat_ref, kr_ref, wuv_ref, o_ref, m_s, l_s, acc_s):
    heads, tq, c = qa_ref.shape
    qi = pl.program_id(1)
    n_lt = tq // LANES
    m_s[...] = jnp.full(m_s.shape, NEG_INF, F32)
    l_s[...] = jnp.zeros(l_s.shape, F32)
    acc_s[...] = jnp.zeros(acc_s.shape, F32)
    causal = lax.broadcasted_iota(I32, (tq, tq), 1) <= lax.broadcasted_iota(I32, (tq, tq), 0)

    def chunk(j, diagonal):
        start = pl.multiple_of(j * tq, tq)
        kc = lat_ref[pl.ds(start, tq), :]
        krc = kr_ref[pl.ds(start, tq), :]
        def scores(h):
            return _dot_nt(qa_ref[h], kc) + _dot_nt(qr_ref[h], krc)

        s_next = scores(0)
        for h in range(heads):
            s, s_next = s_next, (scores(h + 1) if h + 1 < heads else None)
            if diagonal:
                s = jnp.where(causal, s, NEG_INF)
            tiles = [s[:, t * LANES:(t + 1) * LANES] for t in range(n_lt)]
            mx = tiles[0]
            for t in tiles[1:]:
                mx = jnp.maximum(mx, t)
            m_old = m_s[h]
            m_new = jnp.maximum(m_old, jnp.broadcast_to(jnp.max(mx, axis=1, keepdims=True), (tq, LANES)))
            alpha = jnp.exp(m_old - m_new)
            ps = [jnp.exp(t - m_new) for t in tiles]
            part = ps[0]
            for p in ps[1:]:
                part = part + p
            l_s[h] = alpha * l_s[h] + part
            pv = _dot(jnp.concatenate(ps, axis=1), kc)
            acc_s[h] = jnp.concatenate([alpha] * (c // LANES), axis=1) * acc_s[h] + pv
            m_s[h] = m_new

    def step(j, carry):
        chunk(j, False)
        return carry

    lax.fori_loop(0, qi, step, 0)
    chunk(qi, True)
    for h in range(heads):
        o_lat = acc_s[h] / jnp.sum(l_s[h], axis=1, keepdims=True)
        o_ref[:, h * V_DIM:(h + 1) * V_DIM] = _dot(o_lat, wuv_ref[h]).astype(o_ref.dtype)


def _attn_prompt(qa, qr, latb, krb, w_uv_t, batch, seq, tq=256):
    tq = min(tq, seq)
    nq = seq // tq
    return pl.pallas_call(
        _attn_prompt_kernel,
        grid=(batch, nq),
        in_specs=[pl.BlockSpec((MLA_HEADS, tq, KV_LORA), lambda b, i: (0, b * nq + i, 0)),
                  pl.BlockSpec((MLA_HEADS, tq, ROPE_DIM), lambda b, i: (0, b * nq + i, 0)),
                  pl.BlockSpec((seq, KV_LORA), lambda b, i: (b, 0)),
                  pl.BlockSpec((seq, ROPE_DIM), lambda b, i: (b, 0)),
                  pl.BlockSpec(w_uv_t.shape, lambda b, i: (0, 0, 0))],
        out_specs=pl.BlockSpec((tq, MLA_HEADS * V_DIM), lambda b, i: (b * nq + i, 0)),
        out_shape=jax.ShapeDtypeStruct((batch * seq, MLA_HEADS * V_DIM), BF16),
        scratch_shapes=[pltpu.VMEM((MLA_HEADS, tq, LANES), F32), pltpu.VMEM((MLA_HEADS, tq, LANES), F32),
                        pltpu.VMEM((MLA_HEADS, tq, KV_LORA), F32)],
        compiler_params=_cparams("arbitrary", "arbitrary"),
        name="attn_prompt",
    )(qa, qr, latb, krb, w_uv_t)


def _attn_decode_kernel(pt_ref, qa_ref, qr_ref, lat1_ref, kr1_ref, wuv_ref, *rest, pages):
    lat_refs = rest[:pages]
    kr_refs = rest[pages:2 * pages]
    o_ref, m_s, l_s, acc_s = rest[2 * pages:]
    j = pl.program_id(1)
    qa = qa_ref[0]
    qr = qr_ref[0]

    @pl.when(j == 0)
    def _():
        lat1 = lat1_ref[0].astype(F32)
        m_s[...] = (jnp.sum(qa.astype(F32) * lat1, axis=1, keepdims=True)
                    + jnp.sum(qr.astype(F32) * kr1_ref[0].astype(F32), axis=1, keepdims=True))
        l_s[...] = jnp.ones(l_s.shape, F32)
        acc_s[...] = jnp.broadcast_to(lat1, acc_s.shape)

    lats = [r[0].astype(BF16) for r in lat_refs]
    s = jnp.concatenate([_dot_nt(qa, lats[p]) + _dot(qr, kr_refs[p][0]) for p in range(pages)], axis=1)
    m_old = m_s[...]
    m_new = jnp.maximum(m_old, jnp.max(s, axis=1, keepdims=True))
    alpha = jnp.exp(m_old - m_new)
    p_all = jnp.exp(s - m_new)
    l_s[...] = alpha * l_s[...] + jnp.sum(p_all, axis=1, keepdims=True)
    page = lats[0].shape[0]
    pv = _dot(p_all[:, :page], lats[0])
    for p in range(1, pages):
        pv = pv + _dot(p_all[:, p * page:(p + 1) * page], lats[p])
    acc_s[...] = alpha * acc_s[...] + pv
    m_s[...] = m_new

    @pl.when(j == pl.num_programs(1) - 1)
    def _():
        o_all = _dot(acc_s[...] / l_s[...], wuv_ref[...])
        for h in range(o_all.shape[0]):
            o_ref[0, :, h * V_DIM:(h + 1) * V_DIM] = o_all[h:h + 1, h * V_DIM:(h + 1) * V_DIM].astype(o_ref.dtype)


def _attn_decode(qa, qr, lat1, kr1, w_uv_flat, cache_latent, cache_krope_t, page_table, pages=64):
    batch, n_pages = page_table.shape
    pages = min(pages, n_pages)
    page = cache_latent.shape[1]
    steps = n_pages // pages
    one = lambda shape: pl.BlockSpec((1,) + shape, lambda b, j, pt: (b,) + (0,) * len(shape))
    paged = lambda w, p: pl.BlockSpec((1, page, w) if w == KV_LORA else (1, w, page),
                                      lambda b, j, pt: (pt[b, j * pages + p], 0, 0))
    grid_spec = pltpu.PrefetchScalarGridSpec(
        num_scalar_prefetch=1,
        grid=(batch, steps),
        in_specs=[one((MLA_HEADS, KV_LORA)), one((MLA_HEADS, ROPE_DIM)), one((1, KV_LORA)), one((1, ROPE_DIM)),
                  pl.BlockSpec(w_uv_flat.shape, lambda b, j, pt: (0, 0))]
                 + [paged(KV_LORA, p) for p in range(pages)] + [paged(ROPE_DIM, p) for p in range(pages)],
        out_specs=one((1, MLA_HEADS * V_DIM)),
        scratch_shapes=[pltpu.VMEM((MLA_HEADS, 1), F32), pltpu.VMEM((MLA_HEADS, 1), F32),
                        pltpu.VMEM((MLA_HEADS, KV_LORA), F32)],
    )
    return pl.pallas_call(
        functools.partial(_attn_decode_kernel, pages=pages),
        grid_spec=grid_spec,
        out_shape=jax.ShapeDtypeStruct((batch, 1, MLA_HEADS * V_DIM), BF16),
        compiler_params=_cparams("arbitrary", "arbitrary"),
        name="attn_decode",
    )(page_table, qa, qr, lat1, kr1, w_uv_flat, *([cache_latent] * pages), *([cache_krope_t] * pages))


def _rope_tables(pos):
    half = ROPE_DIM // 2
    inv = ROPE_THETA ** (-jnp.arange(half, dtype=F32) / half)
    ang = pos.astype(F32)[:, None] * inv[None, :]
    cos, sin = jnp.cos(ang), jnp.sin(ang)
    return jnp.concatenate([cos, cos], axis=1), jnp.concatenate([-sin, sin], axis=1)


def _routing_tables(tile_counts, n_rows):
    bm = EXPERT_ROWS
    tile_counts = (tile_counts + RUN_ALIGN - 1) // RUN_ALIGN * RUN_ALIGN
    counts = jnp.sum(tile_counts, axis=0)
    padded = (counts + bm - 1) // bm * bm
    pad_end = jnp.cumsum(padded).astype(I32)
    pad_start = pad_end - padded
    tile_dst = pad_start[None, :] + jnp.cumsum(tile_counts, axis=0) - tile_counts
    tile_src = jnp.cumsum(tile_counts, axis=1) - tile_counts
    run_end = tile_src + tile_counts
    piece_row = jnp.arange(TILE_PIECES, dtype=I32) * RUN_ALIGN
    piece_e = jnp.sum((run_end[:, None, :] <= piece_row[None, :, None]).astype(I32), axis=2)
    owner = (piece_e[:, :, None] == jnp.arange(N_EXPERTS, dtype=I32)[None, None, :]).astype(I32)
    piece_dst = jnp.sum(owner * (tile_dst - tile_src)[:, None, :], axis=2) + piece_row[None, :]
    n_pieces = run_end[:, -1] // RUN_ALIGN
    block_start = jnp.arange(n_rows // bm, dtype=I32) * bm
    block_e = jnp.sum((pad_end[None, :] <= block_start[:, None]).astype(I32), axis=1)
    block_e = jnp.minimum(block_e, N_EXPERTS - 1)
    first = jnp.concatenate([jnp.ones((1,), I32), (block_e[1:] != block_e[:-1]).astype(I32)])
    n_used = pad_end[-1:] // bm
    return (pad_end, padded, tile_src.astype(I32), piece_dst.reshape(-1).astype(I32), n_pieces.astype(I32),
            block_e, first, n_used)


def kernel(x_prompt, x_sample, state_mlstm_C, state_mlstm_n, state_mlstm_m, cache_latent, cache_krope, page_table, c_prompt, c_sample, w_ada, b_ada, g_mix, g_ffn, w_mlstm_in, b_mlstm_gates, g_mlstm_head, w_mlstm_out, w_ada_kv, b_ada_kv, g_kv_in, w_dkv, g_kv_lat, w_uk, w_uv, w_dq, g_q_lat, w_uq, w_mla_out, w_router, b_router, w_exp_gate, w_exp_up, w_exp_down, w_sh_gate, w_sh_up, w_sh_down, g_final):
    bp, tp, d = x_prompt.shape
    bs, ts, _ = x_sample.shape
    depth = w_ada.shape[0]
    n_a = w_mlstm_in.shape[0]
    past_len = page_table.shape[1] * cache_latent.shape[1]
    hk, hv = ML_HEADS * ML_DK, ML_HEADS * ML_DV
    n_p, n_s = bp * tp, bs * ts
    n_tot = n_p + n_s
    n_moe_tiles = n_p // MOE_TILE + 1
    n_rows = n_tot * TOP_K + n_moe_tiles * N_EXPERTS * (RUN_ALIGN - 1) + N_EXPERTS * (EXPERT_ROWS - 1)
    n_rows = -(-n_rows // EXPERT_ROWS) * EXPERT_ROWS

    groups = [
        dict(batch=bp, seq=tp, tok=_Tok(bp, tp, 256), x=x_prompt.reshape(n_p, d), pos0=0,
             C0=jnp.zeros((n_a, bp, ML_HEADS, ML_DK, ML_DV), F32), n0=jnp.zeros((n_a, bp, ML_HEADS, ML_DK), F32),
             m0=jnp.full((n_a, bp, ML_HEADS), M_INIT, F32)),
        dict(batch=bs, seq=ts, tok=_Tok(bs, ts, 256), x=x_sample.reshape(n_s, d), pos0=past_len,
             C0=state_mlstm_C, n0=state_mlstm_n, m0=state_mlstm_m),
    ]
    c_all = jnp.concatenate([c_prompt, c_sample], axis=0)
    row0 = [0, bp]
    for g in groups:
        g["Cs"], g["ns"], g["ms"] = [], [], []

    for g in groups:
        g["cos2"], g["sin2"] = _rope_tables(g["pos0"] + jnp.arange(g["seq"], dtype=I32))
        if g["seq"] == 1:
            g["cos2"] = jnp.broadcast_to(g["cos2"], (g["batch"], ROPE_DIM))
            g["sin2"] = jnp.broadcast_to(g["sin2"], (g["batch"], ROPE_DIM))

    mod_kv = _ada(c_all, w_ada_kv, b_ada_kv)
    w_uk_t = jnp.transpose(w_uk, (1, 2, 0))
    w_uv_t = jnp.transpose(w_uv, (1, 0, 2)).astype(BF16)

    for l in range(depth):
        mod = _ada(c_all, w_ada, b_ada[l], layer=l)
        w_router_t = w_router[l].T
        ws_gu = jnp.concatenate([w_sh_gate[l], w_sh_up[l]], axis=1).astype(BF16)
        ws_d = w_sh_down[l].astype(BF16)
        for gi, g in enumerate(groups):
            r0, nb = row0[gi], g["batch"]
            sh_m, sc_m, g_m, sh_f, sc_f, g_f = [mod[r0:r0 + nb, i * d:(i + 1) * d] for i in range(6)]
            g["g_f"] = g_f
            tok = g["tok"]
            wdt = F32 if g["seq"] == 1 else BF16
            if l < n_a:
                w_in = w_mlstm_in[l]
                w_main = w_in[:, :2 * hk + 2 * hv].astype(wdt)
                w_gates = jnp.pad(w_in[:, 2 * hk + 2 * hv:], ((0, 0), (0, 128 - 2 * ML_HEADS)))
                b_gates = jnp.pad(b_mlstm_gates[l], (0, 128 - 2 * ML_HEADS)).reshape(1, 128)
                if g["seq"] == 1:
                    proj, gates = _in0(tok, g["x"], sh_m, sc_m, g_mix[l], w_main, w_gates, b_gates, F32)
                    a, Cn, nn, mn = _mlstm_step(proj, gates, g["C0"][l], g["n0"][l], g["m0"][l],
                                                g_mlstm_head[l], nb)
                    a = a.reshape(nb, hv)
                else:
                    proj, gates = _in0(tok, g["x"], sh_m, sc_m, g_mix[l], w_main, w_gates, b_gates, BF16)
                    a, Cn, nn, mn = _mlstm_chunks(proj, gates, g["C0"][l], g["n0"][l], g["m0"][l],
                                                  g_mlstm_head[l], nb, g["seq"])
                g["Cs"].append(Cn)
                g["ns"].append(nn)
                g["ms"].append(mn.reshape(nb, ML_HEADS))
                w_o = w_mlstm_out[l].astype(wdt)
            else:
                j = l - n_a
                w_q = w_uq[j].reshape(-1, MLA_HEADS, NOPE_DIM + ROPE_DIM)
                w_q_nope = w_q[:, :, :NOPE_DIM].reshape(-1, MLA_HEADS * NOPE_DIM).astype(wdt)
                w_q_rope = w_q[:, :, NOPE_DIM:].reshape(-1, MLA_HEADS * ROPE_DIM).astype(wdt)
                sh_kv, sc_kv = mod_kv[r0:r0 + nb, :d], mod_kv[r0:r0 + nb, d:]
                lat, kr, latb, krb, qa, qr = _in1(tok, g["x"], sh_kv, sc_kv, g_kv_in, w_dkv.astype(wdt), g_kv_lat,
                                                  g["cos2"], g["sin2"], sh_m, sc_m, g_mix[l], w_dq[j].astype(wdt),
                                                  g_q_lat[j], w_q_nope, w_q_rope, w_uk_t.astype(wdt))
                if l == n_a:
                    g["lat"], g["kr"], g["latb"], g["krb"] = lat, kr, latb, krb
                if g["seq"] == 1:
                    a = _attn_decode(jnp.transpose(qa, (1, 0, 2)), jnp.transpose(qr, (1, 0, 2)),
                                     g["latb"].reshape(nb, 1, KV_LORA), g["krb"].reshape(nb, 1, ROPE_DIM),
                                     w_uv.reshape(KV_LORA, MLA_HEADS * V_DIM).astype(BF16),
                                     cache_latent, jnp.swapaxes(cache_krope, 1, 2),
                                     page_table).reshape(nb, MLA_HEADS * V_DIM)
                else:
                    a = _attn_prompt(qa, qr, g["latb"], g["krb"], w_uv_t, nb, g["seq"])
                w_o = w_mla_out[j].astype(wdt)
            g["xs"], g["hp"], g["idx"], g["wt"], g["rank"], g["cnt"] = _post(
                tok, g["x"], a, w_o, g_m, g_ffn[l], sh_f, sc_f, g_f, w_router_t.astype(wdt), b_router[l], ws_gu, ws_d)

        gp, gs = groups
        pad_t = MOE_TILE - n_s
        lanes = lambda v, fill: jnp.concatenate(
            [gp[v], jnp.pad(gs[v], ((0, 0), (0, pad_t)), constant_values=fill)], axis=1)
        idx_all, wt_all = lanes("idx", 0), lanes("wt", 0)
        rank_all = lanes("rank", TILE_ROWS)
        tile_counts = jnp.concatenate([gp["cnt"][:, :, 0], gs["cnt"][:, :, 0]], axis=0).astype(I32)
        pad_end, padded, tsrc, pdst, npc, block_e, first, n_used = _routing_tables(tile_counts, n_rows)
        rows = lambda v: jnp.pad(v, ((0, pad_t), (0, 0)))
        xs_sorted = _dispatch(gp["hp"], rows(gs["hp"]), idx_all, rank_all, pad_end, padded, tsrc, pdst, npc, n_rows)
        out_sorted = _experts(xs_sorted, block_e, first, n_used, w_exp_gate, w_exp_up, w_exp_down, l)
        last = l == depth - 1
        gp["x"] = _combine(gp["tok"], 0, gp["xs"], gp["g_f"], g_final, idx_all, rank_all, wt_all, out_sorted,
                           tsrc, pdst, npc, last, clear=False)
        gs["x"] = _combine(_Tok(MOE_TILE, 1, MOE_TILE), gp["tok"].n_tiles, rows(gs["xs"]), rows(gs["g_f"]), g_final,
                           idx_all, rank_all, wt_all, out_sorted, tsrc, pdst, npc, last, clear=True)[:n_s]

    outs = []
    for g in groups:
        nb, seq = g["batch"], g["seq"]
        outs.append((g["x"].reshape(nb, seq, d), jnp.stack(g["Cs"]), jnp.stack(g["ns"]), jnp.stack(g["ms"]),
                     g["lat"].reshape(nb, seq, KV_LORA), g["kr"].reshape(nb, seq, ROPE_DIM)))
    p, s = outs
    return (p[0], s[0], p[1], p[2], p[3], p[4], p[5], s[1], s[2], s[3], s[4], s[5])
```

```python
import functools

import jax
import jax.numpy as jnp
from jax import lax
from jax.experimental import pallas as pl
from jax.experimental.pallas import tpu as pltpu

F32 = jnp.float32
BF16 = jnp.bfloat16
I32 = jnp.int32
HIGHEST = lax.Precision.HIGHEST
NEG_INF = float("-inf")

ML_HEADS = 4
ML_DK = 128
ML_DV = 256
ML_CHUNK = 128
M_INIT = -1e30
MLA_HEADS = 8
NOPE_DIM = 128
ROPE_DIM = 64
V_DIM = 128
KV_LORA = 256
ROPE_THETA = 10000.0
N_EXPERTS = 64
TOP_K = 8
N_GROUPS = 8
TOPK_GROUPS = 4
ROUTED_SCALE = 2.5
NORM_EPS = 1e-6

LANES = 128
EXPERT_ROWS = 1024
VMEM_LIMIT = 56 * 1024 * 1024


def _cparams(*sem):
    return pltpu.CompilerParams(dimension_semantics=sem, vmem_limit_bytes=VMEM_LIMIT)


def _dot(a, b):
    return jnp.dot(a.astype(BF16), b.astype(BF16), preferred_element_type=F32)


def _dot_nt(a, b):
    return lax.dot_general(a.astype(BF16), b.astype(BF16), (((1,), (1,)), ((), ())),
                           preferred_element_type=F32)


def _dot_f32(a, b):
    return jnp.dot(a, b, precision=HIGHEST, preferred_element_type=F32)


def _dot_w(a, w):
    return _dot_f32(a.astype(F32), w) if w.dtype == F32 else _dot(a, w)


def _dot_w_nt(w, a):
    if w.dtype == F32:
        return lax.dot_general(w, a.astype(F32), (((1,), (1,)), ((), ())), precision=HIGHEST,
                               preferred_element_type=F32)
    return _dot_nt(w, a)


def _rms(x, g):
    return x * lax.rsqrt(jnp.mean(x * x, axis=-1, keepdims=True) + NORM_EPS) * g


def _rms_mod(x, g, shift, scale):
    return _rms(x, g) * (1.0 + scale) + shift


def _sigmoid(x):
    return 1.0 / (1.0 + jnp.exp(-x))


def _silu(x):
    return x * _sigmoid(x)


def _log_sigmoid(x):
    return jnp.minimum(x, 0.0) - jnp.log1p(jnp.exp(-jnp.abs(x)))


def _ada_kernel(c_ref, w_ref, b_ref, o_ref):
    o_ref[...] = _dot_f32(_silu(c_ref[...]), w_ref[...]) + b_ref[...]


def _ada(c, w, b, layer=None, tn=1024):
    m, d = c.shape
    n_out = w.shape[-1]
    if layer is None:
        w_spec = pl.BlockSpec((d, tn), lambda j: (0, j))
    else:
        w_spec = pl.BlockSpec((None, d, tn), lambda j: (layer, 0, j))
    return pl.pallas_call(
        _ada_kernel,
        grid=(n_out // tn,),
        in_specs=[pl.BlockSpec((m, d), lambda j: (0, 0)), w_spec, pl.BlockSpec((1, tn), lambda j: (0, j))],
        out_specs=pl.BlockSpec((m, tn), lambda j: (0, j)),
        out_shape=jax.ShapeDtypeStruct((m, n_out), F32),
        compiler_params=_cparams("arbitrary"),
        name="ada",
    )(c, w, b.reshape(1, n_out))


class _Tok:
    def __init__(self, batch, seq, tile):
        if seq == 1:
            self.tm, self.n_tiles, self.per_seq = batch, 1, 1
            self.mod_block = (1, batch, None)
        else:
            self.tm = min(tile, seq)
            self.per_seq = seq // self.tm
            self.n_tiles = batch * self.per_seq
            self.mod_block = (1, 1, None)
        self.batch, self.seq = batch, seq

    def mod(self, v):
        d = v.shape[-1]
        if self.seq == 1:
            return v.reshape(1, self.batch, d), pl.BlockSpec((1, self.batch, d), lambda i, *_: (0, 0, 0))
        per = self.per_seq
        return v.reshape(self.batch, 1, d), pl.BlockSpec((1, 1, d), lambda i, *_: (i // per, 0, 0))

    def rows(self, width):
        return pl.BlockSpec((self.tm, width), lambda i, *_: (i, 0))

    def seq_rows(self, width):
        per = self.per_seq
        return pl.BlockSpec((self.tm, width), lambda i, *_: (i % per, 0))


def _full(shape):
    nd = len(shape)
    return pl.BlockSpec(shape, lambda i, *_: (0,) * nd)


def _in0_kernel(x_ref, sh_ref, sc_ref, g_ref, w_ref, wg_ref, bg_ref, o_ref, gt_ref):
    h = _rms_mod(x_ref[...], g_ref[...], sh_ref[0], sc_ref[0])
    o_ref[...] = _dot_w(h, w_ref[...]).astype(o_ref.dtype)
    gt_ref[...] = _dot_f32(h, wg_ref[...]) + bg_ref[...]


def _in0(tok, x, shift, scale, g, w, w_gates, b_gates, out_dtype):
    n, d = x.shape
    wo = w.shape[1]
    sh, sh_spec = tok.mod(shift)
    sc, sc_spec = tok.mod(scale)
    return pl.pallas_call(
        _in0_kernel,
        grid=(tok.n_tiles,),
        in_specs=[tok.rows(d), sh_spec, sc_spec, _full((1, d)), _full((d, wo)), _full((d, 128)), _full((1, 128))],
        out_specs=[tok.rows(wo), tok.rows(128)],
        out_shape=[jax.ShapeDtypeStruct((n, wo), out_dtype), jax.ShapeDtypeStruct((n, 128), F32)],
        compiler_params=_cparams("arbitrary"),
        name="in0",
    )(x, sh, sc, g.reshape(1, d), w, w_gates, b_gates)


def _mlstm_chunk_kernel(q_ref, k_ref, v_ref, og_ref, gt_ref, c0_ref, n0_ref, m0_ref, gh_ref,
                        hh_ref, c_ref, n_ref, m_ref):
    L = q_ref.shape[0]
    scale = ML_DK ** -0.5

    @pl.when(pl.program_id(1) == 0)
    def _():
        c_ref[...] = c0_ref[...]
        n_ref[...] = n0_ref[...]
        m_ref[...] = m0_ref[...]

    gates = gt_ref[...]
    logf = _log_sigmoid(gates)
    row = lax.broadcasted_iota(I32, (L, L), 0)
    col = lax.broadcasted_iota(I32, (L, L), 1)
    causal = col <= row
    tril = causal.astype(F32)
    b_cols = _dot_f32(tril, logf)
    gates_t = gates.T
    b_rows = _dot_f32(logf.T, (row <= col).astype(F32))

    for h in range(ML_HEADS):
        i_row = gates_t[h:h + 1, :]
        i_col = gates[:, h:h + 1]
        b_row = b_rows[ML_HEADS + h:ML_HEADS + h + 1, :]
        b_col = b_cols[:, ML_HEADS + h:ML_HEADS + h + 1]
        m_prev = m_ref[0, :, h:h + 1]
        c_prev = c_ref[0, h]
        n_prev = n_ref[0, h:h + 1, :]
        q = q_ref[:, h * ML_DK:(h + 1) * ML_DK]
        k = k_ref[:, h * ML_DK:(h + 1) * ML_DK]
        v = v_ref[:, h * ML_DV:(h + 1) * ML_DV]

        g_col = b_col + m_prev
        dmat = jnp.where(causal, b_col - b_row + i_row, NEG_INF)
        m_t = jnp.maximum(g_col, jnp.max(dmat, axis=1, keepdims=True))
        w_intra = jnp.exp(dmat - m_t)
        w_inter = jnp.exp(g_col - m_t)
        s = _dot_nt(q, k) * (w_intra * scale)
        num = _dot(s, v) + w_inter * _dot(q, c_prev)
        qn = jnp.sum(q.astype(F32) * n_prev, axis=1, keepdims=True)
        den = jnp.sum(s, axis=1, keepdims=True) + w_inter * qn
        cell = num / jnp.maximum(jnp.abs(den), jnp.exp(-m_t))

        b_last = b_col[L - 1:L, :]
        m_new = m_t[L - 1:L, :]
        a_inter = jnp.exp(b_last + m_prev - m_new)
        a_intra = jnp.exp(b_last - b_col + i_col - m_new) * scale
        kw = k.astype(F32) * a_intra
        c_ref[0, h] = a_inter * c_prev + _dot(kw.T, v)
        n_ref[0, h:h + 1, :] = a_inter * n_prev + jnp.sum(kw, axis=0, keepdims=True)
        m_ref[0, :, h:h + 1] = m_new

        y = _rms(cell, gh_ref[:, h * ML_DV:(h + 1) * ML_DV])
        og = og_ref[:, h * ML_DV:(h + 1) * ML_DV].astype(F32)
        hh_ref[:, h * ML_DV:(h + 1) * ML_DV] = (y * _sigmoid(og)).astype(hh_ref.dtype)


def _mlstm_chunks(proj, gates, c0, n0, m0, g_head, batch, seq):
    L = seq if seq <= ML_CHUNK else ML_CHUNK
    nc = seq // L
    hk, hv = ML_HEADS * ML_DK, ML_HEADS * ML_DV
    rows = lambda w, cb: pl.BlockSpec((L, w), lambda b, j: (b * nc + j, cb))
    state = lambda shape: pl.BlockSpec((1,) + shape, lambda b, j: (b,) + (0,) * len(shape))
    return pl.pallas_call(
        _mlstm_chunk_kernel,
        grid=(batch, nc),
        in_specs=[rows(hk, 0), rows(hk, 1), rows(hv, 2 * hk // hv), rows(hv, 2 * hk // hv + 1), rows(128, 0),
                  state((ML_HEADS, ML_DK, ML_DV)), state((ML_HEADS, ML_DK)), state((1, ML_HEADS)),
                  pl.BlockSpec((1, hv), lambda b, j: (0, 0))],
        out_specs=[rows(hv, 0), state((ML_HEADS, ML_DK, ML_DV)), state((ML_HEADS, ML_DK)), state((1, ML_HEADS))],
        out_shape=[jax.ShapeDtypeStruct((batch * seq, hv), BF16),
                   jax.ShapeDtypeStruct((batch, ML_HEADS, ML_DK, ML_DV), F32),
                   jax.ShapeDtypeStruct((batch, ML_HEADS, ML_DK), F32),
                   jax.ShapeDtypeStruct((batch, 1, ML_HEADS), F32)],
        compiler_params=_cparams("arbitrary", "arbitrary"),
        name="mlstm_chunks",
    )(proj, proj, proj, proj, gates, c0, n0, m0.reshape(batch, 1, ML_HEADS), g_head.reshape(1, hv))


def _row_to_col(r):
    n = r.shape[1]
    eye = lax.broadcasted_iota(I32, (n, n), 0) == lax.broadcasted_iota(I32, (n, n), 1)
    return jnp.sum(jnp.where(eye, jnp.broadcast_to(r, (n, n)), 0.0), axis=1, keepdims=True)


def _mlstm_step_kernel(p_ref, gt_ref, c0_ref, n0_ref, m0_ref, gh_ref, hh_ref, c_ref, n_ref, m_ref):
    scale = ML_DK ** -0.5
    hk, hv = ML_HEADS * ML_DK, ML_HEADS * ML_DV
    gates = gt_ref[0]
    logf = _log_sigmoid(gates)
    for h in range(ML_HEADS):
        q = p_ref[0, :, h * ML_DK:(h + 1) * ML_DK].astype(F32)
        k = p_ref[0, :, hk + h * ML_DK:hk + (h + 1) * ML_DK].astype(F32)
        v = p_ref[0, :, 2 * hk + h * ML_DV:2 * hk + (h + 1) * ML_DV].astype(F32)
        og = p_ref[0, :, 2 * hk + hv + h * ML_DV:2 * hk + hv + (h + 1) * ML_DV].astype(F32)
        log_i = gates[:, h:h + 1]
        m_prev = m0_ref[0, :, h:h + 1]
        c_prev = c0_ref[0, h]
        n_prev = n0_ref[0, h:h + 1, :]

        g = logf[:, ML_HEADS + h:ML_HEADS + h + 1] + m_prev
        m_t = jnp.maximum(g, log_i)
        w_intra = jnp.exp(log_i - m_t)
        w_inter = jnp.exp(g - m_t)
        s = jnp.sum(q * k, axis=1, keepdims=True) * (w_intra * scale)
        q_c = _row_to_col(q)
        num = s * v + w_inter * jnp.sum(q_c * c_prev, axis=0, keepdims=True)
        den = s + w_inter * jnp.sum(q * n_prev, axis=1, keepdims=True)
        cell = num / jnp.maximum(jnp.abs(den), jnp.exp(-m_t))

        kw = k * (w_intra * scale)
        c_ref[0, h] = w_inter * c_prev + _row_to_col(kw) * v
        n_ref[0, h:h + 1, :] = w_inter * n_prev + kw
        m_ref[0, :, h:h + 1] = m_t

        y = _rms(cell, gh_ref[:, h * ML_DV:(h + 1) * ML_DV])
        hh_ref[0, :, h * ML_DV:(h + 1) * ML_DV] = (y * _sigmoid(og)).astype(hh_ref.dtype)


def _mlstm_step(proj, gates, c0, n0, m0, g_head, batch):
    hv = ML_HEADS * ML_DV
    wp = proj.shape[1]
    one = lambda shape: pl.BlockSpec((1,) + shape, lambda b: (b,) + (0,) * len(shape))
    return pl.pallas_call(
        _mlstm_step_kernel,
        grid=(batch,),
        in_specs=[one((1, wp)), one((1, 128)), one((ML_HEADS, ML_DK, ML_DV)), one((ML_HEADS, ML_DK)),
                  one((1, ML_HEADS)), pl.BlockSpec((1, hv), lambda b: (0, 0))],
        out_specs=[one((1, hv)), one((ML_HEADS, ML_DK, ML_DV)), one((ML_HEADS, ML_DK)), one((1, ML_HEADS))],
        out_shape=[jax.ShapeDtypeStruct((batch, 1, hv), F32),
                   jax.ShapeDtypeStruct((batch, ML_HEADS, ML_DK, ML_DV), F32),
                   jax.ShapeDtypeStruct((batch, ML_HEADS, ML_DK), F32),
                   jax.ShapeDtypeStruct((batch, 1, ML_HEADS), F32)],
        compiler_params=_cparams("arbitrary"),
        name="mlstm_step",
    )(proj.reshape(batch, 1, wp), gates.reshape(batch, 1, 128), c0, n0, m0.reshape(batch, 1, ML_HEADS),
      g_head.reshape(1, hv))


def _group_reduce(x, op):
    e, tm = x.shape
    per = e // N_GROUPS
    x3 = x.reshape(N_GROUPS, per, tm)
    r = op(x3, axis=1, keepdims=True)
    return jnp.broadcast_to(r, (N_GROUPS, per, tm)).reshape(e, tm)


def _route(logits_t, bias_t):
    e, tm = logits_t.shape
    per = e // N_GROUPS
    scores = _sigmoid(logits_t)
    sel = scores + bias_t
    eidx = lax.broadcasted_iota(I32, (e, tm), 0)
    jidx = eidx % per
    gidx = eidx // per
    m1 = _group_reduce(sel, jnp.max)
    first1 = _group_reduce(jnp.where(sel == m1, jidx, per), jnp.min)
    m2 = _group_reduce(jnp.where(jidx == first1, NEG_INF, sel), jnp.max)
    gs = m1 + m2
    chosen = jnp.zeros((e, tm), F32)
    for _ in range(TOPK_GROUPS):
        mx = jnp.max(gs, axis=0, keepdims=True)
        f = jnp.min(jnp.where(gs == mx, gidx, N_GROUPS), axis=0, keepdims=True)
        hit = gidx == f
        chosen = jnp.where(hit, 1.0, chosen)
        gs = jnp.where(hit, NEG_INF, gs)
    cand = jnp.where(chosen > 0.0, sel, NEG_INF)
    picked = jnp.zeros((e, tm), F32)
    idx, wts = [], []
    for _ in range(TOP_K):
        mx = jnp.max(cand, axis=0, keepdims=True)
        f = jnp.min(jnp.where(cand == mx, eidx, e), axis=0, keepdims=True)
        hit = eidx == f
        idx.append(f)
        wts.append(jnp.sum(jnp.where(hit, scores, 0.0), axis=0, keepdims=True))
        cand = jnp.where(hit, NEG_INF, cand)
        picked = jnp.where(hit, 1.0, picked)
    return idx, wts, picked


def _post_kernel(x_ref, a_ref, wo_ref, gm_ref, gn_ref, sh_ref, sc_ref, gf_ref, wr_ref, br_ref,
                 wsgu_ref, wsd_ref,
                 xs_ref, hp_ref, idx_ref, wt_ref, rank_ref, cnt_ref):
    tm, d = x_ref.shape

    x1 = x_ref[...] + gm_ref[0] * _dot_w(a_ref[...], wo_ref[...])
    h2 = _rms_mod(x1, gn_ref[...], sh_ref[0], sc_ref[0])
    hp_ref[...] = h2.astype(BF16)

    ff = wsd_ref.shape[0]
    gu = _dot(h2, wsgu_ref[...])
    shared = _dot(_silu(gu[:, :ff]) * gu[:, ff:], wsd_ref[...])
    xs_ref[...] = x1 + gf_ref[0] * shared

    logits_t = _dot_w_nt(wr_ref[...], h2)
    idx, wts, picked = _route(logits_t, br_ref[:, 0:1])
    e = picked.shape[0]
    eidx = lax.broadcasted_iota(I32, (e, tm), 0)
    before = (lax.broadcasted_iota(I32, (tm, tm), 0) < lax.broadcasted_iota(I32, (tm, tm), 1)).astype(BF16)
    rank_t = jnp.dot(picked.astype(BF16), before, preferred_element_type=F32)
    total = wts[0]
    for k in range(1, TOP_K):
        total = total + wts[k]
    for k in range(TOP_K):
        idx_ref[k:k + 1, :] = idx[k]
        wt_ref[k:k + 1, :] = wts[k] / total * ROUTED_SCALE
        rank_ref[k:k + 1, :] = jnp.sum(jnp.where(eidx == idx[k], rank_t, 0.0), axis=0, keepdims=True).astype(I32)
    cnt_ref[0] = jnp.broadcast_to(jnp.sum(picked, axis=1, keepdims=True), cnt_ref.shape[1:])


def _post(tok, x, a, w_o, g_m, g_norm, shift, scale, g_f, w_router_t, b_router, ws_gu, ws_d):
    n, d = x.shape
    e = w_router_t.shape[0]
    ff = ws_d.shape[0]
    gm, gm_spec = tok.mod(g_m)
    sh, sh_spec = tok.mod(shift)
    sc, sc_spec = tok.mod(scale)
    gf, gf_spec = tok.mod(g_f)
    cols = lambda rows: pl.BlockSpec((rows, tok.tm), lambda i: (0, i))
    return pl.pallas_call(
        _post_kernel,
        grid=(tok.n_tiles,),
        in_specs=[tok.rows(d), tok.rows(a.shape[1]), _full(w_o.shape), gm_spec, _full((1, d)), sh_spec, sc_spec,
                  gf_spec, _full((e, d)), _full((e, 128)), _full((d, 2 * ff)), _full((ff, d))],
        out_specs=[tok.rows(d), tok.rows(d), cols(TOP_K), cols(TOP_K), cols(TOP_K),
                   pl.BlockSpec((1, e, LANES), lambda i: (i, 0, 0))],
        out_shape=[jax.ShapeDtypeStruct((n, d), F32), jax.ShapeDtypeStruct((n, d), BF16),
                   jax.ShapeDtypeStruct((TOP_K, n), I32), jax.ShapeDtypeStruct((TOP_K, n), F32),
                   jax.ShapeDtypeStruct((TOP_K, n), I32), jax.ShapeDtypeStruct((tok.n_tiles, e, LANES), F32)],
        compiler_params=_cparams("arbitrary"),
        name="post",
    )(x, a, w_o, gm, g_norm.reshape(1, d), sh, sc, gf, w_router_t,
      jnp.broadcast_to(b_router.reshape(e, 1), (e, 128)), ws_gu, ws_d)


MOE_TILE = 256
RUN_ALIGN = 16
POS_CHUNK = 512
TILE_ROWS = -(-(MOE_TILE * TOP_K + N_EXPERTS * (RUN_ALIGN - 1)) // POS_CHUNK) * POS_CHUNK
TILE_PIECES = TILE_ROWS // RUN_ALIGN
ZERO_SLAB_ROWS = 256


def _tile_positions(idx_ref, rank_ref, tsrc_ref, tile):
    idx = idx_ref[...]

    def add(e, acc):
        return acc + jnp.where(idx == e, tsrc_ref[tile, e], 0)

    return lax.fori_loop(0, N_EXPERTS, add, rank_ref[...])


def _start_piece(tile, j, pdst_ref, make_copy):
    make_copy(pl.multiple_of(j * RUN_ALIGN, RUN_ALIGN),
              pl.multiple_of(pdst_ref[tile * TILE_PIECES + j], RUN_ALIGN), RUN_ALIGN).start()


def _start_pieces(tile, pdst_ref, npc_ref, make_copy, first=0):
    def body(j, c):
        _start_piece(tile, j, pdst_ref, make_copy)
        return c

    lax.fori_loop(first, npc_ref[tile], body, 0)


FULL_TILE_PIECES = MOE_TILE * TOP_K // RUN_ALIGN
CHUNK_PIECES = POS_CHUNK // RUN_ALIGN


def _wait_pieces(tile, npc_ref, make_copy):
    total = npc_ref[tile] * RUN_ALIGN
    chunk = 1 << (TILE_ROWS.bit_length() - 1)
    while chunk >= RUN_ALIGN:
        @pl.when((total & chunk) != 0)
        def _():
            make_copy(0, 0, chunk).wait()
        chunk //= 2


def _dispatch_kernel(pe_ref, de_ref, tsrc_ref, pdst_ref, npc_ref, idx_ref, rank_ref, hpp_ref, hps_ref,
                     xs_ref, srt, zeros, sem):
    i = pl.program_id(0)
    last = pl.num_programs(0) - 1
    slot = i % 2
    tm = hpp_ref.shape[0]
    zrows = zeros.shape[0]

    def tail_slabs(e, wait):
        for s in range(EXPERT_ROWS // zrows):
            @pl.when(pe_ref[e] - s * zrows > de_ref[e])
            def _():
                start = pl.multiple_of(pe_ref[e] - (s + 1) * zrows, zrows)
                cp = pltpu.make_async_copy(zeros, xs_ref.at[pl.ds(start, zrows)], sem.at[2])
                cp.wait() if wait else cp.start()

    @pl.when(i == 0)
    def _():
        zeros[...] = jnp.zeros_like(zeros)

        def zstart(e, c):
            tail_slabs(e, False)
            return c

        def zwait(e, c):
            tail_slabs(e, True)
            return c

        lax.fori_loop(0, N_EXPERTS, zstart, 0)
        lax.fori_loop(0, N_EXPERTS, zwait, 0)

    pos = _tile_positions(idx_ref, rank_ref, tsrc_ref, i)
    x = jnp.where(i == last, hps_ref[...], hpp_ref[...])
    def sort_chunk(c):
        p_iota = c * POS_CHUNK + lax.broadcasted_iota(I32, (POS_CHUNK, tm), 0)
        hit = pos[0:1, :] == p_iota
        for k in range(1, TOP_K):
            hit = jnp.logical_or(hit, pos[k:k + 1, :] == p_iota)
        onehot = jnp.where(hit, 1.0, 0.0).astype(BF16)
        srt[slot, c * POS_CHUNK:(c + 1) * POS_CHUNK, :] = jnp.dot(onehot, x,
                                                                  preferred_element_type=F32).astype(BF16)

    def copier(buf_slot):
        def copy(s, d, rows):
            return pltpu.make_async_copy(srt.at[buf_slot, pl.ds(s, rows)], xs_ref.at[pl.ds(d, rows)],
                                         sem.at[buf_slot])
        return copy

    def sort_tile(full):
        always = tm * TOP_K // POS_CHUNK if full else 0
        for c in range(TILE_ROWS // POS_CHUNK):
            if c < always:
                sort_chunk(c)
                for j in range(c * CHUNK_PIECES, (c + 1) * CHUNK_PIECES):
                    _start_piece(i, j, pdst_ref, copier(slot))
            else:
                pl.when(npc_ref[i] * RUN_ALIGN > c * POS_CHUNK)(functools.partial(sort_chunk, c))
        _start_pieces(i, pdst_ref, npc_ref, copier(slot), first=FULL_TILE_PIECES if full else 0)

    pl.when(i < last)(functools.partial(sort_tile, True))
    pl.when(i == last)(functools.partial(sort_tile, False))

    @pl.when(i > 0)
    def _():
        _wait_pieces(i - 1, npc_ref, copier(1 - slot))

    @pl.when(i == last)
    def _():
        _wait_pieces(i, npc_ref, copier(slot))


def _dispatch(hp_p, hp_s, idx_all, rank_all, pad_end, data_end, tsrc, pdst, npc, n_rows):
    tm = MOE_TILE
    d = hp_p.shape[1]
    n_tiles = tsrc.shape[0]
    cols = pl.BlockSpec((TOP_K, tm), lambda i, *_: (0, i))
    return pl.pallas_call(
        _dispatch_kernel,
        grid_spec=pltpu.PrefetchScalarGridSpec(
            num_scalar_prefetch=5, grid=(n_tiles,),
            in_specs=[cols, cols,
                      pl.BlockSpec((tm, d), lambda i, *_: (jnp.minimum(i, n_tiles - 2), 0)),
                      pl.BlockSpec((tm, d), lambda i, *_: (0, 0))],
            out_specs=pl.BlockSpec(memory_space=pl.ANY),
            scratch_shapes=[pltpu.VMEM((2, TILE_ROWS, d), BF16), pltpu.VMEM((ZERO_SLAB_ROWS, d), BF16),
                            pltpu.SemaphoreType.DMA((3,))]),
        out_shape=jax.ShapeDtypeStruct((n_rows, d), BF16),
        compiler_params=_cparams("arbitrary"),
        name="dispatch",
    )(pad_end, data_end, tsrc, pdst, npc, idx_all, rank_all, hp_p, hp_s)


def _expert_kernel(be_ref, first_ref, nu_ref, x_ref, wg_ref, wu_ref, wd_ref, o_ref, wgu_s, wd_s):
    i = pl.program_id(0)
    ff = wd_s.shape[0]

    @pl.when(i < nu_ref[0])
    def _():
        @pl.when(first_ref[i] == 1)
        def _():
            wgu_s[:, :ff] = wg_ref[...].astype(BF16)
            wgu_s[:, ff:] = wu_ref[...].astype(BF16)
            wd_s[...] = wd_ref[...].astype(BF16)

        gu = jnp.dot(x_ref[...], wgu_s[...], preferred_element_type=F32)
        act = _silu(gu[:, :ff]) * gu[:, ff:]
        o_ref[...] = jnp.dot(act.astype(BF16), wd_s[...], preferred_element_type=F32).astype(o_ref.dtype)


def _experts(xs, block_e, first, n_used, w_gate, w_up, w_down, layer):
    n_rows, d = xs.shape
    ff = w_gate.shape[-1]
    bm = EXPERT_ROWS
    grid_spec = pltpu.PrefetchScalarGridSpec(
        num_scalar_prefetch=3,
        grid=(n_rows // bm,),
        in_specs=[pl.BlockSpec((bm, d), lambda i, be, fi, nu: (jnp.minimum(i, nu[0] - 1), 0)),
                  pl.BlockSpec((None, None, d, ff), lambda i, be, fi, nu: (layer, be[i], 0, 0)),
                  pl.BlockSpec((None, None, d, ff), lambda i, be, fi, nu: (layer, be[i], 0, 0)),
                  pl.BlockSpec((None, None, ff, d), lambda i, be, fi, nu: (layer, be[i], 0, 0))],
        out_specs=pl.BlockSpec((bm, d), lambda i, be, fi, nu: (jnp.minimum(i, nu[0] - 1), 0)),
        scratch_shapes=[pltpu.VMEM((d, 2 * ff), BF16), pltpu.VMEM((ff, d), BF16)],
    )
    return pl.pallas_call(
        _expert_kernel,
        grid_spec=grid_spec,
        out_shape=jax.ShapeDtypeStruct((n_rows, d), BF16),
        compiler_params=_cparams("arbitrary"),
        name="experts",
    )(block_e, first, n_used, xs, w_gate, w_up, w_down)


def _combine_kernel(tsrc_ref, pdst_ref, npc_ref, idx_ref, rank_ref, w_ref, xs_ref, gf_ref, gfin_ref, os_ref,
                    y_ref, buf, sem, *, tile0, clear, final_norm):
    i = pl.program_id(0)
    tile = tile0 + i
    slot = i % 2
    tm, d = xs_ref.shape

    def copier(buf_slot):
        def copy(s, d, rows):
            return pltpu.make_async_copy(os_ref.at[pl.ds(d, rows)], buf.at[buf_slot, pl.ds(s, rows)],
                                         sem.at[buf_slot])
        return copy

    @pl.when(i == 0)
    def _():
        first_free = 0 if clear else tm * TOP_K
        buf[:, first_free:, :] = jnp.zeros((2, buf.shape[1] - first_free, d), buf.dtype)
        _start_pieces(tile, pdst_ref, npc_ref, copier(0))

    full = not clear
    n_steps = pl.num_programs(0)
    nxt = jnp.minimum(tile + 1, tile0 + n_steps - 1)

    pos = _tile_positions(idx_ref, rank_ref, tsrc_ref, tile).astype(F32)
    w = w_ref[...]
    pos_c = [_row_to_col(pos[k:k + 1, :]) for k in range(TOP_K)]
    w_c = [_row_to_col(w[k:k + 1, :]) for k in range(TOP_K)]
    _wait_pieces(tile, npc_ref, copier(slot))

    def chunk_sum(c):
        p_iota = (c * POS_CHUNK + lax.broadcasted_iota(I32, (tm, POS_CHUNK), 1)).astype(F32)
        wmat = jnp.where(pos_c[0] == p_iota, w_c[0], 0.0)
        for k in range(1, TOP_K):
            wmat = wmat + jnp.where(pos_c[k] == p_iota, w_c[k], 0.0)
        return _dot(wmat, buf[slot, c * POS_CHUNK:(c + 1) * POS_CHUNK, :])

    always = tm * TOP_K // POS_CHUNK
    acc = None
    for c in range(always):
        if full:
            for j in range(c * CHUNK_PIECES, (c + 1) * CHUNK_PIECES):
                _start_piece(nxt, j, pdst_ref, copier(1 - slot))
        part = chunk_sum(c)
        acc = part if acc is None else acc + part
    if full:
        _start_pieces(nxt, pdst_ref, npc_ref, copier(1 - slot), first=FULL_TILE_PIECES)

        @pl.when(i == n_steps - 1)
        def _():
            _wait_pieces(nxt, npc_ref, copier(1 - slot))
    y_ref[...] = acc
    for c in range(always, TILE_ROWS // POS_CHUNK):
        @pl.when(npc_ref[tile] * RUN_ALIGN > c * POS_CHUNK)
        def _():
            y_ref[...] += chunk_sum(c)
    x2 = xs_ref[...] + gf_ref[0] * y_ref[...]
    y_ref[...] = _rms(x2, gfin_ref[...]) if final_norm else x2


def _combine(tok, tile0, xs, g_f, g_final, idx_all, rank_all, wt_all, out_sorted, tsrc, pdst, npc, final_norm,
             clear):
    n, d = xs.shape
    gf, gf_spec = tok.mod(g_f)
    tm = MOE_TILE
    cols = pl.BlockSpec((TOP_K, tm), lambda i, *_: (0, tile0 + i))
    return pl.pallas_call(
        functools.partial(_combine_kernel, tile0=tile0, clear=clear, final_norm=final_norm),
        grid_spec=pltpu.PrefetchScalarGridSpec(
            num_scalar_prefetch=3, grid=(tok.n_tiles,),
            in_specs=[cols, cols, cols, tok.rows(d), gf_spec, _full((1, d)), pl.BlockSpec(memory_space=pl.ANY)],
            out_specs=tok.rows(d),
            scratch_shapes=[pltpu.VMEM((2, TILE_ROWS, d), BF16), pltpu.SemaphoreType.DMA((2,))]),
        out_shape=jax.ShapeDtypeStruct((n, d), F32),
        compiler_params=_cparams("arbitrary"),
        name="combine",
    )(tsrc, pdst, npc, idx_all, rank_all, wt_all, xs, gf, g_final.reshape(1, d), out_sorted)


def _rope(x, cos2, sin2):
    half = x.shape[1] // 2
    swapped = jnp.concatenate([x[:, half:], x[:, :half]], axis=1)
    return x * cos2 + swapped * sin2


def _in1_kernel(x_ref, shk_ref, sck_ref, gk_ref, wdkv_ref, glat_ref, cos_ref, sin_ref,
                shm_ref, scm_ref, gm_ref, wdq_ref, gq_ref, wqn_ref, wqr_ref, wuk_ref,
                lat_ref, kr_ref, latb_ref, krb_ref, qa_ref, qr_ref):
    x = x_ref[...]
    cos2, sin2 = cos_ref[...], sin_ref[...]
    att_scale = (NOPE_DIM + ROPE_DIM) ** -0.5

    hk = _rms_mod(x, gk_ref[...], shk_ref[0], sck_ref[0])
    ckr = _dot_w(hk, wdkv_ref[...])
    lat = _rms(ckr[:, :KV_LORA], glat_ref[...])
    kr = _rope(ckr[:, KV_LORA:], cos2, sin2)
    lat_ref[...] = lat
    kr_ref[...] = kr
    latb_ref[...] = lat.astype(BF16)
    krb_ref[...] = kr.astype(BF16)

    hm = _rms_mod(x, gm_ref[...], shm_ref[0], scm_ref[0])
    q_lat = _rms(_dot_w(hm, wdq_ref[...]), gq_ref[...])
    q_nope = _dot_w(q_lat, wqn_ref[...])
    q_rope = _dot_w(q_lat, wqr_ref[...])
    for h in range(MLA_HEADS):
        qa = _dot_w(q_nope[:, h * NOPE_DIM:(h + 1) * NOPE_DIM], wuk_ref[h])
        qa_ref[h] = (qa * att_scale).astype(BF16)
        qr = _rope(q_rope[:, h * ROPE_DIM:(h + 1) * ROPE_DIM], cos2, sin2)
        qr_ref[h] = (qr * att_scale).astype(BF16)


def _in1(tok, x, sh_kv, sc_kv, g_kv_in, w_dkv, g_kv_lat, cos2, sin2, sh_m, sc_m, g_mix, w_dq, g_q_lat,
         w_q_nope, w_q_rope, w_uk_t):
    n, d = x.shape
    q_lora = w_dq.shape[1]
    shk, shk_spec = tok.mod(sh_kv)
    sck, sck_spec = tok.mod(sc_kv)
    shm, shm_spec = tok.mod(sh_m)
    scm, scm_spec = tok.mod(sc_m)
    heads = lambda w: pl.BlockSpec((MLA_HEADS, tok.tm, w), lambda i: (0, i, 0))
    return pl.pallas_call(
        _in1_kernel,
        grid=(tok.n_tiles,),
        in_specs=[tok.rows(d), shk_spec, sck_spec, _full((1, d)), _full(w_dkv.shape), _full((1, KV_LORA)),
                  tok.seq_rows(ROPE_DIM), tok.seq_rows(ROPE_DIM),
                  shm_spec, scm_spec, _full((1, d)), _full(w_dq.shape), _full((1, q_lora)),
                  _full(w_q_nope.shape), _full(w_q_rope.shape), _full(w_uk_t.shape)],
        out_specs=[tok.rows(KV_LORA), tok.rows(ROPE_DIM), tok.rows(KV_LORA), tok.rows(ROPE_DIM),
                   heads(KV_LORA), heads(ROPE_DIM)],
        out_shape=[jax.ShapeDtypeStruct((n, KV_LORA), F32), jax.ShapeDtypeStruct((n, ROPE_DIM), F32),
                   jax.ShapeDtypeStruct((n, KV_LORA), BF16), jax.ShapeDtypeStruct((n, ROPE_DIM), BF16),
                   jax.ShapeDtypeStruct((MLA_HEADS, n, KV_LORA), BF16),
                   jax.ShapeDtypeStruct((MLA_HEADS, n, ROPE_DIM), BF16)],
        compiler_params=_cparams("arbitrary"),
        name="in1",
    )(x, shk, sck, g_kv_in.reshape(1, d), w_dkv, g_kv_lat.reshape(1, KV_LORA), cos2, sin2,
      shm, scm, g_mix.reshape(1, d), w_dq, g_q_lat.reshape(1, q_lora), w_q_nope, w_q_rope, w_uk_t)


def _attn_prompt_kernel(qa_ref, qr_ref, lat_ref, kr_ref, wuv_ref, o_ref, m_s, l_s, acc_s):
    heads, tq, c = qa_ref.shape
    qi = pl.program_id(1)
    n_lt = tq // LANES
    m_s[...] = jnp.full(m_s.shape, NEG_INF, F32)
    l_s[...] = jnp.zeros(l_s.shape, F32)
    acc_s[...] = jnp.zeros(acc_s.shape, F32)
    causal = lax.broadcasted_iota(I32, (tq, tq), 1) <= lax.broadcasted_iota(I32, (tq, tq), 0)

    def chunk(j, diagonal):
        start = pl.multiple_of(j * tq, tq)
        kc = lat_ref[pl.ds(start, tq), :]
        krc = kr_ref[pl.ds(start, tq), :]
        def scores(h):
            return _dot_nt(qa_ref[h], kc) + _dot_nt(qr_ref[h], krc)

        s_next = scores(0)
        for h in range(heads):
            s, s_next = s_next, (scores(h + 1) if h + 1 < heads else None)
            if diagonal:
                s = jnp.where(causal, s, NEG_INF)
            tiles = [s[:, t * LANES:(t + 1) * LANES] for t in range(n_lt)]
            mx = tiles[0]
            for t in tiles[1:]:
                mx = jnp.maximum(mx, t)
            m_old = m_s[h]
            m_new = jnp.maximum(m_old, jnp.broadcast_to(jnp.max(mx, axis=1, keepdims=True), (tq, LANES)))
            alpha = jnp.exp(m_old - m_new)
            ps = [jnp.exp(t - m_new) for t in tiles]
            part = ps[0]
            for p in ps[1:]:
                part = part + p
            l_s[h] = alpha * l_s[h] + part
            pv = _dot(jnp.concatenate(ps, axis=1), kc)
            acc_s[h] = jnp.concatenate([alpha] * (c // LANES), axis=1) * acc_s[h] + pv
            m_s[h] = m_new

    def step(j, carry):
        chunk(j, False)
        return carry

    lax.fori_loop(0, qi, step, 0)
    chunk(qi, True)
    for h in range(heads):
        o_lat = acc_s[h] / jnp.sum(l_s[h], axis=1, keepdims=True)
        o_ref[:, h * V_DIM:(h + 1) * V_DIM] = _dot(o_lat, wuv_ref[h]).astype(o_ref.dtype)


def _attn_prompt(qa, qr, latb, krb, w_uv_t, batch, seq, tq=256):
    tq = min(tq, seq)
    nq = seq // tq
    return pl.pallas_call(
        _attn_prompt_kernel,
        grid=(batch, nq),
        in_specs=[pl.BlockSpec((MLA_HEADS, tq, KV_LORA), lambda b, i: (0, b * nq + i, 0)),
                  pl.BlockSpec((MLA_HEADS, tq, ROPE_DIM), lambda b, i: (0, b * nq + i, 0)),
                  pl.BlockSpec((seq, KV_LORA), lambda b, i: (b, 0)),
                  pl.BlockSpec((seq, ROPE_DIM), lambda b, i: (b, 0)),
                  pl.BlockSpec(w_uv_t.shape, lambda b, i: (0, 0, 0))],
        out_specs=pl.BlockSpec((tq, MLA_HEADS * V_DIM), lambda b, i: (b * nq + i, 0)),
        out_shape=jax.ShapeDtypeStruct((batch * seq, MLA_HEADS * V_DIM), BF16),
        scratch_shapes=[pltpu.VMEM((MLA_HEADS, tq, LANES), F32), pltpu.VMEM((MLA_HEADS, tq, LANES), F32),
                        pltpu.VMEM((MLA_HEADS, tq, KV_LORA), F32)],
        compiler_params=_cparams("arbitrary", "arbitrary"),
        name="attn_prompt",
    )(qa, qr, latb, krb, w_uv_t)


def _attn_decode_kernel(pt_ref, qa_ref, qr_ref, lat1_ref, kr1_ref, wuv_ref, *rest, pages):
    lat_refs = rest[:pages]
    kr_refs = rest[pages:2 * pages]
    o_ref, m_s, l_s, acc_s = rest[2 * pages:]
    j = pl.program_id(1)
    qa = qa_ref[0]
    qr = qr_ref[0]

    @pl.when(j == 0)
    def _():
        lat1 = lat1_ref[0].astype(F32)
        m_s[...] = (jnp.sum(qa.astype(F32) * lat1, axis=1, keepdims=True)
                    + jnp.sum(qr.astype(F32) * kr1_ref[0].astype(F32), axis=1, keepdims=True))
        l_s[...] = jnp.ones(l_s.shape, F32)
        acc_s[...] = jnp.broadcast_to(lat1, acc_s.shape)

    lats = [r[0].astype(BF16) for r in lat_refs]
    s = jnp.concatenate([_dot_nt(qa, lats[p]) + _dot(qr, kr_refs[p][0]) for p in range(pages)], axis=1)
    m_old = m_s[...]
    m_new = jnp.maximum(m_old, jnp.max(s, axis=1, keepdims=True))
    alpha = jnp.exp(m_old - m_new)
    p_all = jnp.exp(s - m_new)
    l_s[...] = alpha * l_s[...] + jnp.sum(p_all, axis=1, keepdims=True)
    page = lats[0].shape[0]
    pv = _dot(p_all[:, :page], lats[0])
    for p in range(1, pages):
        pv = pv + _dot(p_all[:, p * page:(p + 1) * page], lats[p])
    acc_s[...] = alpha * acc_s[...] + pv
    m_s[...] = m_new

    @pl.when(j == pl.num_programs(1) - 1)
    def _():
        o_all = _dot(acc_s[...] / l_s[...], wuv_ref[...])
        for h in range(o_all.shape[0]):
            o_ref[0, :, h * V_DIM:(h + 1) * V_DIM] = o_all[h:h + 1, h * V_DIM:(h + 1) * V_DIM].astype(o_ref.dtype)


def _attn_decode(qa, qr, lat1, kr1, w_uv_flat, cache_latent, cache_krope_t, page_table, pages=64):
    batch, n_pages = page_table.shape
    pages = min(pages, n_pages)
    page = cache_latent.shape[1]
    steps = n_pages // pages
    one = lambda shape: pl.BlockSpec((1,) + shape, lambda b, j, pt: (b,) + (0,) * len(shape))
    paged = lambda w, p: pl.BlockSpec((1, page, w) if w == KV_LORA else (1, w, page),
                                      lambda b, j, pt: (pt[b, j * pages + p], 0, 0))
    grid_spec = pltpu.PrefetchScalarGridSpec(
        num_scalar_prefetch=1,
        grid=(batch, steps),
        in_specs=[one((MLA_HEADS, KV_LORA)), one((MLA_HEADS, ROPE_DIM)), one((1, KV_LORA)), one((1, ROPE_DIM)),
                  pl.BlockSpec(w_uv_flat.shape, lambda b, j, pt: (0, 0))]
                 + [paged(KV_LORA, p) for p in range(pages)] + [paged(ROPE_DIM, p) for p in range(pages)],
        out_specs=one((1, MLA_HEADS * V_DIM)),
        scratch_shapes=[pltpu.VMEM((MLA_HEADS, 1), F32), pltpu.VMEM((MLA_HEADS, 1), F32),
                        pltpu.VMEM((MLA_HEADS, KV_LORA), F32)],
    )
    return pl.pallas_call(
        functools.partial(_attn_decode_kernel, pages=pages),
        grid_spec=grid_spec,
        out_shape=jax.ShapeDtypeStruct((batch, 1, MLA_HEADS * V_DIM), BF16),
        compiler_params=_cparams("arbitrary", "arbitrary"),
        name="attn_decode",
    )(page_table, qa, qr, lat1, kr1, w_uv_flat, *([cache_latent] * pages), *([cache_krope_t] * pages))


def _rope_tables(pos):
    half = ROPE_DIM // 2
    inv = ROPE_THETA ** (-jnp.arange(half, dtype=F32) / half)
    ang = pos.astype(F32)[:, None] * inv[None, :]
    cos, sin = jnp.cos(ang), jnp.sin(ang)
    return jnp.concatenate([cos, cos], axis=1), jnp.concatenate([-sin, sin], axis=1)


def _routing_tables(tile_counts, n_rows):
    bm = EXPERT_ROWS
    tile_counts = (tile_counts + RUN_ALIGN - 1) // RUN_ALIGN * RUN_ALIGN
    counts = jnp.sum(tile_counts, axis=0)
    padded = (counts + bm - 1) // bm * bm
    pad_end = jnp.cumsum(padded).astype(I32)
    pad_start = pad_end - padded
    tile_dst = pad_start[None, :] + jnp.cumsum(tile_counts, axis=0) - tile_counts
    tile_src = jnp.cumsum(tile_counts, axis=1) - tile_counts
    run_end = tile_src + tile_counts
    piece_row = jnp.arange(TILE_PIECES, dtype=I32) * RUN_ALIGN
    piece_e = jnp.sum((run_end[:, None, :] <= piece_row[None, :, None]).astype(I32), axis=2)
    owner = (piece_e[:, :, None] == jnp.arange(N_EXPERTS, dtype=I32)[None, None, :]).astype(I32)
    piece_dst = jnp.sum(owner * (tile_dst - tile_src)[:, None, :], axis=2) + piece_row[None, :]
    n_pieces = run_end[:, -1] // RUN_ALIGN
    block_start = jnp.arange(n_rows // bm, dtype=I32) * bm
    block_e = jnp.sum((pad_end[None, :] <= block_start[:, None]).astype(I32), axis=1)
    block_e = jnp.minimum(block_e, N_EXPERTS - 1)
    first = jnp.concatenate([jnp.ones((1,), I32), (block_e[1:] != block_e[:-1]).astype(I32)])
    n_used = pad_end[-1:] // bm
    data_end = (pad_start + counts).astype(I32)
    return (pad_end, data_end, tile_src.astype(I32), piece_dst.reshape(-1).astype(I32), n_pieces.astype(I32),
            block_e, first, n_used)


def kernel(x_prompt, x_sample, state_mlstm_C, state_mlstm_n, state_mlstm_m, cache_latent, cache_krope, page_table, c_prompt, c_sample, w_ada, b_ada, g_mix, g_ffn, w_mlstm_in, b_mlstm_gates, g_mlstm_head, w_mlstm_out, w_ada_kv, b_ada_kv, g_kv_in, w_dkv, g_kv_lat, w_uk, w_uv, w_dq, g_q_lat, w_uq, w_mla_out, w_router, b_router, w_exp_gate, w_exp_up, w_exp_down, w_sh_gate, w_sh_up, w_sh_down, g_final):
    bp, tp, d = x_prompt.shape
    bs, ts, _ = x_sample.shape
    depth = w_ada.shape[0]
    n_a = w_mlstm_in.shape[0]
    past_len = page_table.shape[1] * cache_latent.shape[1]
    hk, hv = ML_HEADS * ML_DK, ML_HEADS * ML_DV
    n_p, n_s = bp * tp, bs * ts
    n_tot = n_p + n_s
    n_moe_tiles = n_p // MOE_TILE + 1
    n_rows = n_tot * TOP_K + n_moe_tiles * N_EXPERTS * (RUN_ALIGN - 1) + N_EXPERTS * (EXPERT_ROWS - 1)
    n_rows = -(-n_rows // EXPERT_ROWS) * EXPERT_ROWS

    groups = [
        dict(batch=bp, seq=tp, tok=_Tok(bp, tp, 256), x=x_prompt.reshape(n_p, d), pos0=0,
             C0=jnp.zeros((n_a, bp, ML_HEADS, ML_DK, ML_DV), F32), n0=jnp.zeros((n_a, bp, ML_HEADS, ML_DK), F32),
             m0=jnp.full((n_a, bp, ML_HEADS), M_INIT, F32)),
        dict(batch=bs, seq=ts, tok=_Tok(bs, ts, 256), x=x_sample.reshape(n_s, d), pos0=past_len,
             C0=state_mlstm_C, n0=state_mlstm_n, m0=state_mlstm_m),
    ]
    c_all = jnp.concatenate([c_prompt, c_sample], axis=0)
    row0 = [0, bp]
    for g in groups:
        g["Cs"], g["ns"], g["ms"] = [], [], []

    for g in groups:
        g["cos2"], g["sin2"] = _rope_tables(g["pos0"] + jnp.arange(g["seq"], dtype=I32))
        if g["seq"] == 1:
            g["cos2"] = jnp.broadcast_to(g["cos2"], (g["batch"], ROPE_DIM))
            g["sin2"] = jnp.broadcast_to(g["sin2"], (g["batch"], ROPE_DIM))

    mod_kv = _ada(c_all, w_ada_kv, b_ada_kv)
    w_uk_t = jnp.transpose(w_uk, (1, 2, 0))
    w_uv_t = jnp.transpose(w_uv, (1, 0, 2)).astype(BF16)

    for l in range(depth):
        mod = _ada(c_all, w_ada, b_ada[l], layer=l)
        w_router_t = w_router[l].T
        ws_gu = jnp.concatenate([w_sh_gate[l], w_sh_up[l]], axis=1).astype(BF16)
        ws_d = w_sh_down[l].astype(BF16)
        for gi, g in enumerate(groups):
            r0, nb = row0[gi], g["batch"]
            sh_m, sc_m, g_m, sh_f, sc_f, g_f = [mod[r0:r0 + nb, i * d:(i + 1) * d] for i in range(6)]
            g["g_f"] = g_f
            tok = g["tok"]
            wdt = F32 if g["seq"] == 1 else BF16
            if l < n_a:
                w_in = w_mlstm_in[l]
                w_main = w_in[:, :2 * hk + 2 * hv].astype(wdt)
                w_gates = jnp.pad(w_in[:, 2 * hk + 2 * hv:], ((0, 0), (0, 128 - 2 * ML_HEADS)))
                b_gates = jnp.pad(b_mlstm_gates[l], (0, 128 - 2 * ML_HEADS)).reshape(1, 128)
                if g["seq"] == 1:
                    proj, gates = _in0(tok, g["x"], sh_m, sc_m, g_mix[l], w_main, w_gates, b_gates, F32)
                    a, Cn, nn, mn = _mlstm_step(proj, gates, g["C0"][l], g["n0"][l], g["m0"][l],
                                                g_mlstm_head[l], nb)
                    a = a.reshape(nb, hv)
                else:
                    proj, gates = _in0(tok, g["x"], sh_m, sc_m, g_mix[l], w_main, w_gates, b_gates, BF16)
                    a, Cn, nn, mn = _mlstm_chunks(proj, gates, g["C0"][l], g["n0"][l], g["m0"][l],
                                                  g_mlstm_head[l], nb, g["seq"])
                g["Cs"].append(Cn)
                g["ns"].append(nn)
                g["ms"].append(mn.reshape(nb, ML_HEADS))
                w_o = w_mlstm_out[l].astype(wdt)
            else:
                j = l - n_a
                w_q = w_uq[j].reshape(-1, MLA_HEADS, NOPE_DIM + ROPE_DIM)
                w_q_nope = w_q[:, :, :NOPE_DIM].reshape(-1, MLA_HEADS * NOPE_DIM).astype(wdt)
                w_q_rope = w_q[:, :, NOPE_DIM:].reshape(-1, MLA_HEADS * ROPE_DIM).astype(wdt)
                sh_kv, sc_kv = mod_kv[r0:r0 + nb, :d], mod_kv[r0:r0 + nb, d:]
                lat, kr, latb, krb, qa, qr = _in1(tok, g["x"], sh_kv, sc_kv, g_kv_in, w_dkv.astype(wdt), g_kv_lat,
                                                  g["cos2"], g["sin2"], sh_m, sc_m, g_mix[l], w_dq[j].astype(wdt),
                                                  g_q_lat[j], w_q_nope, w_q_rope, w_uk_t.astype(wdt))
                if l == n_a:
                    g["lat"], g["kr"], g["latb"], g["krb"] = lat, kr, latb, krb
                if g["seq"] == 1:
                    a = _attn_decode(jnp.transpose(qa, (1, 0, 2)), jnp.transpose(qr, (1, 0, 2)),
                                     g["latb"].reshape(nb, 1, KV_LORA), g["krb"].reshape(nb, 1, ROPE_DIM),
                                     w_uv.reshape(KV_LORA, MLA_HEADS * V_DIM).astype(BF16),
                                     cache_latent, jnp.swapaxes(cache_krope, 1, 2),
                                     page_table).reshape(nb, MLA_HEADS * V_DIM)
                else:
                    a = _attn_prompt(qa, qr, g["latb"], g["krb"], w_uv_t, nb, g["seq"])
                w_o = w_mla_out[j].astype(wdt)
            g["xs"], g["hp"], g["idx"], g["wt"], g["rank"], g["cnt"] = _post(
                tok, g["x"], a, w_o, g_m, g_ffn[l], sh_f, sc_f, g_f, w_router_t.astype(wdt), b_router[l], ws_gu, ws_d)

        gp, gs = groups
        pad_t = MOE_TILE - n_s
        lanes = lambda v, fill: jnp.concatenate(
            [gp[v], jnp.pad(gs[v], ((0, 0), (0, pad_t)), constant_values=fill)], axis=1)
        idx_all, wt_all = lanes("idx", 0), lanes("wt", 0)
        rank_all = lanes("rank", TILE_ROWS)
        tile_counts = jnp.concatenate([gp["cnt"][:, :, 0], gs["cnt"][:, :, 0]], axis=0).astype(I32)
        pad_end, data_end, tsrc, pdst, npc, block_e, first, n_used = _routing_tables(tile_counts, n_rows)
        rows = lambda v: jnp.pad(v, ((0, pad_t), (0, 0)))
        xs_sorted = _dispatch(gp["hp"], rows(gs["hp"]), idx_all, rank_all, pad_end, data_end, tsrc, pdst, npc, n_rows)
        out_sorted = _experts(xs_sorted, block_e, first, n_used, w_exp_gate, w_exp_up, w_exp_down, l)
        last = l == depth - 1
        gp["x"] = _combine(gp["tok"], 0, gp["xs"], gp["g_f"], g_final, idx_all, rank_all, wt_all, out_sorted,
                           tsrc, pdst, npc, last, clear=False)
        gs["x"] = _combine(_Tok(MOE_TILE, 1, MOE_TILE), gp["tok"].n_tiles, rows(gs["xs"]), rows(gs["g_f"]), g_final,
                           idx_all, rank_all, wt_all, out_sorted, tsrc, pdst, npc, last, clear=True)[:n_s]

    outs = []
    for g in groups:
        nb, seq = g["batch"], g["seq"]
        outs.append((g["x"].reshape(nb, seq, d), jnp.stack(g["Cs"]), jnp.stack(g["ns"]), jnp.stack(g["ms"]),
                     g["lat"].reshape(nb, seq, KV_LORA), g["kr"].reshape(nb, seq, ROPE_DIM)))
    p, s = outs
    return (p[0], s[0], p[1], p[2], p[3], p[4], p[5], s[1], s[2], s[3], s[4], s[5])
```

```python
import functools

import jax
import jax.numpy as jnp
from jax import lax
from jax.experimental import pallas as pl
from jax.experimental.pallas import tpu as pltpu

F32 = jnp.float32
BF16 = jnp.bfloat16
I32 = jnp.int32
HIGHEST = lax.Precision.HIGHEST
NEG_INF = float("-inf")

ML_HEADS = 4
ML_DK = 128
ML_DV = 256
ML_CHUNK = 128
M_INIT = -1e30
MLA_HEADS = 8
NOPE_DIM = 128
ROPE_DIM = 64
V_DIM = 128
KV_LORA = 256
ROPE_THETA = 10000.0
N_EXPERTS = 64
TOP_K = 8
N_GROUPS = 8
TOPK_GROUPS = 4
ROUTED_SCALE = 2.5
NORM_EPS = 1e-6

LANES = 128
EXPERT_ROWS = 1024
VMEM_LIMIT = 56 * 1024 * 1024


def _cparams(*sem):
    return pltpu.CompilerParams(dimension_semantics=sem, vmem_limit_bytes=VMEM_LIMIT)


def _dot(a, b):
    return jnp.dot(a.astype(BF16), b.astype(BF16), preferred_element_type=F32)


def _dot_nt(a, b):
    return lax.dot_general(a.astype(BF16), b.astype(BF16), (((1,), (1,)), ((), ())),
                           preferred_element_type=F32)


def _dot_f32(a, b):
    return jnp.dot(a, b, precision=HIGHEST, preferred_element_type=F32)


def _dot_w(a, w):
    return _dot_f32(a.astype(F32), w) if w.dtype == F32 else _dot(a, w)


def _dot_w_nt(w, a):
    if w.dtype == F32:
        return lax.dot_general(w, a.astype(F32), (((1,), (1,)), ((), ())), precision=HIGHEST,
                               preferred_element_type=F32)
    return _dot_nt(w, a)


def _rms(x, g):
    return x * lax.rsqrt(jnp.mean(x * x, axis=-1, keepdims=True) + NORM_EPS) * g


def _rms_mod(x, g, shift, scale):
    return _rms(x, g) * (1.0 + scale) + shift


def _sigmoid(x):
    return 1.0 / (1.0 + jnp.exp(-x))


def _silu(x):
    return x * _sigmoid(x)


def _log_sigmoid(x):
    return jnp.minimum(x, 0.0) - jnp.log1p(jnp.exp(-jnp.abs(x)))


def _ada_kernel(c_ref, w_ref, b_ref, o_ref):
    o_ref[...] = _dot_f32(_silu(c_ref[...]), w_ref[...]) + b_ref[...]


def _ada(c, w, b, layer=None, tn=1024):
    m, d = c.shape
    n_out = w.shape[-1]
    if layer is None:
        w_spec = pl.BlockSpec((d, tn), lambda j: (0, j))
    else:
        w_spec = pl.BlockSpec((None, d, tn), lambda j: (layer, 0, j))
    return pl.pallas_call(
        _ada_kernel,
        grid=(n_out // tn,),
        in_specs=[pl.BlockSpec((m, d), lambda j: (0, 0)), w_spec, pl.BlockSpec((1, tn), lambda j: (0, j))],
        out_specs=pl.BlockSpec((m, tn), lambda j: (0, j)),
        out_shape=jax.ShapeDtypeStruct((m, n_out), F32),
        compiler_params=_cparams("arbitrary"),
        name="ada",
    )(c, w, b.reshape(1, n_out))


class _Tok:
    def __init__(self, batch, seq, tile):
        if seq == 1:
            self.tm, self.n_tiles, self.per_seq = batch, 1, 1
            self.mod_block = (1, batch, None)
        else:
            self.tm = min(tile, seq)
            self.per_seq = seq // self.tm
            self.n_tiles = batch * self.per_seq
            self.mod_block = (1, 1, None)
        self.batch, self.seq = batch, seq

    def mod(self, v):
        d = v.shape[-1]
        if self.seq == 1:
            return v.reshape(1, self.batch, d), pl.BlockSpec((1, self.batch, d), lambda i, *_: (0, 0, 0))
        per = self.per_seq
        return v.reshape(self.batch, 1, d), pl.BlockSpec((1, 1, d), lambda i, *_: (i // per, 0, 0))

    def rows(self, width):
        return pl.BlockSpec((self.tm, width), lambda i, *_: (i, 0))

    def seq_rows(self, width):
        per = self.per_seq
        return pl.BlockSpec((self.tm, width), lambda i, *_: (i % per, 0))


def _full(shape):
    nd = len(shape)
    return pl.BlockSpec(shape, lambda i, *_: (0,) * nd)


def _in0_kernel(x_ref, sh_ref, sc_ref, g_ref, w_ref, wg_ref, bg_ref, o_ref, gt_ref):
    h = _rms_mod(x_ref[...], g_ref[...], sh_ref[0], sc_ref[0])
    o_ref[...] = _dot_w(h, w_ref[...]).astype(o_ref.dtype)
    gt_ref[...] = _dot_f32(h, wg_ref[...]) + bg_ref[...]


def _in0(tok, x, shift, scale, g, w, w_gates, b_gates, out_dtype):
    n, d = x.shape
    wo = w.shape[1]
    sh, sh_spec = tok.mod(shift)
    sc, sc_spec = tok.mod(scale)
    return pl.pallas_call(
        _in0_kernel,
        grid=(tok.n_tiles,),
        in_specs=[tok.rows(d), sh_spec, sc_spec, _full((1, d)), _full((d, wo)), _full((d, 128)), _full((1, 128))],
        out_specs=[tok.rows(wo), tok.rows(128)],
        out_shape=[jax.ShapeDtypeStruct((n, wo), out_dtype), jax.ShapeDtypeStruct((n, 128), F32)],
        compiler_params=_cparams("arbitrary"),
        name="in0",
    )(x, sh, sc, g.reshape(1, d), w, w_gates, b_gates)


def _mlstm_chunk_kernel(q_ref, k_ref, v_ref, og_ref, gt_ref, c0_ref, n0_ref, m0_ref, gh_ref,
                        hh_ref, c_ref, n_ref, m_ref):
    L = q_ref.shape[0]
    scale = ML_DK ** -0.5

    @pl.when(pl.program_id(1) == 0)
    def _():
        c_ref[...] = c0_ref[...]
        n_ref[...] = n0_ref[...]
        m_ref[...] = m0_ref[...]

    gates = gt_ref[...]
    logf = _log_sigmoid(gates)
    row = lax.broadcasted_iota(I32, (L, L), 0)
    col = lax.broadcasted_iota(I32, (L, L), 1)
    causal = col <= row
    tril = causal.astype(F32)
    b_cols = _dot_f32(tril, logf)
    gates_t = gates.T
    b_rows = _dot_f32(logf.T, (row <= col).astype(F32))

    for h in range(ML_HEADS):
        i_row = gates_t[h:h + 1, :]
        i_col = gates[:, h:h + 1]
        b_row = b_rows[ML_HEADS + h:ML_HEADS + h + 1, :]
        b_col = b_cols[:, ML_HEADS + h:ML_HEADS + h + 1]
        m_prev = m_ref[0, :, h:h + 1]
        c_prev = c_ref[0, h]
        n_prev = n_ref[0, h:h + 1, :]
        q = q_ref[:, h * ML_DK:(h + 1) * ML_DK]
        k = k_ref[:, h * ML_DK:(h + 1) * ML_DK]
        v = v_ref[:, h * ML_DV:(h + 1) * ML_DV]

        g_col = b_col + m_prev
        dmat = jnp.where(causal, b_col - b_row + i_row, NEG_INF)
        m_t = jnp.maximum(g_col, jnp.max(dmat, axis=1, keepdims=True))
        w_intra = jnp.exp(dmat - m_t)
        w_inter = jnp.exp(g_col - m_t)
        s = _dot_nt(q, k) * (w_intra * scale)
        num = _dot(s, v) + w_inter * _dot(q, c_prev)
        qn = jnp.sum(q.astype(F32) * n_prev, axis=1, keepdims=True)
        den = jnp.sum(s, axis=1, keepdims=True) + w_inter * qn
        cell = num / jnp.maximum(jnp.abs(den), jnp.exp(-m_t))

        b_last = b_col[L - 1:L, :]
        m_new = m_t[L - 1:L, :]
        a_inter = jnp.exp(b_last + m_prev - m_new)
        a_intra = jnp.exp(b_last - b_col + i_col - m_new) * scale
        kw = k.astype(F32) * a_intra
        c_ref[0, h] = a_inter * c_prev + _dot(kw.T, v)
        n_ref[0, h:h + 1, :] = a_inter * n_prev + jnp.sum(kw, axis=0, keepdims=True)
        m_ref[0, :, h:h + 1] = m_new

        y = _rms(cell, gh_ref[:, h * ML_DV:(h + 1) * ML_DV])
        og = og_ref[:, h * ML_DV:(h + 1) * ML_DV].astype(F32)
        hh_ref[:, h * ML_DV:(h + 1) * ML_DV] = (y * _sigmoid(og)).astype(hh_ref.dtype)


def _mlstm_chunks(proj, gates, c0, n0, m0, g_head, batch, seq):
    L = seq if seq <= ML_CHUNK else ML_CHUNK
    nc = seq // L
    hk, hv = ML_HEADS * ML_DK, ML_HEADS * ML_DV
    rows = lambda w, cb: pl.BlockSpec((L, w), lambda b, j: (b * nc + j, cb))
    state = lambda shape: pl.BlockSpec((1,) + shape, lambda b, j: (b,) + (0,) * len(shape))
    return pl.pallas_call(
        _mlstm_chunk_kernel,
        grid=(batch, nc),
        in_specs=[rows(hk, 0), rows(hk, 1), rows(hv, 2 * hk // hv), rows(hv, 2 * hk // hv + 1), rows(128, 0),
                  state((ML_HEADS, ML_DK, ML_DV)), state((ML_HEADS, ML_DK)), state((1, ML_HEADS)),
                  pl.BlockSpec((1, hv), lambda b, j: (0, 0))],
        out_specs=[rows(hv, 0), state((ML_HEADS, ML_DK, ML_DV)), state((ML_HEADS, ML_DK)), state((1, ML_HEADS))],
        out_shape=[jax.ShapeDtypeStruct((batch * seq, hv), BF16),
                   jax.ShapeDtypeStruct((batch, ML_HEADS, ML_DK, ML_DV), F32),
                   jax.ShapeDtypeStruct((batch, ML_HEADS, ML_DK), F32),
                   jax.ShapeDtypeStruct((batch, 1, ML_HEADS), F32)],
        compiler_params=_cparams("arbitrary", "arbitrary"),
        name="mlstm_chunks",
    )(proj, proj, proj, proj, gates, c0, n0, m0.reshape(batch, 1, ML_HEADS), g_head.reshape(1, hv))


def _row_to_col(r):
    n = r.shape[1]
    eye = lax.broadcasted_iota(I32, (n, n), 0) == lax.broadcasted_iota(I32, (n, n), 1)
    return jnp.sum(jnp.where(eye, jnp.broadcast_to(r, (n, n)), 0.0), axis=1, keepdims=True)


def _mlstm_step_kernel(p_ref, gt_ref, c0_ref, n0_ref, m0_ref, gh_ref, hh_ref, c_ref, n_ref, m_ref):
    scale = ML_DK ** -0.5
    hk, hv = ML_HEADS * ML_DK, ML_HEADS * ML_DV
    gates = gt_ref[0]
    logf = _log_sigmoid(gates)
    for h in range(ML_HEADS):
        q = p_ref[0, :, h * ML_DK:(h + 1) * ML_DK].astype(F32)
        k = p_ref[0, :, hk + h * ML_DK:hk + (h + 1) * ML_DK].astype(F32)
        v = p_ref[0, :, 2 * hk + h * ML_DV:2 * hk + (h + 1) * ML_DV].astype(F32)
        og = p_ref[0, :, 2 * hk + hv + h * ML_DV:2 * hk + hv + (h + 1) * ML_DV].astype(F32)
        log_i = gates[:, h:h + 1]
        m_prev = m0_ref[0, :, h:h + 1]
        c_prev = c0_ref[0, h]
        n_prev = n0_ref[0, h:h + 1, :]

        g = logf[:, ML_HEADS + h:ML_HEADS + h + 1] + m_prev
        m_t = jnp.maximum(g, log_i)
        w_intra = jnp.exp(log_i - m_t)
        w_inter = jnp.exp(g - m_t)
        s = jnp.sum(q * k, axis=1, keepdims=True) * (w_intra * scale)
        q_c = _row_to_col(q)
        num = s * v + w_inter * jnp.sum(q_c * c_prev, axis=0, keepdims=True)
        den = s + w_inter * jnp.sum(q * n_prev, axis=1, keepdims=True)
        cell = num / jnp.maximum(jnp.abs(den), jnp.exp(-m_t))

        kw = k * (w_intra * scale)
        c_ref[0, h] = w_inter * c_prev + _row_to_col(kw) * v
        n_ref[0, h:h + 1, :] = w_inter * n_prev + kw
        m_ref[0, :, h:h + 1] = m_t

        y = _rms(cell, gh_ref[:, h * ML_DV:(h + 1) * ML_DV])
        hh_ref[0, :, h * ML_DV:(h + 1) * ML_DV] = (y * _sigmoid(og)).astype(hh_ref.dtype)


def _mlstm_step(proj, gates, c0, n0, m0, g_head, batch):
    hv = ML_HEADS * ML_DV
    wp = proj.shape[1]
    one = lambda shape: pl.BlockSpec((1,) + shape, lambda b: (b,) + (0,) * len(shape))
    return pl.pallas_call(
        _mlstm_step_kernel,
        grid=(batch,),
        in_specs=[one((1, wp)), one((1, 128)), one((ML_HEADS, ML_DK, ML_DV)), one((ML_HEADS, ML_DK)),
                  one((1, ML_HEADS)), pl.BlockSpec((1, hv), lambda b: (0, 0))],
        out_specs=[one((1, hv)), one((ML_HEADS, ML_DK, ML_DV)), one((ML_HEADS, ML_DK)), one((1, ML_HEADS))],
        out_shape=[jax.ShapeDtypeStruct((batch, 1, hv), F32),
                   jax.ShapeDtypeStruct((batch, ML_HEADS, ML_DK, ML_DV), F32),
                   jax.ShapeDtypeStruct((batch, ML_HEADS, ML_DK), F32),
                   jax.ShapeDtypeStruct((batch, 1, ML_HEADS), F32)],
        compiler_params=_cparams("arbitrary"),
        name="mlstm_step",
    )(proj.reshape(batch, 1, wp), gates.reshape(batch, 1, 128), c0, n0, m0.reshape(batch, 1, ML_HEADS),
      g_head.reshape(1, hv))


def _group_reduce(x, op):
    e, tm = x.shape
    per = e // N_GROUPS
    x3 = x.reshape(N_GROUPS, per, tm)
    r = op(x3, axis=1, keepdims=True)
    return jnp.broadcast_to(r, (N_GROUPS, per, tm)).reshape(e, tm)


def _route(logits_t, bias_t):
    e, tm = logits_t.shape
    per = e // N_GROUPS
    scores = _sigmoid(logits_t)
    sel = scores + bias_t
    eidx = lax.broadcasted_iota(I32, (e, tm), 0)
    jidx = eidx % per
    gidx = eidx // per
    m1 = _group_reduce(sel, jnp.max)
    first1 = _group_reduce(jnp.where(sel == m1, jidx, per), jnp.min)
    m2 = _group_reduce(jnp.where(jidx == first1, NEG_INF, sel), jnp.max)
    gs = m1 + m2
    chosen = jnp.zeros((e, tm), F32)
    for _ in range(TOPK_GROUPS):
        mx = jnp.max(gs, axis=0, keepdims=True)
        f = jnp.min(jnp.where(gs == mx, gidx, N_GROUPS), axis=0, keepdims=True)
        hit = gidx == f
        chosen = jnp.where(hit, 1.0, chosen)
        gs = jnp.where(hit, NEG_INF, gs)
    cand = jnp.where(chosen > 0.0, sel, NEG_INF)
    picked = jnp.zeros((e, tm), F32)
    idx, wts = [], []
    for _ in range(TOP_K):
        mx = jnp.max(cand, axis=0, keepdims=True)
        f = jnp.min(jnp.where(cand == mx, eidx, e), axis=0, keepdims=True)
        hit = eidx == f
        idx.append(f)
        wts.append(jnp.sum(jnp.where(hit, scores, 0.0), axis=0, keepdims=True))
        cand = jnp.where(hit, NEG_INF, cand)
        picked = jnp.where(hit, 1.0, picked)
    return idx, wts, picked


def _post_kernel(x_ref, a_ref, wo_ref, gm_ref, gn_ref, sh_ref, sc_ref, gf_ref, wr_ref, br_ref,
                 wsgu_ref, wsd_ref,
                 xs_ref, hp_ref, idx_ref, wt_ref, rank_ref, cnt_ref):
    tm, d = x_ref.shape

    x1 = x_ref[...] + gm_ref[0] * _dot_w(a_ref[...], wo_ref[...])
    h2 = _rms_mod(x1, gn_ref[...], sh_ref[0], sc_ref[0])
    hp_ref[...] = h2.astype(BF16)

    ff = wsd_ref.shape[0]
    gu = _dot(h2, wsgu_ref[...])
    shared = _dot(_silu(gu[:, :ff]) * gu[:, ff:], wsd_ref[...])
    xs_ref[...] = x1 + gf_ref[0] * shared

    logits_t = _dot_w_nt(wr_ref[...], h2)
    idx, wts, picked = _route(logits_t, br_ref[:, 0:1])
    e = picked.shape[0]
    eidx = lax.broadcasted_iota(I32, (e, tm), 0)
    before = (lax.broadcasted_iota(I32, (tm, tm), 0) < lax.broadcasted_iota(I32, (tm, tm), 1)).astype(BF16)
    rank_t = jnp.dot(picked.astype(BF16), before, preferred_element_type=F32)
    total = wts[0]
    for k in range(1, TOP_K):
        total = total + wts[k]
    for k in range(TOP_K):
        idx_ref[k:k + 1, :] = idx[k]
        wt_ref[k:k + 1, :] = wts[k] / total * ROUTED_SCALE
        rank_ref[k:k + 1, :] = jnp.sum(jnp.where(eidx == idx[k], rank_t, 0.0), axis=0, keepdims=True).astype(I32)
    cnt_ref[0] = jnp.broadcast_to(jnp.sum(picked, axis=1, keepdims=True), cnt_ref.shape[1:])


def _post(tok, x, a, w_o, g_m, g_norm, shift, scale, g_f, w_router_t, b_router, ws_gu, ws_d):
    n, d = x.shape
    e = w_router_t.shape[0]
    ff = ws_d.shape[0]
    gm, gm_spec = tok.mod(g_m)
    sh, sh_spec = tok.mod(shift)
    sc, sc_spec = tok.mod(scale)
    gf, gf_spec = tok.mod(g_f)
    cols = lambda rows: pl.BlockSpec((rows, tok.tm), lambda i: (0, i))
    return pl.pallas_call(
        _post_kernel,
        grid=(tok.n_tiles,),
        in_specs=[tok.rows(d), tok.rows(a.shape[1]), _full(w_o.shape), gm_spec, _full((1, d)), sh_spec, sc_spec,
                  gf_spec, _full((e, d)), _full((e, 128)), _full((d, 2 * ff)), _full((ff, d))],
        out_specs=[tok.rows(d), tok.rows(d), cols(TOP_K), cols(TOP_K), cols(TOP_K),
                   pl.BlockSpec((1, e, LANES), lambda i: (i, 0, 0))],
        out_shape=[jax.ShapeDtypeStruct((n, d), F32), jax.ShapeDtypeStruct((n, d), BF16),
                   jax.ShapeDtypeStruct((TOP_K, n), I32), jax.ShapeDtypeStruct((TOP_K, n), F32),
                   jax.ShapeDtypeStruct((TOP_K, n), I32), jax.ShapeDtypeStruct((tok.n_tiles, e, LANES), F32)],
        compiler_params=_cparams("arbitrary"),
        name="post",
    )(x, a, w_o, gm, g_norm.reshape(1, d), sh, sc, gf, w_router_t,
      jnp.broadcast_to(b_router.reshape(e, 1), (e, 128)), ws_gu, ws_d)


MOE_TILE = 256
RUN_ALIGN = 16
POS_CHUNK = 512
TILE_ROWS = -(-(MOE_TILE * TOP_K + N_EXPERTS * (RUN_ALIGN - 1)) // POS_CHUNK) * POS_CHUNK
TILE_PIECES = TILE_ROWS // RUN_ALIGN
ZERO_SLAB_ROWS = 256


def _tile_positions(idx_ref, rank_ref, tsrc_ref, tile):
    idx = idx_ref[...]

    def add(e, acc):
        return acc + jnp.where(idx == e, tsrc_ref[tile, e], 0)

    return lax.fori_loop(0, N_EXPERTS, add, rank_ref[...])


def _start_piece(tile, j, pdst_ref, make_copy):
    make_copy(pl.multiple_of(j * RUN_ALIGN, RUN_ALIGN),
              pl.multiple_of(pdst_ref[tile * TILE_PIECES + j], RUN_ALIGN), RUN_ALIGN).start()


def _start_pieces(tile, pdst_ref, npc_ref, make_copy, first=0):
    def body(j, c):
        _start_piece(tile, j, pdst_ref, make_copy)
        return c

    lax.fori_loop(first, npc_ref[tile], body, 0)


FULL_TILE_PIECES = MOE_TILE * TOP_K // RUN_ALIGN
CHUNK_PIECES = POS_CHUNK // RUN_ALIGN


def _wait_pieces(tile, npc_ref, make_copy):
    total = npc_ref[tile] * RUN_ALIGN
    chunk = 1 << (TILE_ROWS.bit_length() - 1)
    while chunk >= RUN_ALIGN:
        @pl.when((total & chunk) != 0)
        def _():
            make_copy(0, 0, chunk).wait()
        chunk //= 2


def _dispatch_kernel(pe_ref, de_ref, tsrc_ref, pdst_ref, npc_ref, idx_ref, rank_ref, hpp_ref, hps_ref,
                     xs_ref, srt, zeros, sem):
    i = pl.program_id(0)
    last = pl.num_programs(0) - 1
    slot = i % 2
    tm = hpp_ref.shape[0]
    zrows = zeros.shape[0]

    def tail_slabs(e, wait):
        for s in range(EXPERT_ROWS // zrows):
            @pl.when(pe_ref[e] - s * zrows > de_ref[e])
            def _():
                start = pl.multiple_of(pe_ref[e] - (s + 1) * zrows, zrows)
                cp = pltpu.make_async_copy(zeros, xs_ref.at[pl.ds(start, zrows)], sem.at[2])
                cp.wait() if wait else cp.start()

    @pl.when(i == 0)
    def _():
        zeros[...] = jnp.zeros_like(zeros)

        def zstart(e, c):
            tail_slabs(e, False)
            return c

        def zwait(e, c):
            tail_slabs(e, True)
            return c

        lax.fori_loop(0, N_EXPERTS, zstart, 0)
        lax.fori_loop(0, N_EXPERTS, zwait, 0)

    pos = _tile_positions(idx_ref, rank_ref, tsrc_ref, i)
    x = jnp.where(i == last, hps_ref[...], hpp_ref[...])
    def sort_chunk(c):
        p_iota = c * POS_CHUNK + lax.broadcasted_iota(I32, (POS_CHUNK, tm), 0)
        hit = pos[0:1, :] == p_iota
        for k in range(1, TOP_K):
            hit = jnp.logical_or(hit, pos[k:k + 1, :] == p_iota)
        onehot = jnp.where(hit, 1.0, 0.0).astype(BF16)
        srt[slot, c * POS_CHUNK:(c + 1) * POS_CHUNK, :] = jnp.dot(onehot, x,
                                                                  preferred_element_type=F32).astype(BF16)

    def copier(buf_slot):
        def copy(s, d, rows):
            return pltpu.make_async_copy(srt.at[buf_slot, pl.ds(s, rows)], xs_ref.at[pl.ds(d, rows)],
                                         sem.at[buf_slot])
        return copy

    def sort_tile(full):
        always = tm * TOP_K // POS_CHUNK if full else 0
        for c in range(TILE_ROWS // POS_CHUNK):
            if c < always:
                sort_chunk(c)
                for j in range(c * CHUNK_PIECES, (c + 1) * CHUNK_PIECES):
                    _start_piece(i, j, pdst_ref, copier(slot))
            else:
                pl.when(npc_ref[i] * RUN_ALIGN > c * POS_CHUNK)(functools.partial(sort_chunk, c))
        _start_pieces(i, pdst_ref, npc_ref, copier(slot), first=FULL_TILE_PIECES if full else 0)

    pl.when(i < last)(functools.partial(sort_tile, True))
    pl.when(i == last)(functools.partial(sort_tile, False))

    @pl.when(i > 0)
    def _():
        _wait_pieces(i - 1, npc_ref, copier(1 - slot))

    @pl.when(i == last)
    def _():
        _wait_pieces(i, npc_ref, copier(slot))


def _dispatch(hp_p, hp_s, idx_all, rank_all, pad_end, data_end, tsrc, pdst, npc, n_rows):
    tm = MOE_TILE
    d = hp_p.shape[1]
    n_tiles = tsrc.shape[0]
    cols = pl.BlockSpec((TOP_K, tm), lambda i, *_: (0, i))
    return pl.pallas_call(
        _dispatch_kernel,
        grid_spec=pltpu.PrefetchScalarGridSpec(
            num_scalar_prefetch=5, grid=(n_tiles,),
            in_specs=[cols, cols,
                      pl.BlockSpec((tm, d), lambda i, *_: (jnp.minimum(i, n_tiles - 2), 0)),
                      pl.BlockSpec((tm, d), lambda i, *_: (0, 0))],
            out_specs=pl.BlockSpec(memory_space=pl.ANY),
            scratch_shapes=[pltpu.VMEM((2, TILE_ROWS, d), BF16), pltpu.VMEM((ZERO_SLAB_ROWS, d), BF16),
                            pltpu.SemaphoreType.DMA((3,))]),
        out_shape=jax.ShapeDtypeStruct((n_rows, d), BF16),
        compiler_params=_cparams("arbitrary"),
        name="dispatch",
    )(pad_end, data_end, tsrc, pdst, npc, idx_all, rank_all, hp_p, hp_s)


def _expert_kernel(be_ref, first_ref, nu_ref, x_ref, wg_ref, wu_ref, wd_ref, o_ref, wgu_s, wd_s):
    i = pl.program_id(0)
    ff = wd_s.shape[0]

    @pl.when(i < nu_ref[0])
    def _():
        @pl.when(first_ref[i] == 1)
        def _():
            wgu_s[:, :ff] = wg_ref[...].astype(BF16)
            wgu_s[:, ff:] = wu_ref[...].astype(BF16)
            wd_s[...] = wd_ref[...].astype(BF16)

        gu = jnp.dot(x_ref[...], wgu_s[...], preferred_element_type=F32)
        act = _silu(gu[:, :ff]) * gu[:, ff:]
        o_ref[...] = jnp.dot(act.astype(BF16), wd_s[...], preferred_element_type=F32).astype(o_ref.dtype)


def _experts(xs, block_e, first, n_used, w_gate, w_up, w_down, layer):
    n_rows, d = xs.shape
    ff = w_gate.shape[-1]
    bm = EXPERT_ROWS
    grid_spec = pltpu.PrefetchScalarGridSpec(
        num_scalar_prefetch=3,
        grid=(n_rows // bm,),
        in_specs=[pl.BlockSpec((bm, d), lambda i, be, fi, nu: (jnp.minimum(i, nu[0] - 1), 0)),
                  pl.BlockSpec((None, None, d, ff), lambda i, be, fi, nu: (layer, be[i], 0, 0)),
                  pl.BlockSpec((None, None, d, ff), lambda i, be, fi, nu: (layer, be[i], 0, 0)),
                  pl.BlockSpec((None, None, ff, d), lambda i, be, fi, nu: (layer, be[i], 0, 0))],
        out_specs=pl.BlockSpec((bm, d), lambda i, be, fi, nu: (jnp.minimum(i, nu[0] - 1), 0)),
        scratch_shapes=[pltpu.VMEM((d, 2 * ff), BF16), pltpu.VMEM((ff, d), BF16)],
    )
    return pl.pallas_call(
        _expert_kernel,
        grid_spec=grid_spec,
        out_shape=jax.ShapeDtypeStruct((n_rows, d), BF16),
        compiler_params=_cparams("arbitrary"),
        name="experts",
    )(block_e, first, n_used, xs, w_gate, w_up, w_down)


def _combine_kernel(tsrc_ref, pdst_ref, npc_ref, idx_ref, rank_ref, w_ref, xs_ref, gf_ref, gfin_ref, os_ref,
                    y_ref, buf, sem, *, tile0, clear, final_norm):
    i = pl.program_id(0)
    tile = tile0 + i
    slot = i % 2
    tm, d = xs_ref.shape

    def copier(buf_slot):
        def copy(s, d, rows):
            return pltpu.make_async_copy(os_ref.at[pl.ds(d, rows)], buf.at[buf_slot, pl.ds(s, rows)],
                                         sem.at[buf_slot])
        return copy

    @pl.when(i == 0)
    def _():
        first_free = 0 if clear else tm * TOP_K
        buf[:, first_free:, :] = jnp.zeros((2, buf.shape[1] - first_free, d), buf.dtype)
        _start_pieces(tile, pdst_ref, npc_ref, copier(0))

    full = not clear
    n_steps = pl.num_programs(0)
    nxt = jnp.minimum(tile + 1, tile0 + n_steps - 1)

    pos = _tile_positions(idx_ref, rank_ref, tsrc_ref, tile).astype(F32)
    w = w_ref[...]
    pos_c = [_row_to_col(pos[k:k + 1, :]) for k in range(TOP_K)]
    w_c = [_row_to_col(w[k:k + 1, :]) for k in range(TOP_K)]
    _wait_pieces(tile, npc_ref, copier(slot))

    def chunk_sum(c):
        p_iota = (c * POS_CHUNK + lax.broadcasted_iota(I32, (tm, POS_CHUNK), 1)).astype(F32)
        wmat = jnp.where(pos_c[0] == p_iota, w_c[0], 0.0)
        for k in range(1, TOP_K):
            wmat = jnp.where(pos_c[k] == p_iota, w_c[k], wmat)
        return _dot(wmat, buf[slot, c * POS_CHUNK:(c + 1) * POS_CHUNK, :])

    always = tm * TOP_K // POS_CHUNK
    acc = None
    for c in range(always):
        if full:
            for j in range(c * CHUNK_PIECES, (c + 1) * CHUNK_PIECES):
                _start_piece(nxt, j, pdst_ref, copier(1 - slot))
        part = chunk_sum(c)
        acc = part if acc is None else acc + part
    if full:
        _start_pieces(nxt, pdst_ref, npc_ref, copier(1 - slot), first=FULL_TILE_PIECES)

        @pl.when(i == n_steps - 1)
        def _():
            _wait_pieces(nxt, npc_ref, copier(1 - slot))
    y_ref[...] = acc
    for c in range(always, TILE_ROWS // POS_CHUNK):
        @pl.when(npc_ref[tile] * RUN_ALIGN > c * POS_CHUNK)
        def _():
            y_ref[...] += chunk_sum(c)
    x2 = xs_ref[...] + gf_ref[0] * y_ref[...]
    y_ref[...] = _rms(x2, gfin_ref[...]) if final_norm else x2


def _combine(tok, tile0, xs, g_f, g_final, idx_all, rank_all, wt_all, out_sorted, tsrc, pdst, npc, final_norm,
             clear):
    n, d = xs.shape
    gf, gf_spec = tok.mod(g_f)
    tm = MOE_TILE
    cols = pl.BlockSpec((TOP_K, tm), lambda i, *_: (0, tile0 + i))
    return pl.pallas_call(
        functools.partial(_combine_kernel, tile0=tile0, clear=clear, final_norm=final_norm),
        grid_spec=pltpu.PrefetchScalarGridSpec(
            num_scalar_prefetch=3, grid=(tok.n_tiles,),
            in_specs=[cols, cols, cols, tok.rows(d), gf_spec, _full((1, d)), pl.BlockSpec(memory_space=pl.ANY)],
            out_specs=tok.rows(d),
            scratch_shapes=[pltpu.VMEM((2, TILE_ROWS, d), BF16), pltpu.SemaphoreType.DMA((2,))]),
        out_shape=jax.ShapeDtypeStruct((n, d), F32),
        compiler_params=_cparams("arbitrary"),
        name="combine",
    )(tsrc, pdst, npc, idx_all, rank_all, wt_all, xs, gf, g_final.reshape(1, d), out_sorted)


def _rope(x, cos2, sin2):
    half = x.shape[1] // 2
    swapped = jnp.concatenate([x[:, half:], x[:, :half]], axis=1)
    return x * cos2 + swapped * sin2


def _in1_kernel(x_ref, shk_ref, sck_ref, gk_ref, wdkv_ref, glat_ref, cos_ref, sin_ref,
                shm_ref, scm_ref, gm_ref, wdq_ref, gq_ref, wqn_ref, wqr_ref, wuk_ref,
                lat_ref, kr_ref, latb_ref, krb_ref, qa_ref, qr_ref):
    x = x_ref[...]
    cos2, sin2 = cos_ref[...], sin_ref[...]
    att_scale = (NOPE_DIM + ROPE_DIM) ** -0.5

    hk = _rms_mod(x, gk_ref[...], shk_ref[0], sck_ref[0])
    ckr = _dot_w(hk, wdkv_ref[...])
    lat = _rms(ckr[:, :KV_LORA], glat_ref[...])
    kr = _rope(ckr[:, KV_LORA:], cos2, sin2)
    lat_ref[...] = lat
    kr_ref[...] = kr
    latb_ref[...] = lat.astype(BF16)
    krb_ref[...] = kr.astype(BF16)

    hm = _rms_mod(x, gm_ref[...], shm_ref[0], scm_ref[0])
    q_lat = _rms(_dot_w(hm, wdq_ref[...]), gq_ref[...])
    q_nope = _dot_w(q_lat, wqn_ref[...])
    q_rope = _dot_w(q_lat, wqr_ref[...])
    for h in range(MLA_HEADS):
        qa = _dot_w(q_nope[:, h * NOPE_DIM:(h + 1) * NOPE_DIM], wuk_ref[h])
        qa_ref[h] = (qa * att_scale).astype(BF16)
        qr = _rope(q_rope[:, h * ROPE_DIM:(h + 1) * ROPE_DIM], cos2, sin2)
        qr_ref[h] = (qr * att_scale).astype(BF16)


def _in1(tok, x, sh_kv, sc_kv, g_kv_in, w_dkv, g_kv_lat, cos2, sin2, sh_m, sc_m, g_mix, w_dq, g_q_lat,
         w_q_nope, w_q_rope, w_uk_t):
    n, d = x.shape
    q_lora = w_dq.shape[1]
    shk, shk_spec = tok.mod(sh_kv)
    sck, sck_spec = tok.mod(sc_kv)
    shm, shm_spec = tok.mod(sh_m)
    scm, scm_spec = tok.mod(sc_m)
    heads = lambda w: pl.BlockSpec((MLA_HEADS, tok.tm, w), lambda i: (0, i, 0))
    return pl.pallas_call(
        _in1_kernel,
        grid=(tok.n_tiles,),
        in_specs=[tok.rows(d), shk_spec, sck_spec, _full((1, d)), _full(w_dkv.shape), _full((1, KV_LORA)),
                  tok.seq_rows(ROPE_DIM), tok.seq_rows(ROPE_DIM),
                  shm_spec, scm_spec, _full((1, d)), _full(w_dq.shape), _full((1, q_lora)),
                  _full(w_q_nope.shape), _full(w_q_rope.shape), _full(w_uk_t.shape)],
        out_specs=[tok.rows(KV_LORA), tok.rows(ROPE_DIM), tok.rows(KV_LORA), tok.rows(ROPE_DIM),
                   heads(KV_LORA), heads(ROPE_DIM)],
        out_shape=[jax.ShapeDtypeStruct((n, KV_LORA), F32), jax.ShapeDtypeStruct((n, ROPE_DIM), F32),
                   jax.ShapeDtypeStruct((n, KV_LORA), BF16), jax.ShapeDtypeStruct((n, ROPE_DIM), BF16),
                   jax.ShapeDtypeStruct((MLA_HEADS, n, KV_LORA), BF16),
                   jax.ShapeDtypeStruct((MLA_HEADS, n, ROPE_DIM), BF16)],
        compiler_params=_cparams("arbitrary"),
        name="in1",
    )(x, shk, sck, g_kv_in.reshape(1, d), w_dkv, g_kv_lat.reshape(1, KV_LORA), cos2, sin2,
      shm, scm, g_mix.reshape(1, d), w_dq, g_q_lat.reshape(1, q_lora), w_q_nope, w_q_rope, w_uk_t)


def _attn_prompt_kernel(qa_ref, qr_ref, lat_ref, kr_ref, wuv_ref, o_ref, m_s, l_s, acc_s):
    heads, tq, c = qa_ref.shape
    qi = pl.program_id(1)
    n_lt = tq // LANES
    m_s[...] = jnp.full(m_s.shape, NEG_INF, F32)
    l_s[...] = jnp.zeros(l_s.shape, F32)
    acc_s[...] = jnp.zeros(acc_s.shape, F32)
    causal = lax.broadcasted_iota(I32, (tq, tq), 1) <= lax.broadcasted_iota(I32, (tq, tq), 0)

    def chunk(j, diagonal):
        start = pl.multiple_of(j * tq, tq)
        kc = lat_ref[pl.ds(start, tq), :]
        krc = kr_ref[pl.ds(start, tq), :]
        def scores(h):
            return _dot_nt(qa_ref[h], kc) + _dot_nt(qr_ref[h], krc)

        s_next = scores(0)
        for h in range(heads):
            s, s_next = s_next, (scores(h + 1) if h + 1 < heads else None)
            if diagonal:
                s = jnp.where(causal, s, NEG_INF)
            tiles = [s[:, t * LANES:(t + 1) * LANES] for t in range(n_lt)]
            mx = tiles[0]
            for t in tiles[1:]:
                mx = jnp.maximum(mx, t)
            m_old = m_s[h]
            m_new = jnp.maximum(m_old, jnp.broadcast_to(jnp.max(mx, axis=1, keepdims=True), (tq, LANES)))
            alpha = jnp.exp(m_old - m_new)
            ps = [jnp.exp(t - m_new) for t in tiles]
            part = ps[0]
            for p in ps[1:]:
                part = part + p
            l_s[h] = alpha * l_s[h] + part
            pv = _dot(jnp.concatenate(ps, axis=1), kc)
            acc_s[h] = jnp.concatenate([alpha] * (c // LANES), axis=1) * acc_s[h] + pv
            m_s[h] = m_new

    def step(j, carry):
        chunk(j, False)
        return carry

    lax.fori_loop(0, qi, step, 0)
    chunk(qi, True)
    for h in range(heads):
        o_lat = acc_s[h] / jnp.sum(l_s[h], axis=1, keepdims=True)
        o_ref[:, h * V_DIM:(h + 1) * V_DIM] = _dot(o_lat, wuv_ref[h]).astype(o_ref.dtype)


def _attn_prompt(qa, qr, latb, krb, w_uv_t, batch, seq, tq=256):
    tq = min(tq, seq)
    nq = seq // tq
    return pl.pallas_call(
        _attn_prompt_kernel,
        grid=(batch, nq),
        in_specs=[pl.BlockSpec((MLA_HEADS, tq, KV_LORA), lambda b, i: (0, b * nq + i, 0)),
                  pl.BlockSpec((MLA_HEADS, tq, ROPE_DIM), lambda b, i: (0, b * nq + i, 0)),
                  pl.BlockSpec((seq, KV_LORA), lambda b, i: (b, 0)),
                  pl.BlockSpec((seq, ROPE_DIM), lambda b, i: (b, 0)),
                  pl.BlockSpec(w_uv_t.shape, lambda b, i: (0, 0, 0))],
        out_specs=pl.BlockSpec((tq, MLA_HEADS * V_DIM), lambda b, i: (b * nq + i, 0)),
        out_shape=jax.ShapeDtypeStruct((batch * seq, MLA_HEADS * V_DIM), BF16),
        scratch_shapes=[pltpu.VMEM((MLA_HEADS, tq, LANES), F32), pltpu.VMEM((MLA_HEADS, tq, LANES), F32),
                        pltpu.VMEM((MLA_HEADS, tq, KV_LORA), F32)],
        compiler_params=_cparams("arbitrary", "arbitrary"),
        name="attn_prompt",
    )(qa, qr, latb, krb, w_uv_t)


def _attn_decode_kernel(pt_ref, qa_ref, qr_ref, lat1_ref, kr1_ref, wuv_ref, *rest, pages):
    lat_refs = rest[:pages]
    kr_refs = rest[pages:2 * pages]
    o_ref, m_s, l_s, acc_s = rest[2 * pages:]
    j = pl.program_id(1)
    qa = qa_ref[0]
    qr = qr_ref[0]

    @pl.when(j == 0)
    def _():
        lat1 = lat1_ref[0].astype(F32)
        m_s[...] = (jnp.sum(qa.astype(F32) * lat1, axis=1, keepdims=True)
                    + jnp.sum(qr.astype(F32) * kr1_ref[0].astype(F32), axis=1, keepdims=True))
        l_s[...] = jnp.ones(l_s.shape, F32)
        acc_s[...] = jnp.broadcast_to(lat1, acc_s.shape)

    lats = [r[0].astype(BF16) for r in lat_refs]
    s = jnp.concatenate([_dot_nt(qa, lats[p]) + _dot(qr, kr_refs[p][0]) for p in range(pages)], axis=1)
    m_old = m_s[...]
    m_new = jnp.maximum(m_old, jnp.max(s, axis=1, keepdims=True))
    alpha = jnp.exp(m_old - m_new)
    p_all = jnp.exp(s - m_new)
    l_s[...] = alpha * l_s[...] + jnp.sum(p_all, axis=1, keepdims=True)
    page = lats[0].shape[0]
    pv = _dot(p_all[:, :page], lats[0])
    for p in range(1, pages):
        pv = pv + _dot(p_all[:, p * page:(p + 1) * page], lats[p])
    acc_s[...] = alpha * acc_s[...] + pv
    m_s[...] = m_new

    @pl.when(j == pl.num_programs(1) - 1)
    def _():
        o_all = _dot(acc_s[...] / l_s[...], wuv_ref[...])
        for h in range(o_all.shape[0]):
            o_ref[0, :, h * V_DIM:(h + 1) * V_DIM] = o_all[h:h + 1, h * V_DIM:(h + 1) * V_DIM].astype(o_ref.dtype)


def _attn_decode(qa, qr, lat1, kr1, w_uv_flat, cache_latent, cache_krope_t, page_table, pages=64):
    batch, n_pages = page_table.shape
    pages = min(pages, n_pages)
    page = cache_latent.shape[1]
    steps = n_pages // pages
    one = lambda shape: pl.BlockSpec((1,) + shape, lambda b, j, pt: (b,) + (0,) * len(shape))
    paged = lambda w, p: pl.BlockSpec((1, page, w) if w == KV_LORA else (1, w, page),
                                      lambda b, j, pt: (pt[b, j * pages + p], 0, 0))
    grid_spec = pltpu.PrefetchScalarGridSpec(
        num_scalar_prefetch=1,
        grid=(batch, steps),
        in_specs=[one((MLA_HEADS, KV_LORA)), one((MLA_HEADS, ROPE_DIM)), one((1, KV_LORA)), one((1, ROPE_DIM)),
                  pl.BlockSpec(w_uv_flat.shape, lambda b, j, pt: (0, 0))]
                 + [paged(KV_LORA, p) for p in range(pages)] + [paged(ROPE_DIM, p) for p in range(pages)],
        out_specs=one((1, MLA_HEADS * V_DIM)),
        scratch_shapes=[pltpu.VMEM((MLA_HEADS, 1), F32), pltpu.VMEM((MLA_HEADS, 1), F32),
                        pltpu.VMEM((MLA_HEADS, KV_LORA), F32)],
    )
    return pl.pallas_call(
        functools.partial(_attn_decode_kernel, pages=pages),
        grid_spec=grid_spec,
        out_shape=jax.ShapeDtypeStruct((batch, 1, MLA_HEADS * V_DIM), BF16),
        compiler_params=_cparams("arbitrary", "arbitrary"),
        name="attn_decode",
    )(page_table, qa, qr, lat1, kr1, w_uv_flat, *([cache_latent] * pages), *([cache_krope_t] * pages))


def _rope_tables(pos):
    half = ROPE_DIM // 2
    inv = ROPE_THETA ** (-jnp.arange(half, dtype=F32) / half)
    ang = pos.astype(F32)[:, None] * inv[None, :]
    cos, sin = jnp.cos(ang), jnp.sin(ang)
    return jnp.concatenate([cos, cos], axis=1), jnp.concatenate([-sin, sin], axis=1)


def _routing_tables(tile_counts, n_rows):
    bm = EXPERT_ROWS
    tile_counts = (tile_counts + RUN_ALIGN - 1) // RUN_ALIGN * RUN_ALIGN
    counts = jnp.sum(tile_counts, axis=0)
    padded = (counts + bm - 1) // bm * bm
    pad_end = jnp.cumsum(padded).astype(I32)
    pad_start = pad_end - padded
    tile_dst = pad_start[None, :] + jnp.cumsum(tile_counts, axis=0) - tile_counts
    tile_src = jnp.cumsum(tile_counts, axis=1) - tile_counts
    run_end = tile_src + tile_counts
    piece_row = jnp.arange(TILE_PIECES, dtype=I32) * RUN_ALIGN
    piece_e = jnp.sum((run_end[:, None, :] <= piece_row[None, :, None]).astype(I32), axis=2)
    owner = (piece_e[:, :, None] == jnp.arange(N_EXPERTS, dtype=I32)[None, None, :]).astype(I32)
    piece_dst = jnp.sum(owner * (tile_dst - tile_src)[:, None, :], axis=2) + piece_row[None, :]
    n_pieces = run_end[:, -1] // RUN_ALIGN
    block_start = jnp.arange(n_rows // bm, dtype=I32) * bm
    block_e = jnp.sum((pad_end[None, :] <= block_start[:, None]).astype(I32), axis=1)
    block_e = jnp.minimum(block_e, N_EXPERTS - 1)
    first = jnp.concatenate([jnp.ones((1,), I32), (block_e[1:] != block_e[:-1]).astype(I32)])
    n_used = pad_end[-1:] // bm
    data_end = (pad_start + counts).astype(I32)
    return (pad_end, data_end, tile_src.astype(I32), piece_dst.reshape(-1).astype(I32), n_pieces.astype(I32),
            block_e, first, n_used)


def kernel(x_prompt, x_sample, state_mlstm_C, state_mlstm_n, state_mlstm_m, cache_latent, cache_krope, page_table, c_prompt, c_sample, w_ada, b_ada, g_mix, g_ffn, w_mlstm_in, b_mlstm_gates, g_mlstm_head, w_mlstm_out, w_ada_kv, b_ada_kv, g_kv_in, w_dkv, g_kv_lat, w_uk, w_uv, w_dq, g_q_lat, w_uq, w_mla_out, w_router, b_router, w_exp_gate, w_exp_up, w_exp_down, w_sh_gate, w_sh_up, w_sh_down, g_final):
    bp, tp, d = x_prompt.shape
    bs, ts, _ = x_sample.shape
    depth = w_ada.shape[0]
    n_a = w_mlstm_in.shape[0]
    past_len = page_table.shape[1] * cache_latent.shape[1]
    hk, hv = ML_HEADS * ML_DK, ML_HEADS * ML_DV
    n_p, n_s = bp * tp, bs * ts
    n_tot = n_p + n_s
    n_moe_tiles = n_p // MOE_TILE + 1
    n_rows = n_tot * TOP_K + n_moe_tiles * N_EXPERTS * (RUN_ALIGN - 1) + N_EXPERTS * (EXPERT_ROWS - 1)
    n_rows = -(-n_rows // EXPERT_ROWS) * EXPERT_ROWS

    groups = [
        dict(batch=bp, seq=tp, tok=_Tok(bp, tp, 256), x=x_prompt.reshape(n_p, d), pos0=0,
             C0=jnp.zeros((n_a, bp, ML_HEADS, ML_DK, ML_DV), F32), n0=jnp.zeros((n_a, bp, ML_HEADS, ML_DK), F32),
             m0=jnp.full((n_a, bp, ML_HEADS), M_INIT, F32)),
        dict(batch=bs, seq=ts, tok=_Tok(bs, ts, 256), x=x_sample.reshape(n_s, d), pos0=past_len,
             C0=state_mlstm_C, n0=state_mlstm_n, m0=state_mlstm_m),
    ]
    c_all = jnp.concatenate([c_prompt, c_sample], axis=0)
    row0 = [0, bp]
    for g in groups:
        g["Cs"], g["ns"], g["ms"] = [], [], []

    for g in groups:
        g["cos2"], g["sin2"] = _rope_tables(g["pos0"] + jnp.arange(g["seq"], dtype=I32))
        if g["seq"] == 1:
            g["cos2"] = jnp.broadcast_to(g["cos2"], (g["batch"], ROPE_DIM))
            g["sin2"] = jnp.broadcast_to(g["sin2"], (g["batch"], ROPE_DIM))

    mod_kv = _ada(c_all, w_ada_kv, b_ada_kv)
    w_uk_t = jnp.transpose(w_uk, (1, 2, 0))
    w_uv_t = jnp.transpose(w_uv, (1, 0, 2)).astype(BF16)

    for l in range(depth):
        mod = _ada(c_all, w_ada, b_ada[l], layer=l)
        w_router_t = w_router[l].T
        ws_gu = jnp.concatenate([w_sh_gate[l], w_sh_up[l]], axis=1).astype(BF16)
        ws_d = w_sh_down[l].astype(BF16)
        for gi, g in enumerate(groups):
            r0, nb = row0[gi], g["batch"]
            sh_m, sc_m, g_m, sh_f, sc_f, g_f = [mod[r0:r0 + nb, i * d:(i + 1) * d] for i in range(6)]
            g["g_f"] = g_f
            tok = g["tok"]
            wdt = F32 if g["seq"] == 1 else BF16
            if l < n_a:
                w_in = w_mlstm_in[l]
                w_main = w_in[:, :2 * hk + 2 * hv].astype(wdt)
                w_gates = jnp.pad(w_in[:, 2 * hk + 2 * hv:], ((0, 0), (0, 128 - 2 * ML_HEADS)))
                b_gates = jnp.pad(b_mlstm_gates[l], (0, 128 - 2 * ML_HEADS)).reshape(1, 128)
                if g["seq"] == 1:
                    proj, gates = _in0(tok, g["x"], sh_m, sc_m, g_mix[l], w_main, w_gates, b_gates, F32)
                    a, Cn, nn, mn = _mlstm_step(proj, gates, g["C0"][l], g["n0"][l], g["m0"][l],
                                                g_mlstm_head[l], nb)
                    a = a.reshape(nb, hv)
                else:
                    proj, gates = _in0(tok, g["x"], sh_m, sc_m, g_mix[l], w_main, w_gates, b_gates, BF16)
                    a, Cn, nn, mn = _mlstm_chunks(proj, gates, g["C0"][l], g["n0"][l], g["m0"][l],
                                                  g_mlstm_head[l], nb, g["seq"])
                g["Cs"].append(Cn)
                g["ns"].append(nn)
                g["ms"].append(mn.reshape(nb, ML_HEADS))
                w_o = w_mlstm_out[l].astype(wdt)
            else:
                j = l - n_a
                w_q = w_uq[j].reshape(-1, MLA_HEADS, NOPE_DIM + ROPE_DIM)
                w_q_nope = w_q[:, :, :NOPE_DIM].reshape(-1, MLA_HEADS * NOPE_DIM).astype(wdt)
                w_q_rope = w_q[:, :, NOPE_DIM:].reshape(-1, MLA_HEADS * ROPE_DIM).astype(wdt)
                sh_kv, sc_kv = mod_kv[r0:r0 + nb, :d], mod_kv[r0:r0 + nb, d:]
                lat, kr, latb, krb, qa, qr = _in1(tok, g["x"], sh_kv, sc_kv, g_kv_in, w_dkv.astype(wdt), g_kv_lat,
                                                  g["cos2"], g["sin2"], sh_m, sc_m, g_mix[l], w_dq[j].astype(wdt),
                                                  g_q_lat[j], w_q_nope, w_q_rope, w_uk_t.astype(wdt))
                if l == n_a:
                    g["lat"], g["kr"], g["latb"], g["krb"] = lat, kr, latb, krb
                if g["seq"] == 1:
                    a = _attn_decode(jnp.transpose(qa, (1, 0, 2)), jnp.transpose(qr, (1, 0, 2)),
                                     g["latb"].reshape(nb, 1, KV_LORA), g["krb"].reshape(nb, 1, ROPE_DIM),
                                     w_uv.reshape(KV_LORA, MLA_HEADS * V_DIM).astype(BF16),
                                     cache_latent, jnp.swapaxes(cache_krope, 1, 2),
                                     page_table).reshape(nb, MLA_HEADS * V_DIM)
                else:
                    a = _attn_prompt(qa, qr, g["latb"], g["krb"], w_uv_t, nb, g["seq"])
                w_o = w_mla_out[j].astype(wdt)
            g["xs"], g["hp"], g["idx"], g["wt"], g["rank"], g["cnt"] = _post(
                tok, g["x"], a, w_o, g_m, g_ffn[l], sh_f, sc_f, g_f, w_router_t.astype(wdt), b_router[l], ws_gu, ws_d)

        gp, gs = groups
        pad_t = MOE_TILE - n_s
        lanes = lambda v, fill: jnp.concatenate(
            [gp[v], jnp.pad(gs[v], ((0, 0), (0, pad_t)), constant_values=fill)], axis=1)
        idx_all, wt_all = lanes("idx", 0), lanes("wt", 0)
        rank_all = lanes("rank", TILE_ROWS)
        tile_counts = jnp.concatenate([gp["cnt"][:, :, 0], gs["cnt"][:, :, 0]], axis=0).astype(I32)
        pad_end, data_end, tsrc, pdst, npc, block_e, first, n_used = _routing_tables(tile_counts, n_rows)
        rows = lambda v: jnp.pad(v, ((0, pad_t), (0, 0)))
        xs_sorted = _dispatch(gp["hp"], rows(gs["hp"]), idx_all, rank_all, pad_end, data_end, tsrc, pdst, npc, n_rows)
        out_sorted = _experts(xs_sorted, block_e, first, n_used, w_exp_gate, w_exp_up, w_exp_down, l)
        last = l == depth - 1
        gp["x"] = _combine(gp["tok"], 0, gp["xs"], gp["g_f"], g_final, idx_all, rank_all, wt_all, out_sorted,
                           tsrc, pdst, npc, last, clear=False)
        gs["x"] = _combine(_Tok(MOE_TILE, 1, MOE_TILE), gp["tok"].n_tiles, rows(gs["xs"]), rows(gs["g_f"]), g_final,
                           idx_all, rank_all, wt_all, out_sorted, tsrc, pdst, npc, last, clear=True)[:n_s]

    outs = []
    for g in groups:
        nb, seq = g["batch"], g["seq"]
        outs.append((g["x"].reshape(nb, seq, d), jnp.stack(g["Cs"]), jnp.stack(g["ns"]), jnp.stack(g["ms"]),
                     g["lat"].reshape(nb, seq, KV_LORA), g["kr"].reshape(nb, seq, ROPE_DIM)))
    p, s = outs
    return (p[0], s[0], p[1], p[2], p[3], p[4], p[5], s[1], s[2], s[3], s[4], s[5])
```
